```python
import functools
import jax, jax.numpy as jnp
from jax import lax
import numpy as np

D_MODEL = 4096
BATCH = 4
SEQ = 2048
DEPTH = 1
DEC_BATCH = 32
DEC_SEQ = 4
PAST_LEN = 8192
PAGE_SIZE = 128

D_CONV = D_MODEL // 2
CONV_W = 3
N_HEADS = 16
HEAD_DIM = 128
N_KV_HEADS = 4
KV_GROUP = N_HEADS // N_KV_HEADS
D_ATTN = N_HEADS * HEAD_DIM
N_IDX_HEADS = 16
IDX_DIM = 64
TOPK_MAX = 256
D_FF = 4 * D_MODEL
ROPE_THETA = 10000.0
NORM_EPS = 1e-6
Q_BLOCK = 128
N_MOD = 6
COL_SIZES = (D_CONV, D_CONV, D_CONV,
             D_ATTN, N_KV_HEADS * HEAD_DIM, N_KV_HEADS * HEAD_DIM,
             N_IDX_HEADS * IDX_DIM, IDX_DIM, N_IDX_HEADS,
             D_MODEL, D_MODEL)
D_IN = 3 * D_CONV + D_ATTN + 2 * N_KV_HEADS * HEAD_DIM + N_IDX_HEADS * IDX_DIM + IDX_DIM + N_IDX_HEADS + 2 * D_MODEL

kernel_name = "hybrid_conv_dsa_adaln_decode_step"


def rmsnorm(x, g):
    xf = x.astype(jnp.float32)
    y = xf * lax.rsqrt(jnp.mean(xf * xf, axis=-1, keepdims=True) + NORM_EPS)
    return (y * g.astype(jnp.float32)).astype(x.dtype)


def rope(x, pos):
    half = x.shape[-1] // 2
    inv = ROPE_THETA ** (-jnp.arange(half, dtype=jnp.float32) / half)
    ang = pos.astype(jnp.float32)[:, None] * inv[None, :]
    cos = jnp.cos(ang)[:, None, :]
    sin = jnp.sin(ang)[:, None, :]
    x1 = x[..., :half].astype(jnp.float32)
    x2 = x[..., half:].astype(jnp.float32)
    out = jnp.concatenate([x1 * cos - x2 * sin, x2 * cos + x1 * sin], axis=-1)
    return out.astype(x.dtype)


def split_cols(z):
    offs = []
    acc = 0
    for s in COL_SIZES[:-1]:
        acc += s
        offs.append(acc)
    return jnp.split(z, offs, axis=-1)


def modulations(c, w_ada, b_ada):
    m = jax.nn.silu(c) @ w_ada + b_ada
    return jnp.split(m[:, None, :], N_MOD, axis=-1)


def short_conv(u, buf, conv_w):
    t = u.shape[1]
    ext = jnp.concatenate([buf, u], axis=1)
    y = ext[:, 0:t] * conv_w[0]
    for j in range(1, CONV_W):
        y = y + ext[:, j:j + t] * conv_w[j]
    return y, ext[:, -(CONV_W - 1):]


def gather_rows(a, idx):
    return jax.vmap(lambda ab, ib: ab[ib])(a, idx)


def indexer_select(qi, wi, qpos, ki, k_top):
    s = jnp.einsum('bthd,bsd->bths', qi, ki, preferred_element_type=jnp.float32) * (IDX_DIM ** -0.5)
    score = jnp.einsum('bth,bths->bts', wi.astype(jnp.float32), jax.nn.relu(s))
    kpos = jnp.arange(ki.shape[1])
    admissible = kpos[None, None, :] <= qpos[None, :, None]
    score = jnp.where(admissible, score, -jnp.inf)
    _, idx = lax.top_k(score, k_top)
    valid = idx <= qpos[None, :, None]
    return idx, valid


def sparse_attend(q, k_sel, v_sel, valid):
    b, t = q.shape[:2]
    qg = q.reshape(b, t, N_KV_HEADS, KV_GROUP, HEAD_DIM)
    s = jnp.einsum('btgrd,btkgd->btgrk', qg, k_sel, preferred_element_type=jnp.float32) * (HEAD_DIM ** -0.5)
    s = jnp.where(valid[:, :, None, None, :], s, -jnp.inf)
    p = jax.nn.softmax(s, axis=-1)
    o = jnp.einsum('btgrk,btkgd->btgrd', p.astype(v_sel.dtype), v_sel)
    return o.reshape(b, t, D_ATTN)


def prompt_attention(q, k, v, qi, ki, wi):
    b, s_len = q.shape[:2]
    k_top = min(TOPK_MAX, s_len // 4)
    nb = s_len // Q_BLOCK

    def to_blocks(a):
        return jnp.swapaxes(a.reshape((b, nb, Q_BLOCK) + a.shape[2:]), 0, 1)

    def one_block(args):
        qb, qib, wib, qpos = args
        idx, valid = indexer_select(qib, wib, qpos, ki, k_top)
        return sparse_attend(qb, gather_rows(k, idx), gather_rows(v, idx), valid)

    pos = jnp.arange(s_len).reshape(nb, Q_BLOCK)
    out = lax.map(one_block, (to_blocks(q), to_blocks(qi), to_blocks(wi), pos))
    return jnp.swapaxes(out, 0, 1).reshape(b, s_len, D_ATTN)


def sample_attention(q, k, v, qi, ki, wi, cache_k, cache_v, cache_kidx, page_table):
    b, t = q.shape[:2]
    page = cache_k.shape[1]
    past = page_table.shape[1] * page
    k_top = min(TOPK_MAX, (past + t) // 4)
    ki_past = cache_kidx[page_table].reshape(b, past, IDX_DIM)
    ki_all = jnp.concatenate([ki_past, ki], axis=1)
    qpos = past + jnp.arange(t)
    idx, valid = indexer_select(qi, wi, qpos, ki_all, k_top)
    in_past = idx < past
    pidx = jnp.minimum(idx, past - 1)
    phys = jnp.take_along_axis(page_table, (pidx // page).reshape(b, -1), axis=1).reshape(idx.shape)
    off = pidx % page
    nidx = jnp.clip(idx - past, 0, t - 1)
    sel = in_past[..., None, None]
    k_sel = jnp.where(sel, cache_k[phys, off], gather_rows(k, nidx))
    v_sel = jnp.where(sel, cache_v[phys, off], gather_rows(v, nidx))
    return sparse_attend(q, k_sel, v_sel, valid)


def layer_block(x, c, pos, conv_buf, attention, w_ada, b_ada, g_mix, w_in, conv_w,
                w_conv_out, w_attn_out, w_out, g_ffn, w_up, w_down):
    b, t, _ = x.shape
    sh1, sc1, gt1, sh2, sc2, gt2 = modulations(c, w_ada, b_ada)
    h = rmsnorm(x, g_mix) * (1 + sc1) + sh1
    z = h @ w_in
    xin, bg, cg, q, k, v, qi, ki, wi, g_c, g_a = split_cols(z)
    u_conv, conv_buf_new = short_conv(cg * xin, conv_buf, conv_w)
    y_conv = (bg * u_conv) @ w_conv_out
    q = rope(q.reshape(b, t, N_HEADS, HEAD_DIM), pos)
    k = rope(k.reshape(b, t, N_KV_HEADS, HEAD_DIM), pos)
    v = v.reshape(b, t, N_KV_HEADS, HEAD_DIM)
    qi = rope(qi.reshape(b, t, N_IDX_HEADS, IDX_DIM), pos)
    ki = rope(ki[:, :, None, :], pos)[:, :, 0, :]
    wi = wi * (N_IDX_HEADS ** -0.5)
    y_attn = attention(q, k, v, qi, ki, wi) @ w_attn_out
    merged = jax.nn.sigmoid(g_c) * y_conv + jax.nn.sigmoid(g_a) * y_attn
    x = x + gt1 * (merged @ w_out)
    h2 = rmsnorm(x, g_ffn) * (1 + sc2) + sh2
    x = x + gt2 * (jnp.square(jax.nn.relu(h2 @ w_up)) @ w_down)
    return x, conv_buf_new, k, v, ki


def setup_inputs(seed: int = 0) -> dict:
    key = jax.random.key(seed)
    ks = jax.random.split(key, 24)
    f32 = jnp.float32
    n_pages = PAST_LEN // PAGE_SIZE
    n_used = DEC_BATCH * n_pages
    n_pool = n_used + (n_used + 3) // 4

    def nrm(k, shape, scale):
        return jax.random.normal(k, shape, f32) * scale

    page_table = jax.random.permutation(ks[0], n_pool)[:n_used].reshape(DEC_BATCH, n_pages).astype(jnp.int32)
    return {
        "x_prompt": nrm(ks[1], (BATCH, SEQ, D_MODEL), 1.0),
        "x_sample": nrm(ks[2], (DEC_BATCH, DEC_SEQ, D_MODEL), 1.0),
        "cache_k": nrm(ks[3], (DEPTH, n_pool, PAGE_SIZE, N_KV_HEADS, HEAD_DIM), 1.0),
        "cache_v": nrm(ks[4], (DEPTH, n_pool, PAGE_SIZE, N_KV_HEADS, HEAD_DIM), 1.0),
        "cache_kidx": nrm(ks[5], (DEPTH, n_pool, PAGE_SIZE, IDX_DIM), 1.0),
        "state_conv": nrm(ks[6], (DEPTH, DEC_BATCH, CONV_W - 1, D_CONV), 1.0),
        "page_table": page_table,
        "c_prompt": nrm(ks[7], (BATCH, D_MODEL), 1.0),
        "c_sample": nrm(ks[8], (DEC_BATCH, D_MODEL), 1.0),
        "w_ada": nrm(ks[9], (DEPTH, D_MODEL, N_MOD * D_MODEL), 0.5 * D_MODEL ** -0.5),
        "b_ada": nrm(ks[10], (DEPTH, N_MOD * D_MODEL), 0.01),
        "g_mix": 1.0 + nrm(ks[11], (DEPTH, D_MODEL), 0.02),
        "w_in": nrm(ks[12], (DEPTH, D_MODEL, D_IN), D_MODEL ** -0.5),
        "conv_w": nrm(ks[13], (DEPTH, CONV_W, D_CONV), CONV_W ** -0.5),
        "w_conv_out": nrm(ks[14], (DEPTH, D_CONV, D_MODEL), D_CONV ** -0.5),
        "w_attn_out": nrm(ks[15], (DEPTH, D_ATTN, D_MODEL), D_ATTN ** -0.5),
        "w_out": nrm(ks[16], (DEPTH, D_MODEL, D_MODEL), D_MODEL ** -0.5),
        "g_ffn": 1.0 + nrm(ks[17], (DEPTH, D_MODEL), 0.02),
        "w_up": nrm(ks[18], (DEPTH, D_MODEL, D_FF), D_MODEL ** -0.5),
        "w_down": nrm(ks[19], (DEPTH, D_FF, D_MODEL), D_FF ** -0.5),
        "g_final": 1.0 + nrm(ks[20], (D_MODEL,), 0.02),
    }


def reference(x_prompt, x_sample, cache_k, cache_v, cache_kidx, state_conv, page_table,
              c_prompt, c_sample, w_ada, b_ada, g_mix, w_in, conv_w, w_conv_out, w_attn_out,
              w_out, g_ffn, w_up, w_down, g_final):
    b_p, s_p, _ = x_prompt.shape
    t_s = x_sample.shape[1]
    past = page_table.shape[1] * cache_k.shape[2]
    pos_p = jnp.arange(s_p)
    pos_s = past + jnp.arange(t_s)
    conv_zero = jnp.zeros((b_p, CONV_W - 1, D_CONV), x_prompt.dtype)
    xp, xs = x_prompt, x_sample
    kp_l, vp_l, kip_l, cp_l = [], [], [], []
    ks_l, vs_l, kis_l, cs_l = [], [], [], []
    for l in range(DEPTH):
        wl = (w_ada[l], b_ada[l], g_mix[l], w_in[l], conv_w[l], w_conv_out[l],
              w_attn_out[l], w_out[l], g_ffn[l], w_up[l], w_down[l])
        xp, cbp, kp, vp, kip = layer_block(xp, c_prompt, pos_p, conv_zero, prompt_attention, *wl)
        samp_attn = functools.partial(sample_attention, cache_k=cache_k[l], cache_v=cache_v[l],
                                      cache_kidx=cache_kidx[l], page_table=page_table)
        xs, cbs, ksn, vsn, kis = layer_block(xs, c_sample, pos_s, state_conv[l], samp_attn, *wl)
        kp_l.append(kp); vp_l.append(vp); kip_l.append(kip); cp_l.append(cbp)
        ks_l.append(ksn); vs_l.append(vsn); kis_l.append(kis); cs_l.append(cbs)
    y_prompt = rmsnorm(xp, g_final)
    y_sample = rmsnorm(xs, g_final)
    k_prompt = jnp.stack(kp_l)
    v_prompt = jnp.stack(vp_l)
    kidx_prompt = jnp.stack(kip_l)
    conv_prompt = jnp.stack(cp_l)
    k_sample = jnp.stack(ks_l)
    v_sample = jnp.stack(vs_l)
    kidx_sample = jnp.stack(kis_l)
    conv_sample = jnp.stack(cs_l)
    return (y_prompt, y_sample, k_prompt, v_prompt, kidx_prompt, conv_prompt,
            k_sample, v_sample, kidx_sample, conv_sample)
```

```python
import functools

import jax
import jax.numpy as jnp
from jax import lax
from jax.experimental import pallas as pl
from jax.experimental.pallas import tpu as pltpu

N_HEADS = 16
HEAD_DIM = 128
N_KV_HEADS = 4
KV_GROUP = N_HEADS // N_KV_HEADS
N_IDX_HEADS = 16
IDX_DIM = 64
TOPK_MAX = 256
CONV_W = 3
ROPE_THETA = 10000.0
NORM_EPS = 1e-6
N_MOD = 6
D_ATTN = N_HEADS * HEAD_DIM
D_KV = N_KV_HEADS * HEAD_DIM
D_QI = N_IDX_HEADS * IDX_DIM

V7X_LANES = 128
V7X_VMEM_LIMIT_BYTES = 56 * 1024 * 1024

BF16 = jnp.bfloat16
F32 = jnp.float32
NEG_INF = float("-inf")

BISECT_STEPS = 22


def _tile(n, target, mult=8):
    if n <= target:
        return n
    t = (target // mult) * mult
    while t >= mult:
        if n % t == 0:
            return t
        t -= mult
    return n


def _params(*sem):
    return pltpu.CompilerParams(dimension_semantics=sem, vmem_limit_bytes=V7X_VMEM_LIMIT_BYTES)


def _dot(a, b):
    return jnp.dot(a, b, preferred_element_type=F32)


def _dot_nt(a, b):
    return lax.dot_general(a, b, (((1,), (1,)), ((), ())), preferred_element_type=F32)


def _ada_kernel(c_ref, w_ref, b_ref, o_ref):
    c = c_ref[...]
    a = (c * jax.nn.sigmoid(c)).astype(BF16)
    o_ref[...] = _dot(a, w_ref[...].astype(BF16)) + b_ref[...]


def _ada(c, w, b):
    m, d = c.shape
    n = w.shape[1]
    tn = _tile(n, 512, V7X_LANES)
    return pl.pallas_call(
        _ada_kernel,
        grid=(n // tn,),
        in_specs=[pl.BlockSpec((m, d), lambda j: (0, 0)),
                  pl.BlockSpec((d, tn), lambda j: (0, j)),
                  pl.BlockSpec((1, tn), lambda j: (0, j))],
        out_specs=pl.BlockSpec((m, tn), lambda j: (0, j)),
        out_shape=jax.ShapeDtypeStruct((m, n), F32),
        compiler_params=_params("arbitrary"),
        name="ada",
    )(c, w, b.reshape(1, n))


class _Rows:
    def __init__(self, n_rows, rows_per_group, per_token):
        self.n = n_rows
        self.rows_per_group = rows_per_group
        self.per_token = per_token

    def mod_spec(self, tm, d, which, ncol_blocks_per_mod=1, col=None):
        tn = d // ncol_blocks_per_mod
        rpg = self.rows_per_group
        if self.per_token:
            shape = (None, tm, tn)
            where = lambda i: (0, i)
        else:
            shape = (None, 1, tn)
            where = lambda i: ((i * tm) // rpg, 0)
        if col is None:
            return pl.BlockSpec(shape, lambda i, *_: (*where(i), which * ncol_blocks_per_mod))
        return pl.BlockSpec(shape, lambda i, j, *_: (*where(i), which * ncol_blocks_per_mod + col(j)))


def _norm_mod(x, g, sc, sh):
    ms = jnp.mean(x * x, axis=-1, keepdims=True)
    y = x * lax.rsqrt(ms + NORM_EPS) * g
    return y * (1.0 + sc) + sh


def _norm_mod_kernel(x_ref, g_ref, sc_ref, sh_ref, o_ref):
    o_ref[...] = _norm_mod(x_ref[...], g_ref[...], sc_ref[...], sh_ref[...]).astype(o_ref.dtype)


def _norm_mod_call(x, g, mods3, rows, which_sc, which_sh):
    n, d = x.shape
    tm = _tile(min(n, rows.rows_per_group if not rows.per_token else n), 256)
    return pl.pallas_call(
        _norm_mod_kernel,
        grid=(n // tm,),
        in_specs=[pl.BlockSpec((tm, d), lambda i: (i, 0)),
                  pl.BlockSpec((1, d), lambda i: (0, 0)),
                  rows.mod_spec(tm, d, which_sc),
                  rows.mod_spec(tm, d, which_sh)],
        out_specs=pl.BlockSpec((tm, d), lambda i: (i, 0)),
        out_shape=jax.ShapeDtypeStruct((n, d), BF16),
        compiler_params=_params("parallel"),
        name="norm_mod",
    )(x, g.reshape(1, d), mods3, mods3)


def _mm_kernel(a_ref, b_ref, o_ref):
    o_ref[...] = _dot(a_ref[...], b_ref[...]).astype(o_ref.dtype)


def _mm(a, b, out_dtype, name, tm_target=1024, tn_target=1024):
    n, k = a.shape
    m = b.shape[1]
    tm = _tile(n, tm_target)
    tn = _tile(m, tn_target, V7X_LANES)
    return pl.pallas_call(
        _mm_kernel,
        grid=(n // tm, m // tn),
        in_specs=[pl.BlockSpec((tm, k), lambda i, j: (i, 0)),
                  pl.BlockSpec((k, tn), lambda i, j: (0, j))],
        out_specs=pl.BlockSpec((tm, tn), lambda i, j: (i, j)),
        out_shape=jax.ShapeDtypeStruct((n, m), out_dtype),
        compiler_params=_params("parallel", "arbitrary"),
        name=name,
    )(a, b)


def _swap_halves(x, head_w):
    if head_w == V7X_LANES:
        return pltpu.roll(x, V7X_LANES // 2, axis=1)
    half = head_w // 2
    lane = lax.broadcasted_iota(jnp.int32, x.shape, 1)
    take_upper = (lane & (head_w - 1)) < half
    return jnp.where(take_upper, pltpu.roll(x, V7X_LANES - half, axis=1), pltpu.roll(x, half, axis=1))


def _rope_cols(acc, cos, sin, head_w):
    out = []
    for c in range(acc.shape[1] // V7X_LANES):
        x = acc[:, c * V7X_LANES:(c + 1) * V7X_LANES]
        out.append(x * cos + _swap_halves(x, head_w) * sin)
    return out


def _mm_rope_q_kernel(a_ref, b_ref, cos_ref, sin_ref, o_ref, *, head_w):
    acc = _dot(a_ref[...], b_ref[...])
    cols = _rope_cols(acc, cos_ref[...], sin_ref[...], head_w)
    per = V7X_LANES // head_w
    for c, x in enumerate(cols):
        for p in range(per):
            o_ref[c * per + p] = x[:, p * head_w:(p + 1) * head_w].astype(o_ref.dtype)


def _mm_rope_q(a, b, cos, sin, head_w, name):
    n, k = a.shape
    m = b.shape[1]
    n_heads = m // head_w
    tm = _tile(n, 1024)
    tn = _tile(m, 1024, V7X_LANES)
    hpt = tn // head_w
    return pl.pallas_call(
        functools.partial(_mm_rope_q_kernel, head_w=head_w),
        grid=(n // tm, m // tn),
        in_specs=[pl.BlockSpec((tm, k), lambda i, j: (i, 0)),
                  pl.BlockSpec((k, tn), lambda i, j: (0, j)),
                  pl.BlockSpec((tm, V7X_LANES), lambda i, j: (i, 0)),
                  pl.BlockSpec((tm, V7X_LANES), lambda i, j: (i, 0))],
        out_specs=pl.BlockSpec((hpt, tm, head_w), lambda i, j: (j, i, 0)),
        out_shape=jax.ShapeDtypeStruct((n_heads, n, head_w), BF16),
        compiler_params=_params("parallel", "arbitrary"),
        name=name,
    )(a, b, cos, sin)


def _mm_rope_kv_kernel(a_ref, b_ref, cos_ref, sin_ref, o32_ref, o16_ref, *, head_w, rope):
    acc = _dot(a_ref[...], b_ref[...])
    if rope:
        cols = _rope_cols(acc, cos_ref[...], sin_ref[...], head_w)
        for c, x in enumerate(cols):
            o32_ref[:, c * V7X_LANES:(c + 1) * V7X_LANES] = x
            o16_ref[:, c * V7X_LANES:(c + 1) * V7X_LANES] = x.astype(BF16)
    else:
        o32_ref[...] = acc
        o16_ref[...] = acc.astype(BF16)


def _mm_rope_kv(a, b, cos, sin, head_w, rope, name):
    n, k = a.shape
    m = b.shape[1]
    tm = _tile(n, 1024)
    return pl.pallas_call(
        functools.partial(_mm_rope_kv_kernel, head_w=head_w, rope=rope),
        grid=(n // tm,),
        in_specs=[pl.BlockSpec((tm, k), lambda i: (i, 0)),
                  pl.BlockSpec((k, m), lambda i: (0, 0)),
                  pl.BlockSpec((tm, V7X_LANES), lambda i: (i, 0)),
                  pl.BlockSpec((tm, V7X_LANES), lambda i: (i, 0))],
        out_specs=[pl.BlockSpec((tm, m), lambda i: (i, 0)),
                   pl.BlockSpec((tm, m), lambda i: (i, 0))],
        out_shape=[jax.ShapeDtypeStruct((n, m), F32), jax.ShapeDtypeStruct((n, m), BF16)],
        compiler_params=_params("parallel"),
        name=name,
    )(a, b, cos, sin)


PREV_ROWS = 16


def _conv_prompt_kernel(xin_ref, bg_ref, cg_ref, xin_prev_ref, cg_prev_ref, w_ref, u_ref, st_ref,
                        *, blocks_per_seq):
    i = pl.program_id(1)
    p = cg_ref[...].astype(F32) * xin_ref[...].astype(F32)
    prev = cg_prev_ref[...].astype(F32) * xin_prev_ref[...].astype(F32)
    prev = jnp.where(i % blocks_per_seq == 0, 0.0, prev)
    tm = p.shape[0]
    last = PREV_ROWS - 1
    row = lax.broadcasted_iota(jnp.int32, p.shape, 0)
    p1 = pltpu.roll(p, 1, axis=0)
    p1 = jnp.where(row == 0, prev[last:last + 1], p1)
    p2 = pltpu.roll(p, 2, axis=0)
    p2 = jnp.where(row == 0, prev[last - 1:last], jnp.where(row == 1, prev[last:last + 1], p2))
    w = w_ref[...]
    y = p2 * w[0:1] + p1 * w[1:2] + p * w[2:3]
    u_ref[...] = (bg_ref[...].astype(F32) * y).astype(u_ref.dtype)
    st_ref[...] = p[tm - (CONV_W - 1):, :]


def _conv_prompt(zc, conv_w, n_seq, seq):
    n = zc.shape[0]
    dc = zc.shape[1] // 3
    tm = _tile(seq, 512)
    tc = _tile(dc, 512, V7X_LANES)
    ncb = dc // tc
    bps = seq // tm
    prev_row_blk = lambda i: jnp.maximum(i * (tm // PREV_ROWS) - 1, 0)
    u, st = pl.pallas_call(
        functools.partial(_conv_prompt_kernel, blocks_per_seq=bps),
        grid=(ncb, n // tm),
        in_specs=[pl.BlockSpec((tm, tc), lambda j, i: (i, j)),
                  pl.BlockSpec((tm, tc), lambda j, i: (i, ncb + j)),
                  pl.BlockSpec((tm, tc), lambda j, i: (i, 2 * ncb + j)),
                  pl.BlockSpec((PREV_ROWS, tc), lambda j, i: (prev_row_blk(i), j)),
                  pl.BlockSpec((PREV_ROWS, tc), lambda j, i: (prev_row_blk(i), 2 * ncb + j)),
                  pl.BlockSpec((CONV_W, tc), lambda j, i: (0, j))],
        out_specs=[pl.BlockSpec((tm, tc), lambda j, i: (i, j)),
                   pl.BlockSpec((None, CONV_W - 1, tc), lambda j, i: (i // bps, 0, j))],
        out_shape=[jax.ShapeDtypeStruct((n, dc), BF16),
                   jax.ShapeDtypeStruct((n_seq, CONV_W - 1, dc), F32)],
        compiler_params=_params("parallel", "arbitrary"),
        name="conv_prompt",
    )(zc, zc, zc, zc, zc, conv_w)
    return u, st


def _conv_sample_kernel(xin_ref, bg_ref, cg_ref, st_ref, w_ref, u_ref, nst_ref):
    t_len = xin_ref.shape[0]
    w = w_ref[...]
    ext = [st_ref[j] for j in range(CONV_W - 1)]
    ext += [cg_ref[t].astype(F32) * xin_ref[t].astype(F32) for t in range(t_len)]
    for t in range(t_len):
        y = ext[t] * w[0:1]
        for j in range(1, CONV_W):
            y = y + ext[t + j] * w[j:j + 1]
        u_ref[t] = (bg_ref[t].astype(F32) * y).astype(u_ref.dtype)
    for j in range(CONV_W - 1):
        nst_ref[j] = ext[t_len + j]


def _conv_sample(zc_t, st_t, conv_w):
    t_len, b, dc3 = zc_t.shape
    dc = dc3 // 3
    tc = _tile(dc, 512, V7X_LANES)
    ncb = dc // tc
    return pl.pallas_call(
        _conv_sample_kernel,
        grid=(ncb,),
        in_specs=[pl.BlockSpec((t_len, b, tc), lambda j: (0, 0, j)),
                  pl.BlockSpec((t_len, b, tc), lambda j: (0, 0, ncb + j)),
                  pl.BlockSpec((t_len, b, tc), lambda j: (0, 0, 2 * ncb + j)),
                  pl.BlockSpec((CONV_W - 1, b, tc), lambda j: (0, 0, j)),
                  pl.BlockSpec((CONV_W, tc), lambda j: (0, j))],
        out_specs=[pl.BlockSpec((t_len, b, tc), lambda j: (0, 0, j)),
                   pl.BlockSpec((CONV_W - 1, b, tc), lambda j: (0, 0, j))],
        out_shape=[jax.ShapeDtypeStruct((t_len, b, dc), BF16),
                   jax.ShapeDtypeStruct((CONV_W - 1, b, dc), F32)],
        compiler_params=_params("parallel"),
        name="conv_sample",
    )(zc_t, zc_t, zc_t, st_t, conv_w)


def _count_ge(score_at, n_chunks, thr, shape):
    def body(c, acc):
        return acc + jnp.where(score_at(c) >= thr, 1.0, 0.0)
    part = lax.fori_loop(0, n_chunks, body, jnp.zeros(shape, F32))
    return jnp.sum(part, axis=1, keepdims=True)


def _topk_threshold(score_at, n_chunks, want, shape):
    rows = shape[0]

    def minmax(c, carry):
        mx, mn = carry
        s = score_at(c)
        mx = jnp.maximum(mx, s)
        mn = jnp.minimum(mn, jnp.where(s == NEG_INF, jnp.inf, s))
        return mx, mn

    mx, mn = lax.fori_loop(0, n_chunks, minmax,
                           (jnp.full(shape, NEG_INF, F32), jnp.full(shape, jnp.inf, F32)))
    hi = jnp.max(mx, axis=1, keepdims=True)
    lo = jnp.min(mn, axis=1, keepdims=True)
    cnt_lo = _count_ge(score_at, n_chunks, lo, shape)
    cnt_hi = _count_ge(score_at, n_chunks, hi, shape)
    top_ok = cnt_hi >= want
    lo = jnp.where(top_ok, hi, lo)
    cnt_lo = jnp.where(top_ok, cnt_hi, cnt_lo)

    def bisect(_, carry):
        lo, hi, cnt_lo = carry
        mid = lo + (hi - lo) * 0.5
        cnt = _count_ge(score_at, n_chunks, mid, shape)
        ok = cnt >= want
        return (jnp.where(ok, mid, lo), jnp.where(ok, hi, mid), jnp.where(ok, cnt, cnt_lo))

    lo, hi, cnt_lo = lax.fori_loop(0, BISECT_STEPS, bisect, (lo, hi, cnt_lo))

    def pending_of(cnt_lo, done):
        return jnp.max(jnp.where((cnt_lo > want) & (done == 0.0), 1.0, 0.0))

    def cond(carry):
        return carry[4] > 0.0

    def finish(carry):
        lo, hi, cnt_lo, done, _ = carry

        def below(c, acc):
            s = score_at(c)
            return jnp.maximum(acc, jnp.where(s < hi, s, NEG_INF))

        cand = jnp.max(lax.fori_loop(0, n_chunks, below, jnp.full(shape, NEG_INF, F32)),
                       axis=1, keepdims=True)
        cnt = _count_ge(score_at, n_chunks, cand, shape)
        active = (cnt_lo > want) & (done == 0.0)
        ok = active & (cnt >= want)
        lo = jnp.where(ok, cand, lo)
        cnt_lo = jnp.where(ok, cnt, cnt_lo)
        done = jnp.where(ok, 1.0, done)
        hi = jnp.where(active & (cnt < want), cand, hi)
        return lo, hi, cnt_lo, done, pending_of(cnt_lo, done)

    done0 = jnp.where(lo == hi, 1.0, 0.0) + jnp.zeros((rows, 1), F32)
    lo, hi, cnt_lo, done, _ = lax.while_loop(cond, finish, (lo, hi, cnt_lo, done0, pending_of(cnt_lo, done0)))
    return lo


def _attn_prompt_kernel(qi_ref, wi_ref, ki_ref, q_ref, k_ref, v_ref, o_ref,
                        score_ref, m_ref, l_ref, acc_ref, *, tq, kc, k_top):
    qb = pl.program_id(1)
    n_chunks = (qb * tq) // kc + tq // kc
    qpos = qb * tq + lax.broadcasted_iota(jnp.int32, (tq, 1), 0)

    qi = qi_ref[...].reshape(N_IDX_HEADS * tq, IDX_DIM)
    wi = wi_ref[...]

    def fill(c, _):
        kic = ki_ref[pl.ds(pl.multiple_of(c * kc, kc), kc), :]
        s = _dot_nt(qi, kic)
        acc = jnp.zeros((tq, kc), F32)
        for h in range(N_IDX_HEADS):
            acc = acc + jnp.maximum(s[h * tq:(h + 1) * tq], 0.0) * wi[:, h:h + 1]
        kpos = c * kc + lax.broadcasted_iota(jnp.int32, (tq, kc), 1)
        score_ref[c] = jnp.where(kpos <= qpos, acc, NEG_INF)
        return 0

    lax.fori_loop(0, n_chunks, fill, 0)

    want = jnp.minimum(qpos + 1, k_top).astype(F32)
    thr = _topk_threshold(lambda c: score_ref[c], n_chunks, want, (tq, kc))

    scale = HEAD_DIM ** -0.5
    for g in range(N_KV_HEADS):
        qg = q_ref[g * KV_GROUP:(g + 1) * KV_GROUP].reshape(KV_GROUP * tq, HEAD_DIM)
        m_ref[...] = jnp.full(m_ref.shape, NEG_INF, F32)
        l_ref[...] = jnp.zeros(l_ref.shape, F32)
        acc_ref[...] = jnp.zeros(acc_ref.shape, F32)

        def attend(c, _):
            start = pl.multiple_of(c * kc, kc)
            kch = k_ref[pl.ds(start, kc), g * HEAD_DIM:(g + 1) * HEAD_DIM]
            vch = v_ref[pl.ds(start, kc), g * HEAD_DIM:(g + 1) * HEAD_DIM]
            bias = jnp.where(score_ref[c] >= thr, 0.0, NEG_INF)
            s = _dot_nt(qg, kch) * scale
            s = (s.reshape(KV_GROUP, tq, kc) + bias[None]).reshape(KV_GROUP * tq, kc)
            m_old = m_ref[...]
            m_new = jnp.maximum(m_old, jnp.max(s, axis=1, keepdims=True))
            m_safe = jnp.where(m_new == NEG_INF, 0.0, m_new)
            p = jnp.exp(s - m_safe)
            alpha = jnp.exp(m_old - m_safe)
            l_ref[...] = alpha * l_ref[...] + jnp.sum(p, axis=1, keepdims=True)
            acc_ref[...] = alpha * acc_ref[...] + _dot(p.astype(BF16), vch)
            m_ref[...] = m_new
            return 0

        lax.fori_loop(0, n_chunks, attend, 0)
        out = acc_ref[...] / l_ref[...]
        for r in range(KV_GROUP):
            h = g * KV_GROUP + r
            o_ref[:, h * HEAD_DIM:(h + 1) * HEAD_DIM] = out[r * tq:(r + 1) * tq].astype(o_ref.dtype)


def _attn_prompt(qi_hm, wi, ki16, q_hm, k16, v16, n_seq, seq, k_top):
    n = n_seq * seq
    tq = _tile(seq, 256)
    kc = tq
    nqb = seq // tq
    kernel = functools.partial(_attn_prompt_kernel, tq=tq, kc=kc, k_top=k_top)
    return pl.pallas_call(
        kernel,
        grid=(n_seq, nqb),
        in_specs=[pl.BlockSpec((N_IDX_HEADS, tq, IDX_DIM), lambda b, q: (0, b * nqb + q, 0)),
                  pl.BlockSpec((tq, N_IDX_HEADS), lambda b, q: (b * nqb + q, 0)),
                  pl.BlockSpec((seq, IDX_DIM), lambda b, q: (b, 0)),
                  pl.BlockSpec((N_HEADS, tq, HEAD_DIM), lambda b, q: (0, b * nqb + q, 0)),
                  pl.BlockSpec((seq, D_KV), lambda b, q: (b, 0)),
                  pl.BlockSpec((seq, D_KV), lambda b, q: (b, 0))],
        out_specs=pl.BlockSpec((tq, D_ATTN), lambda b, q: (b * nqb + q, 0)),
        out_shape=jax.ShapeDtypeStruct((n, D_ATTN), BF16),
        scratch_shapes=[pltpu.VMEM((seq // kc, tq, kc), F32),
                        pltpu.VMEM((KV_GROUP * tq, 1), F32),
                        pltpu.VMEM((KV_GROUP * tq, 1), F32),
                        pltpu.VMEM((KV_GROUP * tq, HEAD_DIM), F32)],
        compiler_params=_params("parallel", "arbitrary"),
        name="attn_prompt",
    )(qi_hm, wi, ki16, q_hm, k16, v16)


def _merge_kernel(u_ref, a_ref, wc_ref, wa_ref, gc_ref, ga_ref, o_ref):
    yc = _dot(u_ref[...], wc_ref[...])
    ya = _dot(a_ref[...], wa_ref[...])
    gc = jax.nn.sigmoid(gc_ref[...].astype(F32))
    ga = jax.nn.sigmoid(ga_ref[...].astype(F32))
    o_ref[...] = (gc * yc + ga * ya).astype(o_ref.dtype)


def _merge(u, attn, wc, wa, gates):
    n, dc = u.shape
    da = attn.shape[1]
    d = wc.shape[1]
    tm = _tile(n, 1024)
    tn = _tile(d, 1024, V7X_LANES)
    ncb = d // tn
    return pl.pallas_call(
        _merge_kernel,
        grid=(n // tm, ncb),
        in_specs=[pl.BlockSpec((tm, dc), lambda i, j: (i, 0)),
                  pl.BlockSpec((tm, da), lambda i, j: (i, 0)),
                  pl.BlockSpec((dc, tn), lambda i, j: (0, j)),
                  pl.BlockSpec((da, tn), lambda i, j: (0, j)),
                  pl.BlockSpec((tm, tn), lambda i, j: (i, j)),
                  pl.BlockSpec((tm, tn), lambda i, j: (i, ncb + j))],
        out_specs=pl.BlockSpec((tm, tn), lambda i, j: (i, j)),
        out_shape=jax.ShapeDtypeStruct((n, d), BF16),
        compiler_params=_params("parallel", "arbitrary"),
        name="merge",
    )(u, attn, wc, wa, gates, gates)


def _out_proj_kernel(a_ref, w_ref, x_ref, gt_ref, o_ref):
    o_ref[...] = x_ref[...] + gt_ref[...] * _dot(a_ref[...], w_ref[...])


def _out_proj(a, w, x, mods3, rows):
    n, k = a.shape
    d = w.shape[1]
    tm = _tile(min(n, rows.rows_per_group if not rows.per_token else n), 1024)
    tn = _tile(d, 1024, V7X_LANES)
    ncb = d // tn
    return pl.pallas_call(
        _out_proj_kernel,
        grid=(n // tm, ncb),
        in_specs=[pl.BlockSpec((tm, k), lambda i, j: (i, 0)),
                  pl.BlockSpec((k, tn), lambda i, j: (0, j)),
                  pl.BlockSpec((tm, tn), lambda i, j: (i, j)),
                  rows.mod_spec(tm, d, 2, ncb, col=lambda j: j)],
        out_specs=pl.BlockSpec((tm, tn), lambda i, j: (i, j)),
        out_shape=jax.ShapeDtypeStruct((n, d), F32),
        compiler_params=_params("parallel", "arbitrary"),
        name="out_proj",
    )(a, w, x, mods3)


def _ffn_kernel(x_ref, g_ref, sc_ref, sh_ref, gt_ref, gf_ref, wu_ref, wd_ref, o_ref, h_ref, acc_ref):
    f = pl.program_id(1)

    @pl.when(f == 0)
    def _():
        h_ref[...] = _norm_mod(x_ref[...], g_ref[...], sc_ref[...], sh_ref[...]).astype(BF16)
        acc_ref[...] = jnp.zeros(acc_ref.shape, F32)

    u = jnp.maximum(_dot(h_ref[...], wu_ref[...]), 0.0)
    acc_ref[...] += _dot((u * u).astype(BF16), wd_ref[...])

    @pl.when(f == pl.num_programs(1) - 1)
    def _():
        x2 = x_ref[...] + gt_ref[...] * acc_ref[...]
        ms = jnp.mean(x2 * x2, axis=-1, keepdims=True)
        o_ref[...] = x2 * lax.rsqrt(ms + NORM_EPS) * gf_ref[...]


def _ffn(x1, g_ffn, g_final, mods3, rows, w_up, w_down):
    n, d = x1.shape
    dff = w_up.shape[1]
    tm = _tile(min(n, rows.rows_per_group if not rows.per_token else n), 512)
    tf = _tile(dff, 512, V7X_LANES)
    return pl.pallas_call(
        _ffn_kernel,
        grid=(n // tm, dff // tf),
        in_specs=[pl.BlockSpec((tm, d), lambda i, f: (i, 0), pipeline_mode=pl.Buffered(1)),
                  pl.BlockSpec((1, d), lambda i, f: (0, 0)),
                  rows.mod_spec(tm, d, 4),
                  rows.mod_spec(tm, d, 3),
                  rows.mod_spec(tm, d, 5),
                  pl.BlockSpec((1, d), lambda i, f: (0, 0)),
                  pl.BlockSpec((d, tf), lambda i, f: (0, f)),
                  pl.BlockSpec((tf, d), lambda i, f: (f, 0))],
        out_specs=pl.BlockSpec((tm, d), lambda i, f: (i, 0), pipeline_mode=pl.Buffered(1)),
        out_shape=jax.ShapeDtypeStruct((n, d), F32),
        scratch_shapes=[pltpu.VMEM((tm, d), BF16), pltpu.VMEM((tm, d), F32)],
        compiler_params=_params("parallel", "arbitrary"),
        name="ffn",
    )(x1, g_ffn.reshape(1, d), mods3, mods3, mods3, g_final.reshape(1, d), w_up, w_down)


QPAD = 8


def _page_copy(cache_ref, buf_ref, sem_ref, page_id, slot):
    return pltpu.make_async_copy(cache_ref.at[page_id], buf_ref.at[slot], sem_ref.at[slot])


def _sample_index_kernel(pt_ref, qi_ref, wi_ref, kin_ref, cache_ref, score_ref, thr_ref,
                         kbuf_ref, sem_ref, *, n_pages, page, t_len, k_top):
    b = pl.program_id(0)
    for p in range(n_pages):
        _page_copy(cache_ref, kbuf_ref, sem_ref, pt_ref[b, p], p).start()

    qi = qi_ref[...]
    wi = wi_ref[...]
    qrow = lax.broadcasted_iota(jnp.int32, (QPAD, page), 0)

    def head_sum(s):
        s = jnp.maximum(s, 0.0) * wi
        acc = s[0:QPAD]
        for h in range(1, N_IDX_HEADS):
            acc = acc + s[h * QPAD:(h + 1) * QPAD]
        return acc

    for p in range(n_pages):
        _page_copy(cache_ref, kbuf_ref, sem_ref, pt_ref[b, p], p).wait()
        kp = kbuf_ref[p].astype(BF16)
        score_ref[p] = head_sum(_dot_nt(qi, kp))

    lane = lax.broadcasted_iota(jnp.int32, (QPAD, page), 1)
    s_new = head_sum(_dot_nt(qi, kin_ref[...]))
    score_ref[n_pages] = jnp.where((lane <= qrow) & (lane < t_len), s_new, NEG_INF)

    thr = _topk_threshold(lambda c: score_ref[c], n_pages + 1, float(k_top), (QPAD, page))
    thr_ref[...] = jnp.broadcast_to(thr, thr_ref.shape)


def _sample_index(page_table, qi_s, wi_s, ki_new, cache_kidx, t_len, k_top):
    bsz, n_pages = page_table.shape
    page = cache_kidx.shape[1]
    kernel = functools.partial(_sample_index_kernel, n_pages=n_pages, page=page, t_len=t_len, k_top=k_top)
    grid_spec = pltpu.PrefetchScalarGridSpec(
        num_scalar_prefetch=1,
        grid=(bsz,),
        in_specs=[pl.BlockSpec((None, N_IDX_HEADS * QPAD, IDX_DIM), lambda b, pt: (b, 0, 0)),
                  pl.BlockSpec((None, N_IDX_HEADS * QPAD, 1), lambda b, pt: (b, 0, 0)),
                  pl.BlockSpec((None, page, IDX_DIM), lambda b, pt: (b, 0, 0)),
                  pl.BlockSpec(memory_space=pl.ANY)],
        out_specs=[pl.BlockSpec((None, n_pages + 1, QPAD, page), lambda b, pt: (b, 0, 0, 0)),
                   pl.BlockSpec((None, QPAD, V7X_LANES), lambda b, pt: (b, 0, 0))],
        scratch_shapes=[pltpu.VMEM((n_pages, page, IDX_DIM), F32),
                        pltpu.SemaphoreType.DMA((n_pages,))],
    )
    return pl.pallas_call(
        kernel,
        grid_spec=grid_spec,
        out_shape=[jax.ShapeDtypeStruct((bsz, n_pages + 1, QPAD, page), F32),
                   jax.ShapeDtypeStruct((bsz, QPAD, V7X_LANES), F32)],
        compiler_params=_params("arbitrary"),
        name="sample_index",
    )(page_table, qi_s, wi_s, ki_new, cache_kidx)


PAGES_PER_CHUNK = 8


def _sample_attn_kernel(pt_ref, q_ref, score_ref, thr_ref, kn_ref, vn_ref, ck_ref, cv_ref, o_ref,
                        kbuf_ref, vbuf_ref, sem_ref, m_ref, l_ref, acc_ref, *, n_pages, page, ppc):
    b = pl.program_id(0)
    n_chunks = n_pages // ppc
    rows = N_HEADS * QPAD

    def copies(c, slot):
        out = []
        for j in range(ppc):
            pid = pt_ref[b, c * ppc + j]
            out.append(pltpu.make_async_copy(ck_ref.at[pid], kbuf_ref.at[slot, j], sem_ref.at[0, slot, j]))
            out.append(pltpu.make_async_copy(cv_ref.at[pid], vbuf_ref.at[slot, j], sem_ref.at[1, slot, j]))
        return out

    for cp in copies(0, 0):
        cp.start()

    q = q_ref[...]
    thr = thr_ref[...][:, 0:1]
    m_ref[...] = jnp.full(m_ref.shape, NEG_INF, F32)
    l_ref[...] = jnp.zeros(l_ref.shape, F32)
    acc_ref[...] = jnp.zeros(acc_ref.shape, F32)
    scale = HEAD_DIM ** -0.5

    def attend(kch, vch, sel):
        n = kch.shape[0]
        bias = jnp.where(sel, 0.0, NEG_INF)
        s = _dot_nt(q, kch) * scale
        s = (s.reshape(N_HEADS, QPAD, n) + bias[None]).reshape(rows, n)
        m_old = m_ref[...]
        m_new = jnp.maximum(m_old, jnp.max(s, axis=1, keepdims=True))
        m_safe = jnp.where(m_new == NEG_INF, 0.0, m_new)
        p = jnp.exp(s - m_safe)
        alpha = jnp.exp(m_old - m_safe)
        l_ref[...] = alpha * l_ref[...] + jnp.sum(p, axis=1, keepdims=True)
        acc_ref[...] = alpha * acc_ref[...] + _dot(p.astype(BF16), vch)
        m_ref[...] = m_new

    def chunk_body(c, _):
        slot = c % 2

        @pl.when(c + 1 < n_chunks)
        def _():
            for cp in copies(c + 1, 1 - slot):
                cp.start()

        for cp in copies(c, slot):
            cp.wait()
        for j in range(ppc):
            kch = kbuf_ref[slot, j].astype(BF16)
            vch = vbuf_ref[slot, j].astype(BF16)
            attend(kch, vch, score_ref[c * ppc + j] >= thr)
        return 0

    lax.fori_loop(0, n_chunks, chunk_body, 0)
    attend(kn_ref[...], vn_ref[...], score_ref[n_pages] >= thr)

    out = acc_ref[...] / l_ref[...]
    for g in range(N_KV_HEADS):
        o_ref[g * KV_GROUP * QPAD:(g + 1) * KV_GROUP * QPAD, :] = (
            out[g * KV_GROUP * QPAD:(g + 1) * KV_GROUP * QPAD, g * HEAD_DIM:(g + 1) * HEAD_DIM])


def _sample_attn(page_table, q_bd, score, thr, k_new, v_new, cache_k, cache_v):
    bsz, n_pages = page_table.shape
    page = cache_k.shape[1]
    ppc = PAGES_PER_CHUNK if n_pages % PAGES_PER_CHUNK == 0 else 1
    rows = N_HEADS * QPAD
    kernel = functools.partial(_sample_attn_kernel, n_pages=n_pages, page=page, ppc=ppc)
    grid_spec = pltpu.PrefetchScalarGridSpec(
        num_scalar_prefetch=1,
        grid=(bsz,),
        in_specs=[pl.BlockSpec((None, rows, D_KV), lambda b, pt: (b, 0, 0)),
                  pl.BlockSpec((None, n_pages + 1, QPAD, page), lambda b, pt: (b, 0, 0, 0)),
                  pl.BlockSpec((None, QPAD, V7X_LANES), lambda b, pt: (b, 0, 0)),
                  pl.BlockSpec((None, page, D_KV), lambda b, pt: (b, 0, 0)),
                  pl.BlockSpec((None, page, D_KV), lambda b, pt: (b, 0, 0)),
                  pl.BlockSpec(memory_space=pl.ANY),
                  pl.BlockSpec(memory_space=pl.ANY)],
        out_specs=pl.BlockSpec((None, rows, HEAD_DIM), lambda b, pt: (b, 0, 0)),
        scratch_shapes=[pltpu.VMEM((2, ppc, page, D_KV), F32),
                        pltpu.VMEM((2, ppc, page, D_KV), F32),
                        pltpu.SemaphoreType.DMA((2, 2, ppc)),
                        pltpu.VMEM((rows, 1), F32),
                        pltpu.VMEM((rows, 1), F32),
                        pltpu.VMEM((rows, D_KV), F32)],
    )
    return pl.pallas_call(
        kernel,
        grid_spec=grid_spec,
        out_shape=jax.ShapeDtypeStruct((bsz, rows, HEAD_DIM), F32),
        compiler_params=_params("arbitrary"),
        name="sample_attn",
    )(page_table, q_bd, score, thr, k_new, v_new, cache_k, cache_v)


def _rope_tables(pos, head_w):
    half = head_w // 2
    inv = ROPE_THETA ** (-jnp.arange(half, dtype=F32) / half)
    ang = pos.astype(F32)[:, None] * inv[None, :]
    cos, sin = jnp.cos(ang), jnp.sin(ang)
    reps = V7X_LANES // head_w
    cos_t = jnp.tile(jnp.concatenate([cos, cos], axis=1), (1, reps))
    sin_t = jnp.tile(jnp.concatenate([-sin, sin], axis=1), (1, reps))
    return cos_t, sin_t


def _split_w_in(w_in, d):
    dc = d // 2
    o = 3 * dc
    w = {}
    w["conv"] = w_in[:, :o].astype(BF16)
    w["q"] = w_in[:, o:o + D_ATTN].astype(BF16); o += D_ATTN
    w["k"] = w_in[:, o:o + D_KV].astype(BF16); o += D_KV
    w["v"] = w_in[:, o:o + D_KV].astype(BF16); o += D_KV
    w["qi"] = w_in[:, o:o + D_QI].astype(BF16); o += D_QI
    kiwi = w_in[:, o:o + IDX_DIM + N_IDX_HEADS]; o += IDX_DIM + N_IDX_HEADS
    w["kiwi"] = jnp.pad(kiwi, ((0, 0), (0, V7X_LANES - kiwi.shape[1]))).astype(BF16)
    w["gates"] = w_in[:, o:o + 2 * d].astype(BF16)
    return w


def _in_proj(h, w, pos):
    cos128, sin128 = _rope_tables(pos, HEAD_DIM)
    cos64, sin64 = _rope_tables(pos, IDX_DIM)
    lane = jnp.arange(V7X_LANES)
    cos_kw = jnp.where(lane < IDX_DIM, cos64, 1.0)
    sin_kw = jnp.where(lane < IDX_DIM, sin64, 0.0)
    zc = _mm(h, w["conv"], BF16, "in_conv")
    gates = _mm(h, w["gates"], BF16, "in_gates")
    q_hm = _mm_rope_q(h, w["q"], cos128, sin128, HEAD_DIM, "in_q")
    qi_hm = _mm_rope_q(h, w["qi"], cos64, sin64, IDX_DIM, "in_qi")
    k32, k16 = _mm_rope_kv(h, w["k"], cos128, sin128, HEAD_DIM, True, "in_k")
    v32, v16 = _mm_rope_kv(h, w["v"], cos128, sin128, HEAD_DIM, False, "in_v")
    kw32, kw16 = _mm_rope_kv(h, w["kiwi"], cos_kw, sin_kw, IDX_DIM, True, "in_kiwi")
    ki32 = kw32[:, :IDX_DIM]
    ki16 = kw16[:, :IDX_DIM]
    wi = kw32[:, IDX_DIM:IDX_DIM + N_IDX_HEADS] * (N_IDX_HEADS ** -0.5) * (IDX_DIM ** -0.5)
    return zc, gates, q_hm, qi_hm, k32, k16, v32, v16, ki32, ki16, wi


def kernel(x_prompt, x_sample, cache_k, cache_v, cache_kidx, state_conv, page_table, c_prompt, c_sample,
           w_ada, b_ada, g_mix, w_in, conv_w, w_conv_out, w_attn_out, w_out, g_ffn, w_up, w_down, g_final):
    depth = w_ada.shape[0]
    assert depth == 1, "single-layer trunk"
    bp, sp, d = x_prompt.shape
    bs, ts, _ = x_sample.shape
    dc = d // 2
    page = cache_k.shape[2]
    n_pages = page_table.shape[1]
    past = n_pages * page
    assert ts <= QPAD and ts <= page

    c_all = jnp.concatenate([c_prompt, c_sample], axis=0)
    n_c = c_all.shape[0]
    c_all = jnp.pad(c_all, ((0, (-n_c) % 8), (0, 0)))
    mods = _ada(c_all, w_ada[0], b_ada[0])
    mods_p = mods[:bp].reshape(bp, 1, N_MOD * d)
    mods_s = jnp.repeat(mods[bp:bp + bs], ts, axis=0).reshape(1, bs * ts, N_MOD * d)
    rows_p = _Rows(bp * sp, sp, per_token=False)
    rows_s = _Rows(bs * ts, bs * ts, per_token=True)

    w = _split_w_in(w_in[0], d)
    wc16 = w_conv_out[0].astype(BF16)
    wa16 = w_attn_out[0].astype(BF16)
    wo16 = w_out[0].astype(BF16)
    wu16 = w_up[0].astype(BF16)
    wd16 = w_down[0].astype(BF16)

    xp = x_prompt.reshape(bp * sp, d)
    hp = _norm_mod_call(xp, g_mix[0], mods_p, rows_p, 1, 0)
    pos_p = jnp.tile(jnp.arange(sp), bp)
    zc, gates, q_hm, qi_hm, k32, k16, v32, v16, ki32, ki16, wi = _in_proj(hp, w, pos_p)
    u_p, conv_p = _conv_prompt(zc, conv_w[0], bp, sp)
    k_top_p = min(TOPK_MAX, sp // 4)
    attn_p = _attn_prompt(qi_hm, wi, ki16, q_hm, k16, v16, bp, sp, k_top_p)
    merged_p = _merge(u_p, attn_p, wc16, wa16, gates)
    x1_p = _out_proj(merged_p, wo16, xp, mods_p, rows_p)
    y_p = _ffn(x1_p, g_ffn[0], g_final, mods_p, rows_p, wu16, wd16)

    ns = bs * ts
    xs = x_sample.reshape(ns, d)
    hs = _norm_mod_call(xs, g_mix[0], mods_s, rows_s, 1, 0)
    pos_s = jnp.tile(past + jnp.arange(ts), bs)
    zc_s, gates_s, q_hm_s, qi_hm_s, ks32, ks16, vs32, vs16, kis32, kis16, wi_s = _in_proj(hs, w, pos_s)

    zc_t = zc_s.reshape(bs, ts, 3 * dc).transpose(1, 0, 2)
    st_t = state_conv[0].transpose(1, 0, 2)
    u_t, nst_t = _conv_sample(zc_t, st_t, conv_w[0])
    u_s = u_t.transpose(1, 0, 2).reshape(ns, dc)
    conv_s = nst_t.transpose(1, 0, 2)

    k_top_s = min(TOPK_MAX, (past + ts) // 4)
    pad_q = ((0, 0), (0, 0), (0, QPAD - ts), (0, 0))
    qi_s = jnp.pad(qi_hm_s.reshape(N_IDX_HEADS, bs, ts, IDX_DIM).transpose(1, 0, 2, 3), pad_q)
    qi_s = qi_s.reshape(bs, N_IDX_HEADS * QPAD, IDX_DIM)
    wi_sp = jnp.pad(wi_s.reshape(bs, ts, N_IDX_HEADS).transpose(0, 2, 1), ((0, 0), (0, 0), (0, QPAD - ts)))
    wi_sp = wi_sp.reshape(bs, N_IDX_HEADS * QPAD, 1)
    pad_new = ((0, 0), (0, page - ts), (0, 0))
    ki_new = jnp.pad(kis16.reshape(bs, ts, IDX_DIM), pad_new)
    score_s, thr_s = _sample_index(page_table, qi_s, wi_sp, ki_new, cache_kidx[0], ts, k_top_s)

    q_s = jnp.pad(q_hm_s.reshape(N_HEADS, bs, ts, HEAD_DIM).transpose(1, 0, 2, 3), pad_q)
    head_group = jnp.arange(N_HEADS) // KV_GROUP
    onehot = (head_group[:, None] == jnp.arange(N_KV_HEADS)[None, :]).astype(BF16)
    q_bd = (q_s[:, :, :, None, :] * onehot[None, :, None, :, None]).reshape(bs, N_HEADS * QPAD, D_KV)
    k_new = jnp.pad(ks16.reshape(bs, ts, D_KV), pad_new)
    v_new = jnp.pad(vs16.reshape(bs, ts, D_KV), pad_new)
    ck = cache_k[0].reshape(cache_k.shape[1], page, D_KV)
    cv = cache_v[0].reshape(cache_v.shape[1], page, D_KV)
    attn_hq = _sample_attn(page_table, q_bd, score_s, thr_s, k_new, v_new, ck, cv)
    attn_s = attn_hq.reshape(bs, N_HEADS, QPAD, HEAD_DIM)[:, :, :ts].transpose(0, 2, 1, 3)
    attn_s = attn_s.reshape(ns, D_ATTN).astype(BF16)

    merged_s = _merge(u_s, attn_s, wc16, wa16, gates_s)
    x1_s = _out_proj(merged_s, wo16, xs, mods_s, rows_s)
    y_s = _ffn(x1_s, g_ffn[0], g_final, mods_s, rows_s, wu16, wd16)

    return (y_p.reshape(bp, sp, d),
            y_s.reshape(bs, ts, d),
            k32.reshape(1, bp, sp, N_KV_HEADS, HEAD_DIM),
            v32.reshape(1, bp, sp, N_KV_HEADS, HEAD_DIM),
            ki32.reshape(1, bp, sp, IDX_DIM),
            conv_p.reshape(1, bp, CONV_W - 1, dc),
            ks32.reshape(1, bs, ts, N_KV_HEADS, HEAD_DIM),
            vs32.reshape(1, bs, ts, N_KV_HEADS, HEAD_DIM),
            kis32.reshape(1, bs, ts, IDX_DIM),
            conv_s.reshape(1, bs, CONV_W - 1, dc))
```

```python
import functools

import jax
import jax.numpy as jnp
from jax import lax
from jax.experimental import pallas as pl
from jax.experimental.pallas import tpu as pltpu

N_HEADS = 16
HEAD_DIM = 128
N_KV_HEADS = 4
KV_GROUP = N_HEADS // N_KV_HEADS
N_IDX_HEADS = 16
IDX_DIM = 64
TOPK_MAX = 256
CONV_W = 3
ROPE_THETA = 10000.0
NORM_EPS = 1e-6
N_MOD = 6
D_ATTN = N_HEADS * HEAD_DIM
D_KV = N_KV_HEADS * HEAD_DIM
D_QI = N_IDX_HEADS * IDX_DIM

V7X_LANES = 128
V7X_VMEM_LIMIT_BYTES = 56 * 1024 * 1024

BF16 = jnp.bfloat16
F32 = jnp.float32
NEG_INF = float("-inf")

BISECT_STEPS = 22


def _tile(n, target, mult=8):
    if n <= target:
        return n
    t = (target // mult) * mult
    while t >= mult:
        if n % t == 0:
            return t
        t -= mult
    return n


def _params(*sem):
    return pltpu.CompilerParams(dimension_semantics=sem, vmem_limit_bytes=V7X_VMEM_LIMIT_BYTES)


def _dot(a, b):
    return jnp.dot(a, b, preferred_element_type=F32)


def _dot_nt(a, b):
    return lax.dot_general(a, b, (((1,), (1,)), ((), ())), preferred_element_type=F32)


def _ada_kernel(c_ref, w_ref, b_ref, o_ref):
    c = c_ref[...]
    a = (c * jax.nn.sigmoid(c)).astype(BF16)
    o_ref[...] = _dot(a, w_ref[...].astype(BF16)) + b_ref[...]


def _ada(c, w, b):
    m, d = c.shape
    n = w.shape[1]
    tn = _tile(n, 512, V7X_LANES)
    return pl.pallas_call(
        _ada_kernel,
        grid=(n // tn,),
        in_specs=[pl.BlockSpec((m, d), lambda j: (0, 0)),
                  pl.BlockSpec((d, tn), lambda j: (0, j)),
                  pl.BlockSpec((1, tn), lambda j: (0, j))],
        out_specs=pl.BlockSpec((m, tn), lambda j: (0, j)),
        out_shape=jax.ShapeDtypeStruct((m, n), F32),
        compiler_params=_params("arbitrary"),
        name="ada",
    )(c, w, b.reshape(1, n))


class _Rows:
    def __init__(self, n_rows, rows_per_group, per_token):
        self.n = n_rows
        self.rows_per_group = rows_per_group
        self.per_token = per_token

    def mod_spec(self, tm, d, which, ncol_blocks_per_mod=1, col=None):
        tn = d // ncol_blocks_per_mod
        rpg = self.rows_per_group
        if self.per_token:
            shape = (None, tm, tn)
            where = lambda i: (0, i)
        else:
            shape = (None, 1, tn)
            where = lambda i: ((i * tm) // rpg, 0)
        if col is None:
            return pl.BlockSpec(shape, lambda i, *_: (*where(i), which * ncol_blocks_per_mod))
        return pl.BlockSpec(shape, lambda i, j, *_: (*where(i), which * ncol_blocks_per_mod + col(j)))


def _norm_mod(x, g, sc, sh):
    ms = jnp.mean(x * x, axis=-1, keepdims=True)
    y = x * lax.rsqrt(ms + NORM_EPS) * g
    return y * (1.0 + sc) + sh


def _norm_mod_kernel(x_ref, g_ref, sc_ref, sh_ref, o_ref):
    o_ref[...] = _norm_mod(x_ref[...], g_ref[...], sc_ref[...], sh_ref[...]).astype(o_ref.dtype)


def _norm_mod_call(x, g, mods3, rows, which_sc, which_sh):
    n, d = x.shape
    tm = _tile(min(n, rows.rows_per_group if not rows.per_token else n), 256)
    return pl.pallas_call(
        _norm_mod_kernel,
        grid=(n // tm,),
        in_specs=[pl.BlockSpec((tm, d), lambda i: (i, 0)),
                  pl.BlockSpec((1, d), lambda i: (0, 0)),
                  rows.mod_spec(tm, d, which_sc),
                  rows.mod_spec(tm, d, which_sh)],
        out_specs=pl.BlockSpec((tm, d), lambda i: (i, 0)),
        out_shape=jax.ShapeDtypeStruct((n, d), BF16),
        compiler_params=_params("parallel"),
        name="norm_mod",
    )(x, g.reshape(1, d), mods3, mods3)


def _mm_kernel(a_ref, b_ref, o_ref):
    o_ref[...] = _dot(a_ref[...], b_ref[...]).astype(o_ref.dtype)


def _mm(a, b, out_dtype, name, tm_target=1024, tn_target=1024):
    n, k = a.shape
    m = b.shape[1]
    tm = _tile(n, tm_target)
    tn = _tile(m, tn_target, V7X_LANES)
    return pl.pallas_call(
        _mm_kernel,
        grid=(n // tm, m // tn),
        in_specs=[pl.BlockSpec((tm, k), lambda i, j: (i, 0)),
                  pl.BlockSpec((k, tn), lambda i, j: (0, j))],
        out_specs=pl.BlockSpec((tm, tn), lambda i, j: (i, j)),
        out_shape=jax.ShapeDtypeStruct((n, m), out_dtype),
        compiler_params=_params("parallel", "arbitrary"),
        name=name,
    )(a, b)


def _swap_halves(x, head_w):
    if head_w == V7X_LANES:
        return pltpu.roll(x, V7X_LANES // 2, axis=1)
    half = head_w // 2
    lane = lax.broadcasted_iota(jnp.int32, x.shape, 1)
    take_upper = (lane & (head_w - 1)) < half
    return jnp.where(take_upper, pltpu.roll(x, V7X_LANES - half, axis=1), pltpu.roll(x, half, axis=1))


def _rope_cols(acc, cos, sin, head_w):
    out = []
    for c in range(acc.shape[1] // V7X_LANES):
        x = acc[:, c * V7X_LANES:(c + 1) * V7X_LANES]
        out.append(x * cos + _swap_halves(x, head_w) * sin)
    return out


def _mm_rope_q_kernel(a_ref, b_ref, cos_ref, sin_ref, o_ref, *, head_w):
    acc = _dot(a_ref[...], b_ref[...])
    cols = _rope_cols(acc, cos_ref[...], sin_ref[...], head_w)
    per = V7X_LANES // head_w
    for c, x in enumerate(cols):
        for p in range(per):
            o_ref[c * per + p] = x[:, p * head_w:(p + 1) * head_w].astype(o_ref.dtype)


def _mm_rope_q(a, b, cos, sin, head_w, name):
    n, k = a.shape
    m = b.shape[1]
    n_heads = m // head_w
    tm = _tile(n, 1024)
    tn = _tile(m, 1024, V7X_LANES)
    hpt = tn // head_w
    return pl.pallas_call(
        functools.partial(_mm_rope_q_kernel, head_w=head_w),
        grid=(n // tm, m // tn),
        in_specs=[pl.BlockSpec((tm, k), lambda i, j: (i, 0)),
                  pl.BlockSpec((k, tn), lambda i, j: (0, j)),
                  pl.BlockSpec((tm, V7X_LANES), lambda i, j: (i, 0)),
                  pl.BlockSpec((tm, V7X_LANES), lambda i, j: (i, 0))],
        out_specs=pl.BlockSpec((hpt, tm, head_w), lambda i, j: (j, i, 0)),
        out_shape=jax.ShapeDtypeStruct((n_heads, n, head_w), BF16),
        compiler_params=_params("parallel", "arbitrary"),
        name=name,
    )(a, b, cos, sin)


def _mm_rope_kv_kernel(a_ref, b_ref, cos_ref, sin_ref, o32_ref, o16_ref, *, head_w, rope):
    acc = _dot(a_ref[...], b_ref[...])
    if rope:
        cols = _rope_cols(acc, cos_ref[...], sin_ref[...], head_w)
        for c, x in enumerate(cols):
            o32_ref[:, c * V7X_LANES:(c + 1) * V7X_LANES] = x
            o16_ref[:, c * V7X_LANES:(c + 1) * V7X_LANES] = x.astype(BF16)
    else:
        o32_ref[...] = acc
        o16_ref[...] = acc.astype(BF16)


def _mm_rope_kv(a, b, cos, sin, head_w, rope, name):
    n, k = a.shape
    m = b.shape[1]
    tm = _tile(n, 1024)
    return pl.pallas_call(
        functools.partial(_mm_rope_kv_kernel, head_w=head_w, rope=rope),
        grid=(n // tm,),
        in_specs=[pl.BlockSpec((tm, k), lambda i: (i, 0)),
                  pl.BlockSpec((k, m), lambda i: (0, 0)),
                  pl.BlockSpec((tm, V7X_LANES), lambda i: (i, 0)),
                  pl.BlockSpec((tm, V7X_LANES), lambda i: (i, 0))],
        out_specs=[pl.BlockSpec((tm, m), lambda i: (i, 0)),
                   pl.BlockSpec((tm, m), lambda i: (i, 0))],
        out_shape=[jax.ShapeDtypeStruct((n, m), F32), jax.ShapeDtypeStruct((n, m), BF16)],
        compiler_params=_params("parallel"),
        name=name,
    )(a, b, cos, sin)


PREV_ROWS = 16


def _conv_prompt_kernel(xin_ref, bg_ref, cg_ref, xin_prev_ref, cg_prev_ref, w_ref, u_ref, st_ref,
                        *, blocks_per_seq):
    i = pl.program_id(1)
    p = cg_ref[...].astype(F32) * xin_ref[...].astype(F32)
    prev = cg_prev_ref[...].astype(F32) * xin_prev_ref[...].astype(F32)
    prev = jnp.where(i % blocks_per_seq == 0, 0.0, prev)
    tm = p.shape[0]
    last = PREV_ROWS - 1
    row = lax.broadcasted_iota(jnp.int32, p.shape, 0)
    p1 = pltpu.roll(p, 1, axis=0)
    p1 = jnp.where(row == 0, prev[last:last + 1], p1)
    p2 = pltpu.roll(p, 2, axis=0)
    p2 = jnp.where(row == 0, prev[last - 1:last], jnp.where(row == 1, prev[last:last + 1], p2))
    w = w_ref[...]
    y = p2 * w[0:1] + p1 * w[1:2] + p * w[2:3]
    u_ref[...] = (bg_ref[...].astype(F32) * y).astype(u_ref.dtype)
    st_ref[...] = p[tm - (CONV_W - 1):, :]


def _conv_prompt(zc, conv_w, n_seq, seq):
    n = zc.shape[0]
    dc = zc.shape[1] // 3
    tm = _tile(seq, 512)
    tc = _tile(dc, 512, V7X_LANES)
    ncb = dc // tc
    bps = seq // tm
    prev_row_blk = lambda i: jnp.maximum(i * (tm // PREV_ROWS) - 1, 0)
    u, st = pl.pallas_call(
        functools.partial(_conv_prompt_kernel, blocks_per_seq=bps),
        grid=(ncb, n // tm),
        in_specs=[pl.BlockSpec((tm, tc), lambda j, i: (i, j)),
                  pl.BlockSpec((tm, tc), lambda j, i: (i, ncb + j)),
                  pl.BlockSpec((tm, tc), lambda j, i: (i, 2 * ncb + j)),
                  pl.BlockSpec((PREV_ROWS, tc), lambda j, i: (prev_row_blk(i), j)),
                  pl.BlockSpec((PREV_ROWS, tc), lambda j, i: (prev_row_blk(i), 2 * ncb + j)),
                  pl.BlockSpec((CONV_W, tc), lambda j, i: (0, j))],
        out_specs=[pl.BlockSpec((tm, tc), lambda j, i: (i, j)),
                   pl.BlockSpec((None, CONV_W - 1, tc), lambda j, i: (i // bps, 0, j))],
        out_shape=[jax.ShapeDtypeStruct((n, dc), BF16),
                   jax.ShapeDtypeStruct((n_seq, CONV_W - 1, dc), F32)],
        compiler_params=_params("parallel", "arbitrary"),
        name="conv_prompt",
    )(zc, zc, zc, zc, zc, conv_w)
    return u, st


def _conv_sample_kernel(xin_ref, bg_ref, cg_ref, st_ref, w_ref, u_ref, nst_ref):
    t_len = xin_ref.shape[0]
    w = w_ref[...]
    ext = [st_ref[j] for j in range(CONV_W - 1)]
    ext += [cg_ref[t].astype(F32) * xin_ref[t].astype(F32) for t in range(t_len)]
    for t in range(t_len):
        y = ext[t] * w[0:1]
        for j in range(1, CONV_W):
            y = y + ext[t + j] * w[j:j + 1]
        u_ref[t] = (bg_ref[t].astype(F32) * y).astype(u_ref.dtype)
    for j in range(CONV_W - 1):
        nst_ref[j] = ext[t_len + j]


def _conv_sample(zc_t, st_t, conv_w):
    t_len, b, dc3 = zc_t.shape
    dc = dc3 // 3
    tc = _tile(dc, 512, V7X_LANES)
    ncb = dc // tc
    return pl.pallas_call(
        _conv_sample_kernel,
        grid=(ncb,),
        in_specs=[pl.BlockSpec((t_len, b, tc), lambda j: (0, 0, j)),
                  pl.BlockSpec((t_len, b, tc), lambda j: (0, 0, ncb + j)),
                  pl.BlockSpec((t_len, b, tc), lambda j: (0, 0, 2 * ncb + j)),
                  pl.BlockSpec((CONV_W - 1, b, tc), lambda j: (0, 0, j)),
                  pl.BlockSpec((CONV_W, tc), lambda j: (0, j))],
        out_specs=[pl.BlockSpec((t_len, b, tc), lambda j: (0, 0, j)),
                   pl.BlockSpec((CONV_W - 1, b, tc), lambda j: (0, 0, j))],
        out_shape=[jax.ShapeDtypeStruct((t_len, b, dc), BF16),
                   jax.ShapeDtypeStruct((CONV_W - 1, b, dc), F32)],
        compiler_params=_params("parallel"),
        name="conv_sample",
    )(zc_t, zc_t, zc_t, st_t, conv_w)


_SUBLANES = 8
_REDUCERS = {jnp.add: jnp.sum, jnp.maximum: jnp.max, jnp.minimum: jnp.min}


def _fold_chunks(n_chunks, term, combine, init, shape, axis):
    if axis == 0 and shape[0] > _SUBLANES:
        full_term = term
        shape = (_SUBLANES, shape[1])
        term = lambda c: _REDUCERS[combine](full_term(c).reshape(-1, *shape), axis=0)
    if isinstance(n_chunks, int):
        vals = [term(c) for c in range(n_chunks)]
        while len(vals) > 1:
            vals = [combine(vals[i], vals[i + 1]) if i + 1 < len(vals) else vals[i]
                    for i in range(0, len(vals), 2)]
        return vals[0]
    return lax.fori_loop(0, n_chunks, lambda c, acc: combine(acc, term(c)), jnp.full(shape, init, F32))


def _topk_threshold(score_at, n_chunks, want, shape, axis):
    fold = functools.partial(_fold_chunks, n_chunks, shape=shape, axis=axis)

    def count_ge(thr):
        part = fold(lambda c: jnp.where(score_at(c) >= thr, 1.0, 0.0), jnp.add, 0.0)
        return jnp.sum(part, axis=axis, keepdims=True)

    hi = jnp.max(fold(score_at, jnp.maximum, NEG_INF), axis=axis, keepdims=True)
    lo = jnp.min(fold(lambda c: jnp.where(score_at(c) == NEG_INF, jnp.inf, score_at(c)), jnp.minimum, jnp.inf),
                 axis=axis, keepdims=True)
    cnt_lo = count_ge(lo)
    cnt_hi = count_ge(hi)
    top_ok = cnt_hi >= want
    lo = jnp.where(top_ok, hi, lo)
    cnt_lo = jnp.where(top_ok, cnt_hi, cnt_lo)

    def bisect(_, carry):
        lo, hi, cnt_lo = carry
        mid = lo + (hi - lo) * 0.5
        cnt = count_ge(mid)
        ok = cnt >= want
        return (jnp.where(ok, mid, lo), jnp.where(ok, hi, mid), jnp.where(ok, cnt, cnt_lo))

    lo, hi, cnt_lo = lax.fori_loop(0, BISECT_STEPS, bisect, (lo, hi, cnt_lo))

    def pending_of(cnt_lo, done):
        return jnp.max(jnp.where((cnt_lo > want) & (done == 0.0), 1.0, 0.0))

    def finish(carry):
        lo, hi, cnt_lo, done, _ = carry
        cand = jnp.max(fold(lambda c: jnp.where(score_at(c) < hi, score_at(c), NEG_INF), jnp.maximum, NEG_INF),
                       axis=axis, keepdims=True)
        cnt = count_ge(cand)
        active = (cnt_lo > want) & (done == 0.0)
        ok = active & (cnt >= want)
        lo = jnp.where(ok, cand, lo)
        cnt_lo = jnp.where(ok, cnt, cnt_lo)
        done = jnp.where(ok, 1.0, done)
        hi = jnp.where(active & (cnt < want), cand, hi)
        return lo, hi, cnt_lo, done, pending_of(cnt_lo, done)

    done0 = jnp.where(lo == hi, 1.0, 0.0)
    lo, *_ = lax.while_loop(lambda carry: carry[4] > 0.0, finish,
                            (lo, hi, cnt_lo, done0, pending_of(cnt_lo, done0)))
    return lo


def _attn_prompt_kernel(qi_ref, wi_ref, ki_ref, q_ref, k_ref, vt_ref, o_ref,
                        score_ref, m_ref, l_ref, acc_ref, *, tq, kc, k_top):
    qb = pl.program_id(1)
    n_chunks = (qb * tq) // kc + tq // kc
    qpos = qb * tq + lax.broadcasted_iota(jnp.int32, (1, tq), 1)
    wi = wi_ref[...]

    def fill(c, _):
        kic = ki_ref[pl.ds(pl.multiple_of(c * kc, kc), kc), :]
        acc = jnp.zeros((kc, tq), F32)
        for h in range(N_IDX_HEADS):
            acc = acc + jnp.maximum(_dot_nt(kic, qi_ref[h]), 0.0) * wi[h:h + 1]
        kpos = c * kc + lax.broadcasted_iota(jnp.int32, (kc, tq), 0)
        score_ref[c] = jnp.where(kpos <= qpos, acc, NEG_INF)
        return 0

    lax.fori_loop(0, n_chunks, fill, 0)

    want = jnp.minimum(qpos + 1, k_top).astype(F32)
    thr = _topk_threshold(lambda c: score_ref[c], n_chunks, want, (kc, tq), axis=0)

    scale = HEAD_DIM ** -0.5
    for g in range(N_KV_HEADS):
        m_ref[...] = jnp.full(m_ref.shape, NEG_INF, F32)
        l_ref[...] = jnp.zeros(l_ref.shape, F32)
        acc_ref[...] = jnp.zeros(acc_ref.shape, F32)

        def attend(c, _):
            kch = k_ref[pl.ds(pl.multiple_of(c * kc, kc), kc), g * HEAD_DIM:(g + 1) * HEAD_DIM]
            vt = vt_ref[c, g * HEAD_DIM:(g + 1) * HEAD_DIM, :]
            bias = jnp.where(score_ref[c] >= thr, 0.0, NEG_INF)
            heads = range(KV_GROUP)
            m_old = [m_ref[r] for r in heads]
            l_old = [l_ref[r] for r in heads]
            s = [_dot_nt(kch, q_ref[g * KV_GROUP + r]) * scale + bias for r in heads]
            m_new = [jnp.maximum(m_old[r], jnp.max(s[r], axis=0, keepdims=True)) for r in heads]
            m_safe = [jnp.where(m_new[r] == NEG_INF, 0.0, m_new[r]) for r in heads]
            p = [jnp.exp(s[r] - m_safe[r]) for r in heads]
            alpha = [jnp.exp(m_old[r] - m_safe[r]) for r in heads]
            pv = [_dot(vt, p[r].astype(BF16)) for r in heads]
            for r in heads:
                l_ref[r] = alpha[r] * l_old[r] + jnp.sum(p[r], axis=0, keepdims=True)
                acc_ref[r] = alpha[r] * acc_ref[r] + pv[r]
                m_ref[r] = m_new[r]
            return 0

        lax.fori_loop(0, n_chunks, attend, 0)
        for r in range(KV_GROUP):
            h = g * KV_GROUP + r
            out = acc_ref[r] / l_ref[r]
            o_ref[:, h * HEAD_DIM:(h + 1) * HEAD_DIM] = out.T.astype(o_ref.dtype)


def _attn_prompt(qi_hm, wi_t, ki16, q_hm, k16, vt16, n_seq, seq, k_top):
    n = n_seq * seq
    tq = _tile(seq, 256)
    kc = tq
    nqb = seq // tq
    assert vt16.shape == (n_seq, seq // kc, D_KV, kc)
    kernel = functools.partial(_attn_prompt_kernel, tq=tq, kc=kc, k_top=k_top)
    return pl.pallas_call(
        kernel,
        grid=(n_seq, nqb),
        in_specs=[pl.BlockSpec((N_IDX_HEADS, tq, IDX_DIM), lambda b, q: (0, b * nqb + q, 0)),
                  pl.BlockSpec((N_IDX_HEADS, tq), lambda b, q: (0, b * nqb + q)),
                  pl.BlockSpec((seq, IDX_DIM), lambda b, q: (b, 0)),
                  pl.BlockSpec((N_HEADS, tq, HEAD_DIM), lambda b, q: (0, b * nqb + q, 0)),
                  pl.BlockSpec((seq, D_KV), lambda b, q: (b, 0)),
                  pl.BlockSpec((None, seq // kc, D_KV, kc), lambda b, q: (b, 0, 0, 0))],
        out_specs=pl.BlockSpec((tq, D_ATTN), lambda b, q: (b * nqb + q, 0)),
        out_shape=jax.ShapeDtypeStruct((n, D_ATTN), BF16),
        scratch_shapes=[pltpu.VMEM((seq // kc, kc, tq), F32),
                        pltpu.VMEM((KV_GROUP, 1, tq), F32),
                        pltpu.VMEM((KV_GROUP, 1, tq), F32),
                        pltpu.VMEM((KV_GROUP, HEAD_DIM, tq), F32)],
        compiler_params=_params("parallel", "arbitrary"),
        name="attn_prompt",
    )(qi_hm, wi_t, ki16, q_hm, k16, vt16)


def _merge_kernel(u_ref, a_ref, wc_ref, wa_ref, gc_ref, ga_ref, o_ref):
    yc = _dot(u_ref[...], wc_ref[...])
    ya = _dot(a_ref[...], wa_ref[...])
    gc = jax.nn.sigmoid(gc_ref[...].astype(F32))
    ga = jax.nn.sigmoid(ga_ref[...].astype(F32))
    o_ref[...] = (gc * yc + ga * ya).astype(o_ref.dtype)


def _merge(u, attn, wc, wa, gates):
    n, dc = u.shape
    da = attn.shape[1]
    d = wc.shape[1]
    tm = _tile(n, 1024)
    tn = _tile(d, 1024, V7X_LANES)
    ncb = d // tn
    return pl.pallas_call(
        _merge_kernel,
        grid=(n // tm, ncb),
        in_specs=[pl.BlockSpec((tm, dc), lambda i, j: (i, 0)),
                  pl.BlockSpec((tm, da), lambda i, j: (i, 0)),
                  pl.BlockSpec((dc, tn), lambda i, j: (0, j)),
                  pl.BlockSpec((da, tn), lambda i, j: (0, j)),
                  pl.BlockSpec((tm, tn), lambda i, j: (i, j)),
                  pl.BlockSpec((tm, tn), lambda i, j: (i, ncb + j))],
        out_specs=pl.BlockSpec((tm, tn), lambda i, j: (i, j)),
        out_shape=jax.ShapeDtypeStruct((n, d), BF16),
        compiler_params=_params("parallel", "arbitrary"),
        name="merge",
    )(u, attn, wc, wa, gates, gates)


def _out_proj_kernel(a_ref, w_ref, x_ref, gt_ref, o_ref):
    o_ref[...] = x_ref[...] + gt_ref[...] * _dot(a_ref[...], w_ref[...])


def _out_proj(a, w, x, mods3, rows):
    n, k = a.shape
    d = w.shape[1]
    tm = _tile(min(n, rows.rows_per_group if not rows.per_token else n), 1024)
    tn = _tile(d, 1024, V7X_LANES)
    ncb = d // tn
    return pl.pallas_call(
        _out_proj_kernel,
        grid=(n // tm, ncb),
        in_specs=[pl.BlockSpec((tm, k), lambda i, j: (i, 0)),
                  pl.BlockSpec((k, tn), lambda i, j: (0, j)),
                  pl.BlockSpec((tm, tn), lambda i, j: (i, j)),
                  rows.mod_spec(tm, d, 2, ncb, col=lambda j: j)],
        out_specs=pl.BlockSpec((tm, tn), lambda i, j: (i, j)),
        out_shape=jax.ShapeDtypeStruct((n, d), F32),
        compiler_params=_params("parallel", "arbitrary"),
        name="out_proj",
    )(a, w, x, mods3)


def _ffn_kernel(x_ref, g_ref, sc_ref, sh_ref, gt_ref, gf_ref, wu_ref, wd_ref, o_ref, h_ref, acc_ref):
    f = pl.program_id(1)

    @pl.when(f == 0)
    def _():
        h_ref[...] = _norm_mod(x_ref[...], g_ref[...], sc_ref[...], sh_ref[...]).astype(BF16)
        acc_ref[...] = jnp.zeros(acc_ref.shape, F32)

    u = jnp.maximum(_dot(h_ref[...], wu_ref[...]), 0.0)
    acc_ref[...] += _dot((u * u).astype(BF16), wd_ref[...])

    @pl.when(f == pl.num_programs(1) - 1)
    def _():
        x2 = x_ref[...] + gt_ref[...] * acc_ref[...]
        ms = jnp.mean(x2 * x2, axis=-1, keepdims=True)
        o_ref[...] = x2 * lax.rsqrt(ms + NORM_EPS) * gf_ref[...]


def _ffn(x1, g_ffn, g_final, mods3, rows, w_up, w_down):
    n, d = x1.shape
    dff = w_up.shape[1]
    tm = _tile(min(n, rows.rows_per_group if not rows.per_token else n), 512)
    tf = _tile(dff, 512, V7X_LANES)
    return pl.pallas_call(
        _ffn_kernel,
        grid=(n // tm, dff // tf),
        in_specs=[pl.BlockSpec((tm, d), lambda i, f: (i, 0), pipeline_mode=pl.Buffered(1)),
                  pl.BlockSpec((1, d), lambda i, f: (0, 0)),
                  rows.mod_spec(tm, d, 4),
                  rows.mod_spec(tm, d, 3),
                  rows.mod_spec(tm, d, 5),
                  pl.BlockSpec((1, d), lambda i, f: (0, 0)),
                  pl.BlockSpec((d, tf), lambda i, f: (0, f)),
                  pl.BlockSpec((tf, d), lambda i, f: (f, 0))],
        out_specs=pl.BlockSpec((tm, d), lambda i, f: (i, 0), pipeline_mode=pl.Buffered(1)),
        out_shape=jax.ShapeDtypeStruct((n, d), F32),
        scratch_shapes=[pltpu.VMEM((tm, d), BF16), pltpu.VMEM((tm, d), F32)],
        compiler_params=_params("parallel", "arbitrary"),
        name="ffn",
    )(x1, g_ffn.reshape(1, d), mods3, mods3, mods3, g_final.reshape(1, d), w_up, w_down)


QPAD = 8
INDEX_PAGES_PER_DOT = 8


def _sample_index_kernel(pt_ref, qi_ref, wi_ref, kin_ref, expand_ref, cache_ref, bias_ref,
                         kbuf_ref, sem_ref, score_ref, *, n_pages, page, t_len, k_top):
    b = pl.program_id(0)

    def page_copy(p):
        return pltpu.make_async_copy(cache_ref.at[pt_ref[b, p]], kbuf_ref.at[p], sem_ref.at[p])

    for p in range(n_pages):
        page_copy(p).start()

    qi = qi_ref[...]
    wi = wi_ref[...].reshape(N_IDX_HEADS, QPAD, 1)

    def head_sum(s):
        s3 = jnp.maximum(s, 0.0).reshape(N_IDX_HEADS, QPAD, s.shape[1]) * wi
        return jnp.sum(s3, axis=0)

    ppd = INDEX_PAGES_PER_DOT if n_pages % INDEX_PAGES_PER_DOT == 0 else 1
    for p0 in range(0, n_pages, ppd):
        for p in range(p0, p0 + ppd):
            page_copy(p).wait()
        kt = jnp.concatenate([kbuf_ref[p] for p in range(p0, p0 + ppd)], axis=1).astype(BF16)
        sc = head_sum(_dot(qi, kt))
        for j in range(ppd):
            score_ref[p0 + j] = sc[:, j * page:(j + 1) * page]

    qrow = lax.broadcasted_iota(jnp.int32, (QPAD, page), 0)
    lane = lax.broadcasted_iota(jnp.int32, (QPAD, page), 1)
    s_new = head_sum(_dot(qi, kin_ref[...]))
    score_ref[n_pages] = jnp.where((lane <= qrow) & (lane < t_len), s_new, NEG_INF)

    n_chunks = n_pages + 1
    thr = _topk_threshold(lambda c: score_ref[c], n_chunks, float(k_top), (QPAD, page), axis=1)

    sel = [jnp.where(score_ref[c] >= thr, 1.0, 0.0) for c in range(n_chunks)]
    sel += [jnp.zeros((QPAD, page), F32)] * (n_chunks % 2)
    hit = _dot(jnp.concatenate(sel, axis=0).astype(BF16), expand_ref[...])
    bias = jnp.where(hit > 0.5, 0.0, NEG_INF)
    for c in range(n_chunks):
        bias_ref[c] = bias[c * QPAD:(c + 1) * QPAD]


def _sample_index(page_table, qi_s, wi_s, ki_new_t, expand, cache_kidx_t, t_len, k_top):
    bsz, n_pages = page_table.shape
    page = cache_kidx_t.shape[2]
    kernel = functools.partial(_sample_index_kernel, n_pages=n_pages, page=page, t_len=t_len, k_top=k_top)
    grid_spec = pltpu.PrefetchScalarGridSpec(
        num_scalar_prefetch=1,
        grid=(bsz,),
        in_specs=[pl.BlockSpec((None, N_IDX_HEADS * QPAD, IDX_DIM), lambda b, pt: (b, 0, 0)),
                  pl.BlockSpec((None, N_IDX_HEADS * QPAD, 1), lambda b, pt: (b, 0, 0)),
                  pl.BlockSpec((None, IDX_DIM, page), lambda b, pt: (b, 0, 0)),
                  pl.BlockSpec(expand.shape, lambda b, pt: (0, 0)),
                  pl.BlockSpec(memory_space=pl.ANY)],
        out_specs=pl.BlockSpec((None, n_pages + 1, QPAD, N_KV_HEADS * page), lambda b, pt: (b, 0, 0, 0)),
        scratch_shapes=[pltpu.VMEM((n_pages, IDX_DIM, page), F32),
                        pltpu.SemaphoreType.DMA((n_pages,)),
                        pltpu.VMEM((n_pages + 1, QPAD, page), F32)],
    )
    return pl.pallas_call(
        kernel,
        grid_spec=grid_spec,
        out_shape=jax.ShapeDtypeStruct((bsz, n_pages + 1, QPAD, N_KV_HEADS * page), F32),
        compiler_params=_params("arbitrary"),
        name="sample_index",
    )(page_table, qi_s, wi_s, ki_new_t, expand, cache_kidx_t)


PAGES_PER_CHUNK = 8


def _sample_attn_kernel(pt_ref, q_ref, bias_ref, gmask_ref, kn_ref, vn_ref, ck_ref, cv_ref, o_ref,
                        kbuf_ref, vbuf_ref, sem_ref, m_ref, l_ref, acc_ref, *, n_pages, ppc):
    b = pl.program_id(0)
    n_chunks = n_pages // ppc
    rows = N_HEADS * QPAD
    prow = kbuf_ref.shape[2]

    def copies(c, slot):
        out = []
        for j in range(ppc):
            pid = pt_ref[b, c * ppc + j]
            out.append(pltpu.make_async_copy(ck_ref.at[pid], kbuf_ref.at[slot, j], sem_ref.at[0, slot, j]))
            out.append(pltpu.make_async_copy(cv_ref.at[pid], vbuf_ref.at[slot, j], sem_ref.at[1, slot, j]))
        return out

    for cp in copies(0, 0):
        cp.start()

    q = q_ref[...]
    gmask = gmask_ref[...]
    m_ref[...] = jnp.full(m_ref.shape, NEG_INF, F32)
    l_ref[...] = jnp.zeros(l_ref.shape, F32)
    acc_ref[...] = jnp.zeros(acc_ref.shape, F32)
    scale = HEAD_DIM ** -0.5

    def attend(kch, vch, biases):
        s = _dot_nt(q, kch) * scale
        parts = []
        for j, bias in enumerate(biases):
            sj = s[:, j * prow:(j + 1) * prow].reshape(N_HEADS, QPAD, prow) + bias[None]
            parts.append(sj.reshape(rows, prow) + gmask)
        s = parts[0] if len(parts) == 1 else jnp.concatenate(parts, axis=1)
        m_old = m_ref[...]
        m_new = jnp.maximum(m_old, jnp.max(s, axis=1, keepdims=True))
        m_safe = jnp.where(m_new == NEG_INF, 0.0, m_new)
        p = jnp.exp(s - m_safe)
        alpha = jnp.exp(m_old - m_safe)
        l_ref[...] = alpha * l_ref[...] + jnp.sum(p, axis=1, keepdims=True)
        acc_ref[...] = alpha * acc_ref[...] + _dot(p.astype(BF16), vch)
        m_ref[...] = m_new

    def chunk_body(c, _):
        slot = c % 2

        @pl.when(c + 1 < n_chunks)
        def _():
            for cp in copies(c + 1, 1 - slot):
                cp.start()

        for cp in copies(c, slot):
            cp.wait()
        kch = kbuf_ref[slot].reshape(ppc * prow, HEAD_DIM).astype(BF16)
        vch = vbuf_ref[slot].reshape(ppc * prow, HEAD_DIM).astype(BF16)
        attend(kch, vch, [bias_ref[c * ppc + j] for j in range(ppc)])
        return 0

    lax.fori_loop(0, n_chunks, chunk_body, 0)
    attend(kn_ref[...], vn_ref[...], [bias_ref[n_pages]])
    o_ref[...] = acc_ref[...] / l_ref[...]


def _sample_attn(page_table, q_s, bias, gmask, k_new, v_new, cache_k, cache_v):
    bsz, n_pages = page_table.shape
    prow = cache_k.shape[1]
    ppc = PAGES_PER_CHUNK if n_pages % PAGES_PER_CHUNK == 0 else 1
    rows = N_HEADS * QPAD
    kernel = functools.partial(_sample_attn_kernel, n_pages=n_pages, ppc=ppc)
    grid_spec = pltpu.PrefetchScalarGridSpec(
        num_scalar_prefetch=1,
        grid=(bsz,),
        in_specs=[pl.BlockSpec((None, rows, HEAD_DIM), lambda b, pt: (b, 0, 0)),
                  pl.BlockSpec((None, n_pages + 1, QPAD, prow), lambda b, pt: (b, 0, 0, 0)),
                  pl.BlockSpec((rows, prow), lambda b, pt: (0, 0)),
                  pl.BlockSpec((None, prow, HEAD_DIM), lambda b, pt: (b, 0, 0)),
                  pl.BlockSpec((None, prow, HEAD_DIM), lambda b, pt: (b, 0, 0)),
                  pl.BlockSpec(memory_space=pl.ANY),
                  pl.BlockSpec(memory_space=pl.ANY)],
        out_specs=pl.BlockSpec((None, rows, HEAD_DIM), lambda b, pt: (b, 0, 0)),
        scratch_shapes=[pltpu.VMEM((2, ppc, prow, HEAD_DIM), F32),
                        pltpu.VMEM((2, ppc, prow, HEAD_DIM), F32),
                        pltpu.SemaphoreType.DMA((2, 2, ppc)),
                        pltpu.VMEM((rows, 1), F32),
                        pltpu.VMEM((rows, 1), F32),
                        pltpu.VMEM((rows, HEAD_DIM), F32)],
    )
    return pl.pallas_call(
        kernel,
        grid_spec=grid_spec,
        out_shape=jax.ShapeDtypeStruct((bsz, rows, HEAD_DIM), F32),
        compiler_params=_params("arbitrary"),
        name="sample_attn",
    )(page_table, q_s, bias, gmask, k_new, v_new, cache_k, cache_v)


def _rope_tables(pos, head_w):
    half = head_w // 2
    inv = ROPE_THETA ** (-jnp.arange(half, dtype=F32) / half)
    ang = pos.astype(F32)[:, None] * inv[None, :]
    cos, sin = jnp.cos(ang), jnp.sin(ang)
    reps = V7X_LANES // head_w
    cos_t = jnp.tile(jnp.concatenate([cos, cos], axis=1), (1, reps))
    sin_t = jnp.tile(jnp.concatenate([-sin, sin], axis=1), (1, reps))
    return cos_t, sin_t


def _split_w_in(w_in, d):
    dc = d // 2
    o = 3 * dc
    w = {}
    w["conv"] = w_in[:, :o].astype(BF16)
    w["q"] = w_in[:, o:o + D_ATTN].astype(BF16); o += D_ATTN
    w["k"] = w_in[:, o:o + D_KV].astype(BF16); o += D_KV
    w["v"] = w_in[:, o:o + D_KV].astype(BF16); o += D_KV
    w["qi"] = w_in[:, o:o + D_QI].astype(BF16); o += D_QI
    kiwi = w_in[:, o:o + IDX_DIM + N_IDX_HEADS]; o += IDX_DIM + N_IDX_HEADS
    w["kiwi"] = jnp.pad(kiwi, ((0, 0), (0, V7X_LANES - kiwi.shape[1]))).astype(BF16)
    w["gates"] = w_in[:, o:o + 2 * d].astype(BF16)
    return w


def _in_proj(h, w, pos):
    cos128, sin128 = _rope_tables(pos, HEAD_DIM)
    cos64, sin64 = _rope_tables(pos, IDX_DIM)
    lane = jnp.arange(V7X_LANES)
    cos_kw = jnp.where(lane < IDX_DIM, cos64, 1.0)
    sin_kw = jnp.where(lane < IDX_DIM, sin64, 0.0)
    zc = _mm(h, w["conv"], BF16, "in_conv")
    gates = _mm(h, w["gates"], BF16, "in_gates")
    q_hm = _mm_rope_q(h, w["q"], cos128, sin128, HEAD_DIM, "in_q")
    qi_hm = _mm_rope_q(h, w["qi"], cos64, sin64, IDX_DIM, "in_qi")
    k32, k16 = _mm_rope_kv(h, w["k"], cos128, sin128, HEAD_DIM, True, "in_k")
    v32, v16 = _mm_rope_kv(h, w["v"], cos128, sin128, HEAD_DIM, False, "in_v")
    kw32, kw16 = _mm_rope_kv(h, w["kiwi"], cos_kw, sin_kw, IDX_DIM, True, "in_kiwi")
    ki32 = kw32[:, :IDX_DIM]
    ki16 = kw16[:, :IDX_DIM]
    wi = kw32[:, IDX_DIM:IDX_DIM + N_IDX_HEADS] * (N_IDX_HEADS ** -0.5) * (IDX_DIM ** -0.5)
    return zc, gates, q_hm, qi_hm, k32, k16, v32, v16, ki32, ki16, wi


def kernel(x_prompt, x_sample, cache_k, cache_v, cache_kidx, state_conv, page_table, c_prompt, c_sample,
           w_ada, b_ada, g_mix, w_in, conv_w, w_conv_out, w_attn_out, w_out, g_ffn, w_up, w_down, g_final):
    depth = w_ada.shape[0]
    assert depth == 1, "single-layer trunk"
    bp, sp, d = x_prompt.shape
    bs, ts, _ = x_sample.shape
    dc = d // 2
    n_pool, page = cache_k.shape[1], cache_k.shape[2]
    n_pages = page_table.shape[1]
    past = n_pages * page
    assert ts <= QPAD and ts <= page

    c_all = jnp.concatenate([c_prompt, c_sample], axis=0)
    n_c = c_all.shape[0]
    c_all = jnp.pad(c_all, ((0, (-n_c) % 8), (0, 0)))
    mods = _ada(c_all, w_ada[0], b_ada[0])
    mods_p = mods[:bp].reshape(bp, 1, N_MOD * d)
    mods_s = jnp.repeat(mods[bp:bp + bs], ts, axis=0).reshape(1, bs * ts, N_MOD * d)
    rows_p = _Rows(bp * sp, sp, per_token=False)
    rows_s = _Rows(bs * ts, bs * ts, per_token=True)

    w = _split_w_in(w_in[0], d)
    wc16 = w_conv_out[0].astype(BF16)
    wa16 = w_attn_out[0].astype(BF16)
    wo16 = w_out[0].astype(BF16)
    wu16 = w_up[0].astype(BF16)
    wd16 = w_down[0].astype(BF16)

    xp = x_prompt.reshape(bp * sp, d)
    hp = _norm_mod_call(xp, g_mix[0], mods_p, rows_p, 1, 0)
    pos_p = jnp.tile(jnp.arange(sp), bp)
    zc, gates, q_hm, qi_hm, k32, k16, v32, v16, ki32, ki16, wi = _in_proj(hp, w, pos_p)
    u_p, conv_p = _conv_prompt(zc, conv_w[0], bp, sp)
    k_top_p = min(TOPK_MAX, sp // 4)
    kc_p = _tile(sp, 256)
    vt16 = v16.reshape(bp, sp // kc_p, kc_p, D_KV).transpose(0, 1, 3, 2)
    attn_p = _attn_prompt(qi_hm, wi.T, ki16, q_hm, k16, vt16, bp, sp, k_top_p)
    merged_p = _merge(u_p, attn_p, wc16, wa16, gates)
    x1_p = _out_proj(merged_p, wo16, xp, mods_p, rows_p)
    y_p = _ffn(x1_p, g_ffn[0], g_final, mods_p, rows_p, wu16, wd16)

    ns = bs * ts
    xs = x_sample.reshape(ns, d)
    hs = _norm_mod_call(xs, g_mix[0], mods_s, rows_s, 1, 0)
    pos_s = jnp.tile(past + jnp.arange(ts), bs)
    zc_s, gates_s, q_hm_s, qi_hm_s, ks32, ks16, vs32, vs16, kis32, kis16, wi_s = _in_proj(hs, w, pos_s)

    zc_t = zc_s.reshape(bs, ts, 3 * dc).transpose(1, 0, 2)
    st_t = state_conv[0].transpose(1, 0, 2)
    u_t, nst_t = _conv_sample(zc_t, st_t, conv_w[0])
    u_s = u_t.transpose(1, 0, 2).reshape(ns, dc)
    conv_s = nst_t.transpose(1, 0, 2)

    k_top_s = min(TOPK_MAX, (past + ts) // 4)
    pad_q = ((0, 0), (0, 0), (0, QPAD - ts), (0, 0))
    qi_s = jnp.pad(qi_hm_s.reshape(N_IDX_HEADS, bs, ts, IDX_DIM).transpose(1, 0, 2, 3), pad_q)
    qi_s = qi_s.reshape(bs, N_IDX_HEADS * QPAD, IDX_DIM)
    wi_sp = jnp.pad(wi_s.reshape(bs, ts, N_IDX_HEADS).transpose(0, 2, 1), ((0, 0), (0, 0), (0, QPAD - ts)))
    wi_sp = wi_sp.reshape(bs, N_IDX_HEADS * QPAD, 1)
    ki_new_t = jnp.pad(kis16.reshape(bs, ts, IDX_DIM).transpose(0, 2, 1), ((0, 0), (0, 0), (0, page - ts)))
    expand = jnp.repeat(jnp.eye(page, dtype=BF16), N_KV_HEADS, axis=1)
    kidx_t = jnp.swapaxes(cache_kidx.reshape(n_pool, page, IDX_DIM), 1, 2)
    bias_s = _sample_index(page_table, qi_s, wi_sp, ki_new_t, expand, kidx_t, ts, k_top_s)

    q_s = jnp.pad(q_hm_s.reshape(N_HEADS, bs, ts, HEAD_DIM).transpose(1, 0, 2, 3), pad_q)
    q_s = q_s.reshape(bs, N_HEADS * QPAD, HEAD_DIM)
    prow = page * N_KV_HEADS
    col_head = jnp.arange(prow) % N_KV_HEADS
    row_head = jnp.arange(N_HEADS * QPAD) // (KV_GROUP * QPAD)
    gmask = jnp.where(row_head[:, None] == col_head[None, :], 0.0, NEG_INF).astype(F32)
    pad_new = ((0, 0), (0, page - ts), (0, 0), (0, 0))
    k_new = jnp.pad(ks16.reshape(bs, ts, N_KV_HEADS, HEAD_DIM), pad_new).reshape(bs, prow, HEAD_DIM)
    v_new = jnp.pad(vs16.reshape(bs, ts, N_KV_HEADS, HEAD_DIM), pad_new).reshape(bs, prow, HEAD_DIM)
    ck = cache_k.reshape(n_pool, prow, HEAD_DIM)
    cv = cache_v.reshape(n_pool, prow, HEAD_DIM)
    attn_hq = _sample_attn(page_table, q_s, bias_s, gmask, k_new, v_new, ck, cv)
    attn_s = attn_hq.reshape(bs, N_HEADS, QPAD, HEAD_DIM)[:, :, :ts].transpose(0, 2, 1, 3)
    attn_s = attn_s.reshape(ns, D_ATTN).astype(BF16)

    merged_s = _merge(u_s, attn_s, wc16, wa16, gates_s)
    x1_s = _out_proj(merged_s, wo16, xs, mods_s, rows_s)
    y_s = _ffn(x1_s, g_ffn[0], g_final, mods_s, rows_s, wu16, wd16)

    return (y_p.reshape(bp, sp, d),
            y_s.reshape(bs, ts, d),
            k32.reshape(1, bp, sp, N_KV_HEADS, HEAD_DIM),
            v32.reshape(1, bp, sp, N_KV_HEADS, HEAD_DIM),
            ki32.reshape(1, bp, sp, IDX_DIM),
            conv_p.reshape(1, bp, CONV_W - 1, dc),
            ks32.reshape(1, bs, ts, N_KV_HEADS, HEAD_DIM),
            vs32.reshape(1, bs, ts, N_KV_HEADS, HEAD_DIM),
            kis32.reshape(1, bs, ts, IDX_DIM),
            conv_s.reshape(1, bs, CONV_W - 1, dc))
```

```python
import functools

import jax
import jax.numpy as jnp
from jax import lax
from jax.experimental import pallas as pl
from jax.experimental.pallas import tpu as pltpu

N_HEADS = 16
HEAD_DIM = 128
N_KV_HEADS = 4
KV_GROUP = N_HEADS // N_KV_HEADS
N_IDX_HEADS = 16
IDX_DIM = 64
TOPK_MAX = 256
CONV_W = 3
ROPE_THETA = 10000.0
NORM_EPS = 1e-6
N_MOD = 6
D_ATTN = N_HEADS * HEAD_DIM
D_KV = N_KV_HEADS * HEAD_DIM
D_QI = N_IDX_HEADS * IDX_DIM

V7X_LANES = 128
V7X_VMEM_LIMIT_BYTES = 56 * 1024 * 1024

BF16 = jnp.bfloat16
F32 = jnp.float32
NEG_INF = float("-inf")
LOG2_E = 1.4426950408889634

BISECT_STEPS = 22


def _tile(n, target, mult=8):
    if n <= target:
        return n
    t = (target // mult) * mult
    while t >= mult:
        if n % t == 0:
            return t
        t -= mult
    return n


def _params(*sem):
    return pltpu.CompilerParams(dimension_semantics=sem, vmem_limit_bytes=V7X_VMEM_LIMIT_BYTES)


def _dot(a, b):
    return jnp.dot(a, b, preferred_element_type=F32)


def _dot_nt(a, b):
    return lax.dot_general(a, b, (((1,), (1,)), ((), ())), preferred_element_type=F32)


def _ada_kernel(c_ref, w_ref, b_ref, o_ref):
    c = c_ref[...]
    a = (c * jax.nn.sigmoid(c)).astype(BF16)
    o_ref[...] = _dot(a, w_ref[...].astype(BF16)) + b_ref[...]


def _ada(c, w, b):
    m, d = c.shape
    n = w.shape[1]
    tn = _tile(n, 512, V7X_LANES)
    return pl.pallas_call(
        _ada_kernel,
        grid=(n // tn,),
        in_specs=[pl.BlockSpec((m, d), lambda j: (0, 0)),
                  pl.BlockSpec((d, tn), lambda j: (0, j)),
                  pl.BlockSpec((1, tn), lambda j: (0, j))],
        out_specs=pl.BlockSpec((m, tn), lambda j: (0, j)),
        out_shape=jax.ShapeDtypeStruct((m, n), F32),
        compiler_params=_params("arbitrary"),
        name="ada",
    )(c, w, b.reshape(1, n))


class _Rows:
    def __init__(self, n_rows, rows_per_group, per_token):
        self.n = n_rows
        self.rows_per_group = rows_per_group
        self.per_token = per_token

    def mod_spec(self, tm, d, which, ncol_blocks_per_mod=1, swapped=False):
        tn = d // ncol_blocks_per_mod
        rpg = self.rows_per_group
        if self.per_token:
            shape = (None, tm, tn)
            where = lambda i: (0, i)
        else:
            shape = (None, 1, tn)
            where = lambda i: ((i * tm) // rpg, 0)
        if not swapped:
            return pl.BlockSpec(shape, lambda i, *_: (*where(i), which * ncol_blocks_per_mod))
        return pl.BlockSpec(shape, lambda j, i: (*where(i), which * ncol_blocks_per_mod + j))


def _norm_mod(x, g, sc, sh):
    ms = jnp.mean(x * x, axis=-1, keepdims=True)
    y = x * lax.rsqrt(ms + NORM_EPS) * g
    return y * (1.0 + sc) + sh


def _norm_mod_kernel(x_ref, g_ref, sc_ref, sh_ref, o_ref):
    o_ref[...] = _norm_mod(x_ref[...], g_ref[...], sc_ref[...], sh_ref[...]).astype(o_ref.dtype)


def _norm_mod_call(x, g, mods3, rows, which_sc, which_sh):
    n, d = x.shape
    tm = _tile(min(n, rows.rows_per_group if not rows.per_token else n), 256)
    return pl.pallas_call(
        _norm_mod_kernel,
        grid=(n // tm,),
        in_specs=[pl.BlockSpec((tm, d), lambda i: (i, 0)),
                  pl.BlockSpec((1, d), lambda i: (0, 0)),
                  rows.mod_spec(tm, d, which_sc),
                  rows.mod_spec(tm, d, which_sh)],
        out_specs=pl.BlockSpec((tm, d), lambda i: (i, 0)),
        out_shape=jax.ShapeDtypeStruct((n, d), BF16),
        compiler_params=_params("parallel"),
        name="norm_mod",
    )(x, g.reshape(1, d), mods3, mods3)


def _stationary_weights(w_ref, wb_ref):
    if wb_ref is None:
        return w_ref[...]

    @pl.when(pl.program_id(1) == 0)
    def _():
        wb_ref[...] = w_ref[...].astype(BF16)

    return wb_ref[...]


def _proj_kernel(x_ref, w_ref, *rest, epilogue, n_extra, n_out):
    extra, outs, scratch = rest[:n_extra], rest[n_extra:n_extra + n_out], rest[n_extra + n_out:]
    w = _stationary_weights(w_ref, scratch[0] if scratch else None)
    epilogue(_dot(x_ref[...], w), extra, outs)


def _proj(x, w, col0, m, *, tm, tn, epilogue, extra, extra_specs, out_shape, out_specs, name):
    n, k = x.shape
    assert n % tm == 0 and m % tn == 0 and col0 % tn == 0
    jb = col0 // tn
    cast = w.dtype != BF16
    outs = out_shape if isinstance(out_shape, (list, tuple)) else [out_shape]
    return pl.pallas_call(
        functools.partial(_proj_kernel, epilogue=epilogue, n_extra=len(extra), n_out=len(outs)),
        grid=(m // tn, n // tm),
        in_specs=[pl.BlockSpec((tm, k), lambda j, i: (i, 0)),
                  pl.BlockSpec((k, tn), lambda j, i: (0, jb + j))] + list(extra_specs),
        out_specs=out_specs,
        out_shape=out_shape,
        scratch_shapes=[pltpu.VMEM((k, tn), BF16)] if cast else [],
        compiler_params=_params("parallel", "arbitrary"),
        name=name,
    )(x, w, *extra)


PROJ_TM = 1024
PROJ_TN = 512


def _cast_epilogue(acc, extra, outs):
    outs[0][...] = acc.astype(outs[0].dtype)


def _mm(x, w, col0, m, out_dtype, name):
    n = x.shape[0]
    tm, tn = _tile(n, PROJ_TM), _tile(m, PROJ_TN, V7X_LANES)
    return _proj(x, w, col0, m, tm=tm, tn=tn, epilogue=_cast_epilogue, extra=(), extra_specs=(),
                 out_shape=jax.ShapeDtypeStruct((n, m), out_dtype),
                 out_specs=pl.BlockSpec((tm, tn), lambda j, i: (i, j)), name=name)


def _swap_halves(x, head_w):
    if head_w == V7X_LANES:
        return pltpu.roll(x, V7X_LANES // 2, axis=1)
    half = head_w // 2
    lane = lax.broadcasted_iota(jnp.int32, x.shape, 1)
    take_upper = (lane & (head_w - 1)) < half
    return jnp.where(take_upper, pltpu.roll(x, V7X_LANES - half, axis=1), pltpu.roll(x, half, axis=1))


def _rope_cols(acc, cos, sin, head_w):
    out = []
    for c in range(acc.shape[1] // V7X_LANES):
        x = acc[:, c * V7X_LANES:(c + 1) * V7X_LANES]
        out.append(x * cos + _swap_halves(x, head_w) * sin)
    return out


def _rope_q_epilogue(acc, extra, outs, *, head_w):
    cos_ref, sin_ref = extra
    o_ref = outs[0]
    cols = _rope_cols(acc, cos_ref[...], sin_ref[...], head_w)
    per = V7X_LANES // head_w
    for c, x in enumerate(cols):
        for p in range(per):
            o_ref[c * per + p] = x[:, p * head_w:(p + 1) * head_w].astype(o_ref.dtype)


def _mm_rope_q(x, w, col0, m, cos, sin, head_w, name):
    n = x.shape[0]
    tm, tn = _tile(n, PROJ_TM), _tile(m, PROJ_TN, V7X_LANES)
    hpt = tn // head_w
    table = pl.BlockSpec((tm, V7X_LANES), lambda j, i: (i, 0))
    return _proj(x, w, col0, m, tm=tm, tn=tn, epilogue=functools.partial(_rope_q_epilogue, head_w=head_w),
                 extra=(cos, sin), extra_specs=(table, table),
                 out_shape=jax.ShapeDtypeStruct((m // head_w, n, head_w), BF16),
                 out_specs=pl.BlockSpec((hpt, tm, head_w), lambda j, i: (j, i, 0)), name=name)


def _rope_kv_epilogue(acc, extra, outs, *, head_w, rope):
    cos_ref, sin_ref = extra
    o32_ref, o16_ref = outs
    if rope:
        cols = _rope_cols(acc, cos_ref[...], sin_ref[...], head_w)
        for c, x in enumerate(cols):
            o32_ref[:, c * V7X_LANES:(c + 1) * V7X_LANES] = x
            o16_ref[:, c * V7X_LANES:(c + 1) * V7X_LANES] = x.astype(BF16)
    else:
        o32_ref[...] = acc
        o16_ref[...] = acc.astype(BF16)


def _mm_rope_kv(x, w, col0, m, cos, sin, head_w, rope, name):
    n = x.shape[0]
    tm, tn = _tile(n, PROJ_TM), _tile(m, PROJ_TN, V7X_LANES)
    table = pl.BlockSpec((tm, V7X_LANES), lambda j, i: (i, 0))
    tile = pl.BlockSpec((tm, tn), lambda j, i: (i, j))
    return _proj(x, w, col0, m, tm=tm, tn=tn,
                 epilogue=functools.partial(_rope_kv_epilogue, head_w=head_w, rope=rope),
                 extra=(cos, sin), extra_specs=(table, table),
                 out_shape=[jax.ShapeDtypeStruct((n, m), F32), jax.ShapeDtypeStruct((n, m), BF16)],
                 out_specs=[tile, tile], name=name)


PREV_ROWS = 16


def _conv_prompt_kernel(xin_ref, bg_ref, cg_ref, xin_prev_ref, cg_prev_ref, w_ref, u_ref, st_ref,
                        *, blocks_per_seq):
    i = pl.program_id(1)
    p = cg_ref[...].astype(F32) * xin_ref[...].astype(F32)
    prev = cg_prev_ref[...].astype(F32) * xin_prev_ref[...].astype(F32)
    prev = jnp.where(i % blocks_per_seq == 0, 0.0, prev)
    tm = p.shape[0]
    last = PREV_ROWS - 1
    row = lax.broadcasted_iota(jnp.int32, p.shape, 0)
    p1 = pltpu.roll(p, 1, axis=0)
    p1 = jnp.where(row == 0, prev[last:last + 1], p1)
    p2 = pltpu.roll(p, 2, axis=0)
    p2 = jnp.where(row == 0, prev[last - 1:last], jnp.where(row == 1, prev[last:last + 1], p2))
    w = w_ref[...]
    y = p2 * w[0:1] + p1 * w[1:2] + p * w[2:3]
    u_ref[...] = (bg_ref[...].astype(F32) * y).astype(u_ref.dtype)
    st_ref[...] = p[tm - (CONV_W - 1):, :]


def _conv_prompt(zc, conv_w, n_seq, seq):
    n = zc.shape[0]
    dc = zc.shape[1] // 3
    tm = _tile(seq, 512)
    tc = _tile(dc, 512, V7X_LANES)
    ncb = dc // tc
    bps = seq // tm
    prev_row_blk = lambda i: jnp.maximum(i * (tm // PREV_ROWS) - 1, 0)
    u, st = pl.pallas_call(
        functools.partial(_conv_prompt_kernel, blocks_per_seq=bps),
        grid=(ncb, n // tm),
        in_specs=[pl.BlockSpec((tm, tc), lambda j, i: (i, j)),
                  pl.BlockSpec((tm, tc), lambda j, i: (i, ncb + j)),
                  pl.BlockSpec((tm, tc), lambda j, i: (i, 2 * ncb + j)),
                  pl.BlockSpec((PREV_ROWS, tc), lambda j, i: (prev_row_blk(i), j)),
                  pl.BlockSpec((PREV_ROWS, tc), lambda j, i: (prev_row_blk(i), 2 * ncb + j)),
                  pl.BlockSpec((CONV_W, tc), lambda j, i: (0, j))],
        out_specs=[pl.BlockSpec((tm, tc), lambda j, i: (i, j)),
                   pl.BlockSpec((None, CONV_W - 1, tc), lambda j, i: (i // bps, 0, j))],
        out_shape=[jax.ShapeDtypeStruct((n, dc), BF16),
                   jax.ShapeDtypeStruct((n_seq, CONV_W - 1, dc), F32)],
        compiler_params=_params("parallel", "arbitrary"),
        name="conv_prompt",
    )(zc, zc, zc, zc, zc, conv_w)
    return u, st


def _conv_sample_kernel(xin_ref, bg_ref, cg_ref, st_ref, w_ref, u_ref, nst_ref):
    t_len = xin_ref.shape[0]
    w = w_ref[...]
    ext = [st_ref[j] for j in range(CONV_W - 1)]
    ext += [cg_ref[t].astype(F32) * xin_ref[t].astype(F32) for t in range(t_len)]
    for t in range(t_len):
        y = ext[t] * w[0:1]
        for j in range(1, CONV_W):
            y = y + ext[t + j] * w[j:j + 1]
        u_ref[t] = (bg_ref[t].astype(F32) * y).astype(u_ref.dtype)
    for j in range(CONV_W - 1):
        nst_ref[j] = ext[t_len + j]


def _conv_sample(zc_t, st_t, conv_w):
    t_len, b, dc3 = zc_t.shape
    dc = dc3 // 3
    tc = _tile(dc, 512, V7X_LANES)
    ncb = dc // tc
    return pl.pallas_call(
        _conv_sample_kernel,
        grid=(ncb,),
        in_specs=[pl.BlockSpec((t_len, b, tc), lambda j: (0, 0, j)),
                  pl.BlockSpec((t_len, b, tc), lambda j: (0, 0, ncb + j)),
                  pl.BlockSpec((t_len, b, tc), lambda j: (0, 0, 2 * ncb + j)),
                  pl.BlockSpec((CONV_W - 1, b, tc), lambda j: (0, 0, j)),
                  pl.BlockSpec((CONV_W, tc), lambda j: (0, j))],
        out_specs=[pl.BlockSpec((t_len, b, tc), lambda j: (0, 0, j)),
                   pl.BlockSpec((CONV_W - 1, b, tc), lambda j: (0, 0, j))],
        out_shape=[jax.ShapeDtypeStruct((t_len, b, dc), BF16),
                   jax.ShapeDtypeStruct((CONV_W - 1, b, dc), F32)],
        compiler_params=_params("parallel"),
        name="conv_sample",
    )(zc_t, zc_t, zc_t, st_t, conv_w)


_SUBLANES = 8
_REDUCERS = {jnp.add: jnp.sum, jnp.maximum: jnp.max, jnp.minimum: jnp.min}


def _fold_chunks(n_chunks, term, combine, init, shape, axis):
    if axis == 0 and shape[0] > _SUBLANES:
        full_term = term
        shape = (_SUBLANES, shape[1])
        term = lambda c: _REDUCERS[combine](full_term(c).reshape(-1, *shape), axis=0)
    if isinstance(n_chunks, int):
        vals = [term(c) for c in range(n_chunks)]
        while len(vals) > 1:
            vals = [combine(vals[i], vals[i + 1]) if i + 1 < len(vals) else vals[i]
                    for i in range(0, len(vals), 2)]
        return vals[0]
    return lax.fori_loop(0, n_chunks, lambda c, acc: combine(acc, term(c)), jnp.full(shape, init, F32))


def _topk_threshold(score_at, n_chunks, want, shape, axis):
    fold = functools.partial(_fold_chunks, n_chunks, shape=shape, axis=axis)

    def count_ge(thr):
        part = fold(lambda c: jnp.where(score_at(c) >= thr, 1.0, 0.0), jnp.add, 0.0)
        return jnp.sum(part, axis=axis, keepdims=True)

    hi = jnp.max(fold(score_at, jnp.maximum, NEG_INF), axis=axis, keepdims=True)
    lo = jnp.min(fold(lambda c: jnp.where(score_at(c) == NEG_INF, jnp.inf, score_at(c)), jnp.minimum, jnp.inf),
                 axis=axis, keepdims=True)
    cnt_lo = count_ge(lo)
    cnt_hi = count_ge(hi)
    top_ok = cnt_hi >= want
    lo = jnp.where(top_ok, hi, lo)
    cnt_lo = jnp.where(top_ok, cnt_hi, cnt_lo)

    def bisect(_, carry):
        lo, hi, cnt_lo = carry
        mid = lo + (hi - lo) * 0.5
        cnt = count_ge(mid)
        ok = cnt >= want
        return (jnp.where(ok, mid, lo), jnp.where(ok, hi, mid), jnp.where(ok, cnt, cnt_lo))

    lo, hi, cnt_lo = lax.fori_loop(0, BISECT_STEPS, bisect, (lo, hi, cnt_lo))

    def pending_of(cnt_lo, done):
        return jnp.max(jnp.where((cnt_lo > want) & (done == 0.0), 1.0, 0.0))

    def finish(carry):
        lo, hi, cnt_lo, done, _ = carry
        cand = jnp.max(fold(lambda c: jnp.where(score_at(c) < hi, score_at(c), NEG_INF), jnp.maximum, NEG_INF),
                       axis=axis, keepdims=True)
        cnt = count_ge(cand)
        active = (cnt_lo > want) & (done == 0.0)
        ok = active & (cnt >= want)
        lo = jnp.where(ok, cand, lo)
        cnt_lo = jnp.where(ok, cnt, cnt_lo)
        done = jnp.where(ok, 1.0, done)
        hi = jnp.where(active & (cnt < want), cand, hi)
        return lo, hi, cnt_lo, done, pending_of(cnt_lo, done)

    done0 = jnp.where(lo == hi, 1.0, 0.0)
    lo, *_ = lax.while_loop(lambda carry: carry[4] > 0.0, finish,
                            (lo, hi, cnt_lo, done0, pending_of(cnt_lo, done0)))
    return lo


def _attn_prompt_kernel(qi_ref, wi_ref, ki_ref, q_ref, k_ref, vt_ref, o_ref,
                        score_ref, m_ref, l_ref, acc_ref, *, tq, kc, k_top):
    qb = pl.program_id(1)
    n_chunks = (qb * tq) // kc + tq // kc
    qpos = qb * tq + lax.broadcasted_iota(jnp.int32, (1, tq), 1)
    wi = wi_ref[...]

    def fill(c, _):
        kic = ki_ref[pl.ds(pl.multiple_of(c * kc, kc), kc), :]
        acc = jnp.zeros((kc, tq), F32)
        for h in range(N_IDX_HEADS):
            acc = acc + jnp.maximum(_dot_nt(kic, qi_ref[h]), 0.0) * wi[h:h + 1]
        kpos = c * kc + lax.broadcasted_iota(jnp.int32, (kc, tq), 0)
        score_ref[c] = jnp.where(kpos <= qpos, acc, NEG_INF)
        return 0

    lax.fori_loop(0, n_chunks, fill, 0)

    want = jnp.minimum(qpos + 1, k_top).astype(F32)
    thr = _topk_threshold(lambda c: score_ref[c], n_chunks, want, (kc, tq), axis=0)

    scale = HEAD_DIM ** -0.5 * LOG2_E
    for g in range(N_KV_HEADS):
        m_ref[...] = jnp.full(m_ref.shape, NEG_INF, F32)
        l_ref[...] = jnp.zeros(l_ref.shape, F32)
        acc_ref[...] = jnp.zeros(acc_ref.shape, F32)

        def attend(c, _):
            kch = k_ref[pl.ds(pl.multiple_of(c * kc, kc), kc), g * HEAD_DIM:(g + 1) * HEAD_DIM]
            vt = vt_ref[c, g * HEAD_DIM:(g + 1) * HEAD_DIM, :]
            bias = jnp.where(score_ref[c] >= thr, 0.0, NEG_INF)
            heads = range(KV_GROUP)
            m_old = [m_ref[r] for r in heads]
            l_old = [l_ref[r] for r in heads]
            s = [_dot_nt(kch, q_ref[g * KV_GROUP + r]) * scale + bias for r in heads]
            m_new = [jnp.maximum(m_old[r], jnp.max(s[r], axis=0, keepdims=True)) for r in heads]
            m_safe = [jnp.where(m_new[r] == NEG_INF, 0.0, m_new[r]) for r in heads]
            p = [jnp.exp2(s[r] - m_safe[r]) for r in heads]
            alpha = [jnp.exp2(m_old[r] - m_safe[r]) for r in heads]
            pv = [_dot(vt, p[r].astype(BF16)) for r in heads]
            for r in heads:
                l_ref[r] = alpha[r] * l_old[r] + jnp.sum(p[r], axis=0, keepdims=True)
                acc_ref[r] = alpha[r] * acc_ref[r] + pv[r]
                m_ref[r] = m_new[r]
            return 0

        lax.fori_loop(0, n_chunks, attend, 0)
        for r in range(KV_GROUP):
            h = g * KV_GROUP + r
            out = acc_ref[r] / l_ref[r]
            o_ref[:, h * HEAD_DIM:(h + 1) * HEAD_DIM] = out.T.astype(o_ref.dtype)


def _attn_prompt(qi_hm, wi_t, ki16, q_hm, k16, vt16, n_seq, seq, k_top):
    n = n_seq * seq
    tq = _tile(seq, 256)
    kc = tq
    nqb = seq // tq
    assert vt16.shape == (n_seq, seq // kc, D_KV, kc)
    kernel = functools.partial(_attn_prompt_kernel, tq=tq, kc=kc, k_top=k_top)
    return pl.pallas_call(
        kernel,
        grid=(n_seq, nqb),
        in_specs=[pl.BlockSpec((N_IDX_HEADS, tq, IDX_DIM), lambda b, q: (0, b * nqb + q, 0)),
                  pl.BlockSpec((N_IDX_HEADS, tq), lambda b, q: (0, b * nqb + q)),
                  pl.BlockSpec((seq, IDX_DIM), lambda b, q: (b, 0)),
                  pl.BlockSpec((N_HEADS, tq, HEAD_DIM), lambda b, q: (0, b * nqb + q, 0)),
                  pl.BlockSpec((seq, D_KV), lambda b, q: (b, 0)),
                  pl.BlockSpec((None, seq // kc, D_KV, kc), lambda b, q: (b, 0, 0, 0))],
        out_specs=pl.BlockSpec((tq, D_ATTN), lambda b, q: (b * nqb + q, 0)),
        out_shape=jax.ShapeDtypeStruct((n, D_ATTN), BF16),
        scratch_shapes=[pltpu.VMEM((seq // kc, kc, tq), F32),
                        pltpu.VMEM((KV_GROUP, 1, tq), F32),
                        pltpu.VMEM((KV_GROUP, 1, tq), F32),
                        pltpu.VMEM((KV_GROUP, HEAD_DIM, tq), F32)],
        compiler_params=_params("parallel", "arbitrary"),
        name="attn_prompt",
    )(qi_hm, wi_t, ki16, q_hm, k16, vt16)


def _merge_kernel(u_ref, a_ref, wc_ref, wa_ref, gc_ref, ga_ref, o_ref, wcb_ref, wab_ref):
    yc = _dot(u_ref[...], _stationary_weights(wc_ref, wcb_ref))
    ya = _dot(a_ref[...], _stationary_weights(wa_ref, wab_ref))
    gc = jax.nn.sigmoid(gc_ref[...].astype(F32))
    ga = jax.nn.sigmoid(ga_ref[...].astype(F32))
    o_ref[...] = (gc * yc + ga * ya).astype(o_ref.dtype)


def _merge(u, attn, wc, wa, gates):
    n, dc = u.shape
    da = attn.shape[1]
    d = wc.shape[1]
    tm = _tile(n, PROJ_TM)
    tn = _tile(d, PROJ_TN, V7X_LANES)
    ncb = d // tn
    return pl.pallas_call(
        _merge_kernel,
        grid=(ncb, n // tm),
        in_specs=[pl.BlockSpec((tm, dc), lambda j, i: (i, 0)),
                  pl.BlockSpec((tm, da), lambda j, i: (i, 0)),
                  pl.BlockSpec((dc, tn), lambda j, i: (0, j)),
                  pl.BlockSpec((da, tn), lambda j, i: (0, j)),
                  pl.BlockSpec((tm, tn), lambda j, i: (i, j)),
                  pl.BlockSpec((tm, tn), lambda j, i: (i, ncb + j))],
        out_specs=pl.BlockSpec((tm, tn), lambda j, i: (i, j)),
        out_shape=jax.ShapeDtypeStruct((n, d), BF16),
        scratch_shapes=[pltpu.VMEM((dc, tn), BF16), pltpu.VMEM((da, tn), BF16)],
        compiler_params=_params("parallel", "arbitrary"),
        name="merge",
    )(u, attn, wc, wa, gates, gates)


def _out_proj_epilogue(acc, extra, outs):
    x_ref, gt_ref = extra
    outs[0][...] = x_ref[...] + gt_ref[...] * acc


def _out_proj(a, w, x, mods3, rows):
    n = a.shape[0]
    d = w.shape[1]
    tm = _tile(min(n, rows.rows_per_group if not rows.per_token else n), PROJ_TM)
    tn = _tile(d, PROJ_TN, V7X_LANES)
    tile = pl.BlockSpec((tm, tn), lambda j, i: (i, j))
    gate = rows.mod_spec(tm, d, 2, d // tn, swapped=True)
    return _proj(a, w, 0, d, tm=tm, tn=tn, epilogue=_out_proj_epilogue, extra=(x, mods3), extra_specs=(tile, gate),
                 out_shape=jax.ShapeDtypeStruct((n, d), F32), out_specs=tile, name="out_proj")


def _ffn_kernel(x_ref, g_ref, sc_ref, sh_ref, gt_ref, gf_ref, wu_ref, wd_ref, o_ref, h_ref, acc_ref):
    f = pl.program_id(1)

    @pl.when(f == 0)
    def _():
        h_ref[...] = _norm_mod(x_ref[...], g_ref[...], sc_ref[...], sh_ref[...]).astype(BF16)
        acc_ref[...] = jnp.zeros(acc_ref.shape, F32)

    u = jnp.maximum(_dot(h_ref[...], wu_ref[...]), 0.0)
    acc_ref[...] += _dot((u * u).astype(BF16), wd_ref[...])

    @pl.when(f == pl.num_programs(1) - 1)
    def _():
        x2 = x_ref[...] + gt_ref[...] * acc_ref[...]
        ms = jnp.mean(x2 * x2, axis=-1, keepdims=True)
        o_ref[...] = x2 * lax.rsqrt(ms + NORM_EPS) * gf_ref[...]


def _ffn(x1, g_ffn, g_final, mods3, rows, w_up, w_down):
    n, d = x1.shape
    dff = w_up.shape[1]
    tm = _tile(min(n, rows.rows_per_group if not rows.per_token else n), 512)
    tf = _tile(dff, 512, V7X_LANES)
    return pl.pallas_call(
        _ffn_kernel,
        grid=(n // tm, dff // tf),
        in_specs=[pl.BlockSpec((tm, d), lambda i, f: (i, 0), pipeline_mode=pl.Buffered(1)),
                  pl.BlockSpec((1, d), lambda i, f: (0, 0)),
                  rows.mod_spec(tm, d, 4),
                  rows.mod_spec(tm, d, 3),
                  rows.mod_spec(tm, d, 5),
                  pl.BlockSpec((1, d), lambda i, f: (0, 0)),
                  pl.BlockSpec((d, tf), lambda i, f: (0, f)),
                  pl.BlockSpec((tf, d), lambda i, f: (f, 0))],
        out_specs=pl.BlockSpec((tm, d), lambda i, f: (i, 0), pipeline_mode=pl.Buffered(1)),
        out_shape=jax.ShapeDtypeStruct((n, d), F32),
        scratch_shapes=[pltpu.VMEM((tm, d), BF16), pltpu.VMEM((tm, d), F32)],
        compiler_params=_params("parallel", "arbitrary"),
        name="ffn",
    )(x1, g_ffn.reshape(1, d), mods3, mods3, mods3, g_final.reshape(1, d), w_up, w_down)


QPAD = 8
INDEX_PAGES_PER_DOT = 8


def _sample_index_kernel(pt_ref, qi_ref, wi_ref, kin_ref, expand_ref, cache_ref, bias_ref,
                         kbuf_ref, sem_ref, score_ref, *, n_pages, page, t_len, k_top):
    b = pl.program_id(0)

    def page_copy(p):
        return pltpu.make_async_copy(cache_ref.at[pt_ref[b, p]], kbuf_ref.at[p], sem_ref.at[p])

    for p in range(n_pages):
        page_copy(p).start()

    qi = qi_ref[...]
    wi = wi_ref[...].reshape(N_IDX_HEADS, QPAD, 1)

    def head_sum(s):
        s3 = jnp.maximum(s, 0.0).reshape(N_IDX_HEADS, QPAD, s.shape[1]) * wi
        return jnp.sum(s3, axis=0)

    ppd = INDEX_PAGES_PER_DOT if n_pages % INDEX_PAGES_PER_DOT == 0 else 1
    for p0 in range(0, n_pages, ppd):
        for p in range(p0, p0 + ppd):
            page_copy(p).wait()
        kt = jnp.concatenate([kbuf_ref[p] for p in range(p0, p0 + ppd)], axis=1).astype(BF16)
        sc = head_sum(_dot(qi, kt))
        for j in range(ppd):
            score_ref[p0 + j] = sc[:, j * page:(j + 1) * page]

    qrow = lax.broadcasted_iota(jnp.int32, (QPAD, page), 0)
    lane = lax.broadcasted_iota(jnp.int32, (QPAD, page), 1)
    s_new = head_sum(_dot(qi, kin_ref[...]))
    score_ref[n_pages] = jnp.where((lane <= qrow) & (lane < t_len), s_new, NEG_INF)

    n_chunks = n_pages + 1
    thr = _topk_threshold(lambda c: score_ref[c], n_chunks, float(k_top), (QPAD, page), axis=1)

    sel = [jnp.where(score_ref[c] >= thr, 1.0, 0.0) for c in range(n_chunks)]
    sel += [jnp.zeros((QPAD, page), F32)] * (n_chunks % 2)
    hit = _dot(jnp.concatenate(sel, axis=0).astype(BF16), expand_ref[...])
    bias = jnp.where(hit > 0.5, 0.0, NEG_INF)
    for c in range(n_chunks):
        bias_ref[c] = bias[c * QPAD:(c + 1) * QPAD]


def _sample_index(page_table, qi_s, wi_s, ki_new_t, expand, cache_kidx_t, t_len, k_top):
    bsz, n_pages = page_table.shape
    page = cache_kidx_t.shape[2]
    kernel = functools.partial(_sample_index_kernel, n_pages=n_pages, page=page, t_len=t_len, k_top=k_top)
    grid_spec = pltpu.PrefetchScalarGridSpec(
        num_scalar_prefetch=1,
        grid=(bsz,),
        in_specs=[pl.BlockSpec((None, N_IDX_HEADS * QPAD, IDX_DIM), lambda b, pt: (b, 0, 0)),
                  pl.BlockSpec((None, N_IDX_HEADS * QPAD, 1), lambda b, pt: (b, 0, 0)),
                  pl.BlockSpec((None, IDX_DIM, page), lambda b, pt: (b, 0, 0)),
                  pl.BlockSpec(expand.shape, lambda b, pt: (0, 0)),
                  pl.BlockSpec(memory_space=pl.ANY)],
        out_specs=pl.BlockSpec((None, n_pages + 1, QPAD, N_KV_HEADS * page), lambda b, pt: (b, 0, 0, 0)),
        scratch_shapes=[pltpu.VMEM((n_pages, IDX_DIM, page), F32),
                        pltpu.SemaphoreType.DMA((n_pages,)),
                        pltpu.VMEM((n_pages + 1, QPAD, page), F32)],
    )
    return pl.pallas_call(
        kernel,
        grid_spec=grid_spec,
        out_shape=jax.ShapeDtypeStruct((bsz, n_pages + 1, QPAD, N_KV_HEADS * page), F32),
        compiler_params=_params("arbitrary"),
        name="sample_index",
    )(page_table, qi_s, wi_s, ki_new_t, expand, cache_kidx_t)


PAGES_PER_CHUNK = 8


def _sample_attn_kernel(pt_ref, q_ref, bias_ref, gmask_ref, kn_ref, vn_ref, ck_ref, cv_ref, o_ref,
                        kbuf_ref, vbuf_ref, sem_ref, m_ref, l_ref, acc_ref, *, n_pages, ppc):
    b = pl.program_id(0)
    n_chunks = n_pages // ppc
    rows = N_HEADS * QPAD
    prow = kbuf_ref.shape[2]

    def copies(bb, c, slot):
        out = []
        for j in range(ppc):
            pid = pt_ref[bb, c * ppc + j]
            out.append(pltpu.make_async_copy(ck_ref.at[pid], kbuf_ref.at[slot, j], sem_ref.at[0, slot, j]))
            out.append(pltpu.make_async_copy(cv_ref.at[pid], vbuf_ref.at[slot, j], sem_ref.at[1, slot, j]))
        return out

    @pl.when(b == 0)
    def _():
        for cp in copies(0, 0, 0):
            cp.start()

    q = q_ref[...]
    gmask = gmask_ref[...]
    m_ref[...] = jnp.full(m_ref.shape, NEG_INF, F32)
    l_ref[...] = jnp.zeros(l_ref.shape, F32)
    acc_ref[...] = jnp.zeros(acc_ref.shape, F32)
    scale = HEAD_DIM ** -0.5 * LOG2_E

    def attend(kch, vch, biases):
        s = _dot_nt(q, kch) * scale
        parts = []
        for j, bias in enumerate(biases):
            sj = s[:, j * prow:(j + 1) * prow].reshape(N_HEADS, QPAD, prow) + bias[None]
            parts.append(sj.reshape(rows, prow) + gmask)
        s = parts[0] if len(parts) == 1 else jnp.concatenate(parts, axis=1)
        m_old = m_ref[...]
        m_new = jnp.maximum(m_old, jnp.max(s, axis=1, keepdims=True))
        m_safe = jnp.where(m_new == NEG_INF, 0.0, m_new)
        p = jnp.exp2(s - m_safe)
        alpha = jnp.exp2(m_old - m_safe)
        l_ref[...] = alpha * l_ref[...] + jnp.sum(p, axis=1, keepdims=True)
        acc_ref[...] = alpha * acc_ref[...] + _dot(p.astype(BF16), vch)
        m_ref[...] = m_new

    def chunk_body(c, _):
        slot = (b * n_chunks + c) % 2

        @pl.when(c + 1 < n_chunks)
        def _():
            for cp in copies(b, c + 1, 1 - slot):
                cp.start()

        @pl.when((c + 1 == n_chunks) & (b + 1 < pl.num_programs(0)))
        def _():
            for cp in copies(b + 1, 0, 1 - slot):
                cp.start()

        for cp in copies(b, c, slot):
            cp.wait()
        kch = kbuf_ref[slot].reshape(ppc * prow, HEAD_DIM).astype(BF16)
        vch = vbuf_ref[slot].reshape(ppc * prow, HEAD_DIM).astype(BF16)
        attend(kch, vch, [bias_ref[c * ppc + j] for j in range(ppc)])
        return 0

    lax.fori_loop(0, n_chunks, chunk_body, 0)
    attend(kn_ref[...], vn_ref[...], [bias_ref[n_pages]])
    o_ref[...] = acc_ref[...] / l_ref[...]


def _sample_attn(page_table, q_s, bias, gmask, k_new, v_new, cache_k, cache_v):
    bsz, n_pages = page_table.shape
    prow = cache_k.shape[1]
    ppc = PAGES_PER_CHUNK if n_pages % PAGES_PER_CHUNK == 0 else 1
    rows = N_HEADS * QPAD
    kernel = functools.partial(_sample_attn_kernel, n_pages=n_pages, ppc=ppc)
    grid_spec = pltpu.PrefetchScalarGridSpec(
        num_scalar_prefetch=1,
        grid=(bsz,),
        in_specs=[pl.BlockSpec((None, rows, HEAD_DIM), lambda b, pt: (b, 0, 0)),
                  pl.BlockSpec((None, n_pages + 1, QPAD, prow), lambda b, pt: (b, 0, 0, 0)),
                  pl.BlockSpec((rows, prow), lambda b, pt: (0, 0)),
                  pl.BlockSpec((None, prow, HEAD_DIM), lambda b, pt: (b, 0, 0)),
                  pl.BlockSpec((None, prow, HEAD_DIM), lambda b, pt: (b, 0, 0)),
                  pl.BlockSpec(memory_space=pl.ANY),
                  pl.BlockSpec(memory_space=pl.ANY)],
        out_specs=pl.BlockSpec((None, rows, HEAD_DIM), lambda b, pt: (b, 0, 0)),
        scratch_shapes=[pltpu.VMEM((2, ppc, prow, HEAD_DIM), F32),
                        pltpu.VMEM((2, ppc, prow, HEAD_DIM), F32),
                        pltpu.SemaphoreType.DMA((2, 2, ppc)),
                        pltpu.VMEM((rows, 1), F32),
                        pltpu.VMEM((rows, 1), F32),
                        pltpu.VMEM((rows, HEAD_DIM), F32)],
    )
    return pl.pallas_call(
        kernel,
        grid_spec=grid_spec,
        out_shape=jax.ShapeDtypeStruct((bsz, rows, HEAD_DIM), F32),
        compiler_params=_params("arbitrary"),
        name="sample_attn",
    )(page_table, q_s, bias, gmask, k_new, v_new, cache_k, cache_v)


def _rope_tables(pos, head_w):
    half = head_w // 2
    inv = ROPE_THETA ** (-jnp.arange(half, dtype=F32) / half)
    ang = pos.astype(F32)[:, None] * inv[None, :]
    cos, sin = jnp.cos(ang), jnp.sin(ang)
    reps = V7X_LANES // head_w
    cos_t = jnp.tile(jnp.concatenate([cos, cos], axis=1), (1, reps))
    sin_t = jnp.tile(jnp.concatenate([-sin, sin], axis=1), (1, reps))
    return cos_t, sin_t


class _InWeights:
    def __init__(self, w_in, d):
        dc = d // 2
        self.full = w_in
        self.conv = 0
        self.q = 3 * dc
        self.k = self.q + D_ATTN
        self.v = self.k + D_KV
        self.qi = self.v + D_KV
        tail = self.qi + D_QI
        self.aligned = all(o % PROJ_TN == 0 for o in (self.q, self.k, self.v, self.qi))
        n_kw = IDX_DIM + N_IDX_HEADS
        self.kiwi = jnp.pad(w_in[:, tail:tail + n_kw], ((0, 0), (0, V7X_LANES - n_kw))).astype(BF16)
        self.gates = w_in[:, tail + n_kw:tail + n_kw + 2 * d].astype(BF16)

    def group(self, col0, m):
        if self.aligned:
            return self.full, col0
        return self.full[:, col0:col0 + m].astype(BF16), 0


def _in_proj(h, w, pos, d):
    cos128, sin128 = _rope_tables(pos, HEAD_DIM)
    cos64, sin64 = _rope_tables(pos, IDX_DIM)
    lane = jnp.arange(V7X_LANES)
    cos_kw = jnp.where(lane < IDX_DIM, cos64, 1.0)
    sin_kw = jnp.where(lane < IDX_DIM, sin64, 0.0)
    dc3 = 3 * (d // 2)
    zc = _mm(h, *w.group(w.conv, dc3), dc3, BF16, "in_conv")
    gates = _mm(h, w.gates, 0, 2 * d, BF16, "in_gates")
    q_hm = _mm_rope_q(h, *w.group(w.q, D_ATTN), D_ATTN, cos128, sin128, HEAD_DIM, "in_q")
    qi_hm = _mm_rope_q(h, *w.group(w.qi, D_QI), D_QI, cos64, sin64, IDX_DIM, "in_qi")
    k32, k16 = _mm_rope_kv(h, *w.group(w.k, D_KV), D_KV, cos128, sin128, HEAD_DIM, True, "in_k")
    v32, v16 = _mm_rope_kv(h, *w.group(w.v, D_KV), D_KV, cos128, sin128, HEAD_DIM, False, "in_v")
    kw32, kw16 = _mm_rope_kv(h, w.kiwi, 0, V7X_LANES, cos_kw, sin_kw, IDX_DIM, True, "in_kiwi")
    ki32 = kw32[:, :IDX_DIM]
    ki16 = kw16[:, :IDX_DIM]
    wi = kw32[:, IDX_DIM:IDX_DIM + N_IDX_HEADS] * (N_IDX_HEADS ** -0.5) * (IDX_DIM ** -0.5)
    return zc, gates, q_hm, qi_hm, k32, k16, v32, v16, ki32, ki16, wi


def kernel(x_prompt, x_sample, cache_k, cache_v, cache_kidx, state_conv, page_table, c_prompt, c_sample,
           w_ada, b_ada, g_mix, w_in, conv_w, w_conv_out, w_attn_out, w_out, g_ffn, w_up, w_down, g_final):
    depth = w_ada.shape[0]
    assert depth == 1, "single-layer trunk"
    bp, sp, d = x_prompt.shape
    bs, ts, _ = x_sample.shape
    dc = d // 2
    n_pool, page = cache_k.shape[1], cache_k.shape[2]
    n_pages = page_table.shape[1]
    past = n_pages * page
    assert ts <= QPAD and ts <= page

    c_all = jnp.concatenate([c_prompt, c_sample], axis=0)
    n_c = c_all.shape[0]
    c_all = jnp.pad(c_all, ((0, (-n_c) % 8), (0, 0)))
    mods = _ada(c_all, w_ada[0], b_ada[0])
    mods_p = mods[:bp].reshape(bp, 1, N_MOD * d)
    mods_s = jnp.repeat(mods[bp:bp + bs], ts, axis=0).reshape(1, bs * ts, N_MOD * d)
    rows_p = _Rows(bp * sp, sp, per_token=False)
    rows_s = _Rows(bs * ts, bs * ts, per_token=True)

    w = _InWeights(w_in[0], d)
    wc, wa, wo = w_conv_out[0], w_attn_out[0], w_out[0]
    wu16 = w_up[0].astype(BF16)
    wd16 = w_down[0].astype(BF16)

    xp = x_prompt.reshape(bp * sp, d)
    hp = _norm_mod_call(xp, g_mix[0], mods_p, rows_p, 1, 0)
    pos_p = jnp.tile(jnp.arange(sp), bp)
    zc, gates, q_hm, qi_hm, k32, k16, v32, v16, ki32, ki16, wi = _in_proj(hp, w, pos_p, d)
    u_p, conv_p = _conv_prompt(zc, conv_w[0], bp, sp)
    k_top_p = min(TOPK_MAX, sp // 4)
    kc_p = _tile(sp, 256)
    vt16 = v16.reshape(bp, sp // kc_p, kc_p, D_KV).transpose(0, 1, 3, 2)
    attn_p = _attn_prompt(qi_hm, wi.T, ki16, q_hm, k16, vt16, bp, sp, k_top_p)
    merged_p = _merge(u_p, attn_p, wc, wa, gates)
    x1_p = _out_proj(merged_p, wo, xp, mods_p, rows_p)
    y_p = _ffn(x1_p, g_ffn[0], g_final, mods_p, rows_p, wu16, wd16)

    ns = bs * ts
    xs = x_sample.reshape(ns, d)
    hs = _norm_mod_call(xs, g_mix[0], mods_s, rows_s, 1, 0)
    pos_s = jnp.tile(past + jnp.arange(ts), bs)
    zc_s, gates_s, q_hm_s, qi_hm_s, ks32, ks16, vs32, vs16, kis32, kis16, wi_s = _in_proj(hs, w, pos_s, d)

    zc_t = zc_s.reshape(bs, ts, 3 * dc).transpose(1, 0, 2)
    st_t = state_conv[0].transpose(1, 0, 2)
    u_t, nst_t = _conv_sample(zc_t, st_t, conv_w[0])
    u_s = u_t.transpose(1, 0, 2).reshape(ns, dc)
    conv_s = nst_t.transpose(1, 0, 2)

    k_top_s = min(TOPK_MAX, (past + ts) // 4)
    pad_q = ((0, 0), (0, 0), (0, QPAD - ts), (0, 0))
    qi_s = jnp.pad(qi_hm_s.reshape(N_IDX_HEADS, bs, ts, IDX_DIM).transpose(1, 0, 2, 3), pad_q)
    qi_s = qi_s.reshape(bs, N_IDX_HEADS * QPAD, IDX_DIM)
    wi_sp = jnp.pad(wi_s.reshape(bs, ts, N_IDX_HEADS).transpose(0, 2, 1), ((0, 0), (0, 0), (0, QPAD - ts)))
    wi_sp = wi_sp.reshape(bs, N_IDX_HEADS * QPAD, 1)
    ki_new_t = jnp.pad(kis16.reshape(bs, ts, IDX_DIM).transpose(0, 2, 1), ((0, 0), (0, 0), (0, page - ts)))
    expand = jnp.repeat(jnp.eye(page, dtype=BF16), N_KV_HEADS, axis=1)
    kidx_t = jnp.swapaxes(cache_kidx.reshape(n_pool, page, IDX_DIM), 1, 2)
    bias_s = _sample_index(page_table, qi_s, wi_sp, ki_new_t, expand, kidx_t, ts, k_top_s)

    q_s = jnp.pad(q_hm_s.reshape(N_HEADS, bs, ts, HEAD_DIM).transpose(1, 0, 2, 3), pad_q)
    q_s = q_s.reshape(bs, N_HEADS * QPAD, HEAD_DIM)
    prow = page * N_KV_HEADS
    col_head = jnp.arange(prow) % N_KV_HEADS
    row_head = jnp.arange(N_HEADS * QPAD) // (KV_GROUP * QPAD)
    gmask = jnp.where(row_head[:, None] == col_head[None, :], 0.0, NEG_INF).astype(F32)
    pad_new = ((0, 0), (0, page - ts), (0, 0), (0, 0))
    k_new = jnp.pad(ks16.reshape(bs, ts, N_KV_HEADS, HEAD_DIM), pad_new).reshape(bs, prow, HEAD_DIM)
    v_new = jnp.pad(vs16.reshape(bs, ts, N_KV_HEADS, HEAD_DIM), pad_new).reshape(bs, prow, HEAD_DIM)
    ck = cache_k.reshape(n_pool, prow, HEAD_DIM)
    cv = cache_v.reshape(n_pool, prow, HEAD_DIM)
    attn_hq = _sample_attn(page_table, q_s, bias_s, gmask, k_new, v_new, ck, cv)
    attn_s = attn_hq.reshape(bs, N_HEADS, QPAD, HEAD_DIM)[:, :, :ts].transpose(0, 2, 1, 3)
    attn_s = attn_s.reshape(ns, D_ATTN).astype(BF16)

    merged_s = _merge(u_s, attn_s, wc, wa, gates_s)
    x1_s = _out_proj(merged_s, wo, xs, mods_s, rows_s)
    y_s = _ffn(x1_s, g_ffn[0], g_final, mods_s, rows_s, wu16, wd16)

    return (y_p.reshape(bp, sp, d),
            y_s.reshape(bs, ts, d),
            k32.reshape(1, bp, sp, N_KV_HEADS, HEAD_DIM),
            v32.reshape(1, bp, sp, N_KV_HEADS, HEAD_DIM),
            ki32.reshape(1, bp, sp, IDX_DIM),
            conv_p.reshape(1, bp, CONV_W - 1, dc),
            ks32.reshape(1, bs, ts, N_KV_HEADS, HEAD_DIM),
            vs32.reshape(1, bs, ts, N_KV_HEADS, HEAD_DIM),
            kis32.reshape(1, bs, ts, IDX_DIM),
            conv_s.reshape(1, bs, CONV_W - 1, dc))
```

```python
import functools

import jax
import jax.numpy as jnp
from jax import lax
from jax.experimental import pallas as pl
from jax.experimental.pallas import tpu as pltpu

N_HEADS = 16
HEAD_DIM = 128
N_KV_HEADS = 4
KV_GROUP = N_HEADS // N_KV_HEADS
N_IDX_HEADS = 16
IDX_DIM = 64
TOPK_MAX = 256
CONV_W = 3
ROPE_THETA = 10000.0
NORM_EPS = 1e-6
N_MOD = 6
D_ATTN = N_HEADS * HEAD_DIM
D_KV = N_KV_HEADS * HEAD_DIM
D_QI = N_IDX_HEADS * IDX_DIM

V7X_LANES = 128
V7X_VMEM_LIMIT_BYTES = 56 * 1024 * 1024

BF16 = jnp.bfloat16
F32 = jnp.float32
NEG_INF = float("-inf")
LOG2_E = 1.4426950408889634

BISECT_STEPS = 17


def _tile(n, target, mult=8):
    if n <= target:
        return n
    t = (target // mult) * mult
    while t >= mult:
        if n % t == 0:
            return t
        t -= mult
    return n


def _params(*sem):
    return pltpu.CompilerParams(dimension_semantics=sem, vmem_limit_bytes=V7X_VMEM_LIMIT_BYTES)


def _dot(a, b):
    return jnp.dot(a, b, preferred_element_type=F32)


def _dot_nt(a, b):
    return lax.dot_general(a, b, (((1,), (1,)), ((), ())), preferred_element_type=F32)


def _ada_kernel(c_ref, w_ref, b_ref, o_ref):
    c = c_ref[...]
    a = (c * jax.nn.sigmoid(c)).astype(BF16)
    o_ref[...] = _dot(a, w_ref[...].astype(BF16)) + b_ref[...]


def _ada(c, w, b):
    m, d = c.shape
    n = w.shape[1]
    tn = _tile(n, 512, V7X_LANES)
    return pl.pallas_call(
        _ada_kernel,
        grid=(n // tn,),
        in_specs=[pl.BlockSpec((m, d), lambda j: (0, 0)),
                  pl.BlockSpec((d, tn), lambda j: (0, j)),
                  pl.BlockSpec((1, tn), lambda j: (0, j))],
        out_specs=pl.BlockSpec((m, tn), lambda j: (0, j)),
        out_shape=jax.ShapeDtypeStruct((m, n), F32),
        compiler_params=_params("arbitrary"),
        name="ada",
    )(c, w, b.reshape(1, n))


class _Rows:
    def __init__(self, n_rows, rows_per_group, per_token):
        self.n = n_rows
        self.rows_per_group = rows_per_group
        self.per_token = per_token

    def mod_spec(self, tm, d, which, ncol_blocks_per_mod=1, tiled=False):
        tn = d // ncol_blocks_per_mod
        rpg = self.rows_per_group
        if self.per_token:
            shape = (None, tm, tn)
            where = lambda i: (0, i)
        else:
            shape = (None, 1, tn)
            where = lambda i: ((i * tm) // rpg, 0)
        if not tiled:
            return pl.BlockSpec(shape, lambda i, *_: (*where(i), which * ncol_blocks_per_mod))
        return pl.BlockSpec(shape, lambda i, j: (*where(i), which * ncol_blocks_per_mod + j))


def _norm_mod(x, g, sc, sh):
    ms = jnp.mean(x * x, axis=-1, keepdims=True)
    y = x * lax.rsqrt(ms + NORM_EPS) * g
    return y * (1.0 + sc) + sh


def _norm_mod_kernel(x_ref, g_ref, sc_ref, sh_ref, o_ref):
    o_ref[...] = _norm_mod(x_ref[...], g_ref[...], sc_ref[...], sh_ref[...]).astype(o_ref.dtype)


def _norm_mod_call(x, g, mods3, rows, which_sc, which_sh):
    n, d = x.shape
    tm = _tile(min(n, rows.rows_per_group if not rows.per_token else n), 256)
    return pl.pallas_call(
        _norm_mod_kernel,
        grid=(n // tm,),
        in_specs=[pl.BlockSpec((tm, d), lambda i: (i, 0)),
                  pl.BlockSpec((1, d), lambda i: (0, 0)),
                  rows.mod_spec(tm, d, which_sc),
                  rows.mod_spec(tm, d, which_sh)],
        out_specs=pl.BlockSpec((tm, d), lambda i: (i, 0)),
        out_shape=jax.ShapeDtypeStruct((n, d), BF16),
        compiler_params=_params("parallel"),
        name="norm_mod",
    )(x, g.reshape(1, d), mods3, mods3)


def _proj_kernel(x_ref, w_ref, *rest, epilogue, n_extra):
    epilogue(_dot(x_ref[...], w_ref[...]), rest[:n_extra], rest[n_extra:])


def _proj(x, w, col0, m, *, tm, tn, epilogue, extra, extra_specs, out_shape, out_specs, name):
    n, k = x.shape
    assert n % tm == 0 and m % tn == 0 and col0 % tn == 0 and w.dtype == BF16
    jb = col0 // tn
    return pl.pallas_call(
        functools.partial(_proj_kernel, epilogue=epilogue, n_extra=len(extra)),
        grid=(n // tm, m // tn),
        in_specs=[pl.BlockSpec((tm, k), lambda i, j: (i, 0)),
                  pl.BlockSpec((k, tn), lambda i, j: (0, jb + j))] + list(extra_specs),
        out_specs=out_specs,
        out_shape=out_shape,
        compiler_params=_params("parallel", "arbitrary"),
        name=name,
    )(x, w, *extra)


PROJ_TM = 1024
PROJ_TN = 1024


def _proj_tn(m):
    return _tile(m, PROJ_TN, V7X_LANES)


def _cast_epilogue(acc, extra, outs):
    outs[0][...] = acc.astype(outs[0].dtype)


def _mm(x, w, col0, m, out_dtype, name):
    n = x.shape[0]
    tm, tn = _tile(n, PROJ_TM), _proj_tn(m)
    return _proj(x, w, col0, m, tm=tm, tn=tn, epilogue=_cast_epilogue, extra=(), extra_specs=(),
                 out_shape=jax.ShapeDtypeStruct((n, m), out_dtype),
                 out_specs=pl.BlockSpec((tm, tn), lambda i, j: (i, j)), name=name)


def _swap_halves(x, head_w):
    if head_w == V7X_LANES:
        return pltpu.roll(x, V7X_LANES // 2, axis=1)
    half = head_w // 2
    lane = lax.broadcasted_iota(jnp.int32, x.shape, 1)
    take_upper = (lane & (head_w - 1)) < half
    return jnp.where(take_upper, pltpu.roll(x, V7X_LANES - half, axis=1), pltpu.roll(x, half, axis=1))


def _rope_cols(acc, cos, sin, head_w):
    out = []
    for c in range(acc.shape[1] // V7X_LANES):
        x = acc[:, c * V7X_LANES:(c + 1) * V7X_LANES]
        out.append(x * cos + _swap_halves(x, head_w) * sin)
    return out


def _rope_q_epilogue(acc, extra, outs, *, head_w):
    cos_ref, sin_ref = extra
    o_ref = outs[0]
    cols = _rope_cols(acc, cos_ref[...], sin_ref[...], head_w)
    per = V7X_LANES // head_w
    for c, x in enumerate(cols):
        for p in range(per):
            o_ref[c * per + p] = x[:, p * head_w:(p + 1) * head_w].astype(o_ref.dtype)


def _mm_rope_q(x, w, col0, m, cos, sin, head_w, name):
    n = x.shape[0]
    tm, tn = _tile(n, PROJ_TM), _proj_tn(m)
    hpt = tn // head_w
    table = pl.BlockSpec((tm, V7X_LANES), lambda i, j: (i, 0))
    return _proj(x, w, col0, m, tm=tm, tn=tn, epilogue=functools.partial(_rope_q_epilogue, head_w=head_w),
                 extra=(cos, sin), extra_specs=(table, table),
                 out_shape=jax.ShapeDtypeStruct((m // head_w, n, head_w), BF16),
                 out_specs=pl.BlockSpec((hpt, tm, head_w), lambda i, j: (j, i, 0)), name=name)


def _rope_kv_epilogue(acc, extra, outs, *, head_w, rope):
    cos_ref, sin_ref = extra
    o32_ref, o16_ref = outs
    if rope:
        cols = _rope_cols(acc, cos_ref[...], sin_ref[...], head_w)
        for c, x in enumerate(cols):
            o32_ref[:, c * V7X_LANES:(c + 1) * V7X_LANES] = x
            o16_ref[:, c * V7X_LANES:(c + 1) * V7X_LANES] = x.astype(BF16)
    else:
        o32_ref[...] = acc
        o16_ref[...] = acc.astype(BF16)


def _mm_rope_kv(x, w, col0, m, cos, sin, head_w, rope, name):
    n = x.shape[0]
    tm, tn = _tile(n, PROJ_TM), _proj_tn(m)
    table = pl.BlockSpec((tm, V7X_LANES), lambda i, j: (i, 0))
    tile = pl.BlockSpec((tm, tn), lambda i, j: (i, j))
    return _proj(x, w, col0, m, tm=tm, tn=tn,
                 epilogue=functools.partial(_rope_kv_epilogue, head_w=head_w, rope=rope),
                 extra=(cos, sin), extra_specs=(table, table),
                 out_shape=[jax.ShapeDtypeStruct((n, m), F32), jax.ShapeDtypeStruct((n, m), BF16)],
                 out_specs=[tile, tile], name=name)


PREV_ROWS = 16


def _conv_prompt_kernel(xin_ref, bg_ref, cg_ref, xin_prev_ref, cg_prev_ref, w_ref, u_ref, st_ref,
                        *, blocks_per_seq):
    i = pl.program_id(1)
    p = cg_ref[...].astype(F32) * xin_ref[...].astype(F32)
    prev = cg_prev_ref[...].astype(F32) * xin_prev_ref[...].astype(F32)
    prev = jnp.where(i % blocks_per_seq == 0, 0.0, prev)
    tm = p.shape[0]
    last = PREV_ROWS - 1
    row = lax.broadcasted_iota(jnp.int32, p.shape, 0)
    p1 = pltpu.roll(p, 1, axis=0)
    p1 = jnp.where(row == 0, prev[last:last + 1], p1)
    p2 = pltpu.roll(p, 2, axis=0)
    p2 = jnp.where(row == 0, prev[last - 1:last], jnp.where(row == 1, prev[last:last + 1], p2))
    w = w_ref[...]
    y = p2 * w[0:1] + p1 * w[1:2] + p * w[2:3]
    u_ref[...] = (bg_ref[...].astype(F32) * y).astype(u_ref.dtype)
    st_ref[...] = p[tm - (CONV_W - 1):, :]


def _conv_prompt(zc, conv_w, n_seq, seq):
    n = zc.shape[0]
    dc = zc.shape[1] // 3
    tm = _tile(seq, 512)
    tc = _tile(dc, 512, V7X_LANES)
    ncb = dc // tc
    bps = seq // tm
    prev_row_blk = lambda i: jnp.maximum(i * (tm // PREV_ROWS) - 1, 0)
    u, st = pl.pallas_call(
        functools.partial(_conv_prompt_kernel, blocks_per_seq=bps),
        grid=(ncb, n // tm),
        in_specs=[pl.BlockSpec((tm, tc), lambda j, i: (i, j)),
                  pl.BlockSpec((tm, tc), lambda j, i: (i, ncb + j)),
                  pl.BlockSpec((tm, tc), lambda j, i: (i, 2 * ncb + j)),
                  pl.BlockSpec((PREV_ROWS, tc), lambda j, i: (prev_row_blk(i), j)),
                  pl.BlockSpec((PREV_ROWS, tc), lambda j, i: (prev_row_blk(i), 2 * ncb + j)),
                  pl.BlockSpec((CONV_W, tc), lambda j, i: (0, j))],
        out_specs=[pl.BlockSpec((tm, tc), lambda j, i: (i, j)),
                   pl.BlockSpec((None, CONV_W - 1, tc), lambda j, i: (i // bps, 0, j))],
        out_shape=[jax.ShapeDtypeStruct((n, dc), BF16),
                   jax.ShapeDtypeStruct((n_seq, CONV_W - 1, dc), F32)],
        compiler_params=_params("parallel", "arbitrary"),
        name="conv_prompt",
    )(zc, zc, zc, zc, zc, conv_w)
    return u, st


def _conv_sample_kernel(xin_ref, bg_ref, cg_ref, st_ref, w_ref, u_ref, nst_ref):
    t_len = xin_ref.shape[0]
    w = w_ref[...]
    ext = [st_ref[j] for j in range(CONV_W - 1)]
    ext += [cg_ref[t].astype(F32) * xin_ref[t].astype(F32) for t in range(t_len)]
    for t in range(t_len):
        y = ext[t] * w[0:1]
        for j in range(1, CONV_W):
            y = y + ext[t + j] * w[j:j + 1]
        u_ref[t] = (bg_ref[t].astype(F32) * y).astype(u_ref.dtype)
    for j in range(CONV_W - 1):
        nst_ref[j] = ext[t_len + j]


def _conv_sample(zc_t, st_t, conv_w):
    t_len, b, dc3 = zc_t.shape
    dc = dc3 // 3
    tc = _tile(dc, 512, V7X_LANES)
    ncb = dc // tc
    return pl.pallas_call(
        _conv_sample_kernel,
        grid=(ncb,),
        in_specs=[pl.BlockSpec((t_len, b, tc), lambda j: (0, 0, j)),
                  pl.BlockSpec((t_len, b, tc), lambda j: (0, 0, ncb + j)),
                  pl.BlockSpec((t_len, b, tc), lambda j: (0, 0, 2 * ncb + j)),
                  pl.BlockSpec((CONV_W - 1, b, tc), lambda j: (0, 0, j)),
                  pl.BlockSpec((CONV_W, tc), lambda j: (0, j))],
        out_specs=[pl.BlockSpec((t_len, b, tc), lambda j: (0, 0, j)),
                   pl.BlockSpec((CONV_W - 1, b, tc), lambda j: (0, 0, j))],
        out_shape=[jax.ShapeDtypeStruct((t_len, b, dc), BF16),
                   jax.ShapeDtypeStruct((CONV_W - 1, b, dc), F32)],
        compiler_params=_params("parallel"),
        name="conv_sample",
    )(zc_t, zc_t, zc_t, st_t, conv_w)


_SUBLANES = 8
_REDUCERS = {jnp.add: jnp.sum, jnp.maximum: jnp.max, jnp.minimum: jnp.min}


def _fold_chunks(n_chunks, term, combine, init, shape, axis):
    if axis == 0 and shape[0] > _SUBLANES:
        full_term = term
        shape = (_SUBLANES, shape[1])
        term = lambda c: _REDUCERS[combine](full_term(c).reshape(-1, *shape), axis=0)
    if isinstance(n_chunks, int):
        vals = [term(c) for c in range(n_chunks)]
        while len(vals) > 1:
            vals = [combine(vals[i], vals[i + 1]) if i + 1 < len(vals) else vals[i]
                    for i in range(0, len(vals), 2)]
        return vals[0]
    return lax.fori_loop(0, n_chunks, lambda c, acc: combine(acc, term(c)), jnp.full(shape, init, F32))


def _topk_threshold(score_at, n_chunks, want, shape, axis):
    fold = functools.partial(_fold_chunks, n_chunks, shape=shape, axis=axis)

    def count_ge(thr):
        part = fold(lambda c: jnp.where(score_at(c) >= thr, 1.0, 0.0), jnp.add, 0.0)
        return jnp.sum(part, axis=axis, keepdims=True)

    hi = jnp.max(fold(score_at, jnp.maximum, NEG_INF), axis=axis, keepdims=True)
    lo = jnp.min(fold(lambda c: jnp.where(score_at(c) == NEG_INF, jnp.inf, score_at(c)), jnp.minimum, jnp.inf),
                 axis=axis, keepdims=True)
    cnt_lo = count_ge(lo)
    cnt_hi = count_ge(hi)
    top_ok = cnt_hi >= want
    lo = jnp.where(top_ok, hi, lo)
    cnt_lo = jnp.where(top_ok, cnt_hi, cnt_lo)

    def bisect(_, carry):
        lo, hi, cnt_lo = carry
        mid = lo + (hi - lo) * 0.5
        cnt = count_ge(mid)
        ok = cnt >= want
        return (jnp.where(ok, mid, lo), jnp.where(ok, hi, mid), jnp.where(ok, cnt, cnt_lo))

    lo, hi, cnt_lo = lax.fori_loop(0, BISECT_STEPS, bisect, (lo, hi, cnt_lo))

    def pending_of(cnt_lo, done):
        return jnp.max(jnp.where((cnt_lo > want) & (done == 0.0), 1.0, 0.0))

    def finish(carry):
        lo, hi, cnt_lo, done, _ = carry
        cand = jnp.max(fold(lambda c: jnp.where(score_at(c) < hi, score_at(c), NEG_INF), jnp.maximum, NEG_INF),
                       axis=axis, keepdims=True)
        cnt = count_ge(cand)
        active = (cnt_lo > want) & (done == 0.0)
        ok = active & (cnt >= want)
        lo = jnp.where(ok, cand, lo)
        cnt_lo = jnp.where(ok, cnt, cnt_lo)
        done = jnp.where(ok, 1.0, done)
        hi = jnp.where(active & (cnt < want), cand, hi)
        return lo, hi, cnt_lo, done, pending_of(cnt_lo, done)

    done0 = jnp.where(lo == hi, 1.0, 0.0)
    lo, *_ = lax.while_loop(lambda carry: carry[4] > 0.0, finish,
                            (lo, hi, cnt_lo, done0, pending_of(cnt_lo, done0)))
    return lo


def _attn_prompt_kernel(qi_ref, wi_ref, ki_ref, q_ref, k_ref, vt_ref, o_ref,
                        score_ref, m_ref, l_ref, acc_ref, *, tq, kc, k_top):
    qb = pl.program_id(1)
    n_chunks = (qb * tq) // kc + tq // kc
    qpos = qb * tq + lax.broadcasted_iota(jnp.int32, (1, tq), 1)
    wi = wi_ref[...]

    def fill(c, _):
        kic = ki_ref[pl.ds(pl.multiple_of(c * kc, kc), kc), :]
        acc = jnp.zeros((kc, tq), F32)
        for h in range(N_IDX_HEADS):
            acc = acc + jnp.maximum(_dot_nt(kic, qi_ref[h]), 0.0) * wi[h:h + 1]
        kpos = c * kc + lax.broadcasted_iota(jnp.int32, (kc, tq), 0)
        score_ref[c] = jnp.where(kpos <= qpos, acc, NEG_INF)
        return 0

    lax.fori_loop(0, n_chunks, fill, 0)

    want = jnp.minimum(qpos + 1, k_top).astype(F32)
    thr = _topk_threshold(lambda c: score_ref[c], n_chunks, want, (kc, tq), axis=0)

    scale = HEAD_DIM ** -0.5 * LOG2_E
    for g in range(N_KV_HEADS):
        m_ref[...] = jnp.full(m_ref.shape, NEG_INF, F32)
        l_ref[...] = jnp.zeros(l_ref.shape, F32)
        acc_ref[...] = jnp.zeros(acc_ref.shape, F32)

        def attend(c, _):
            kch = k_ref[pl.ds(pl.multiple_of(c * kc, kc), kc), g * HEAD_DIM:(g + 1) * HEAD_DIM]
            vt = vt_ref[c, g * HEAD_DIM:(g + 1) * HEAD_DIM, :]
            bias = jnp.where(score_ref[c] >= thr, 0.0, NEG_INF)
            heads = range(KV_GROUP)
            m_old = [m_ref[r] for r in heads]
            l_old = [l_ref[r] for r in heads]
            s = [_dot_nt(kch, q_ref[g * KV_GROUP + r]) * scale + bias for r in heads]
            m_new = [jnp.maximum(m_old[r], jnp.max(s[r], axis=0, keepdims=True)) for r in heads]
            m_safe = [jnp.where(m_new[r] == NEG_INF, 0.0, m_new[r]) for r in heads]
            p = [jnp.exp2(s[r] - m_safe[r]) for r in heads]
            alpha = [jnp.exp2(m_old[r] - m_safe[r]) for r in heads]
            pv = [_dot(vt, p[r].astype(BF16)) for r in heads]
            for r in heads:
                l_ref[r] = alpha[r] * l_old[r] + jnp.sum(p[r], axis=0, keepdims=True)
                acc_ref[r] = alpha[r] * acc_ref[r] + pv[r]
                m_ref[r] = m_new[r]
            return 0

        lax.fori_loop(0, n_chunks, attend, 0)
        for r in range(KV_GROUP):
            h = g * KV_GROUP + r
            out = acc_ref[r] / l_ref[r]
            o_ref[:, h * HEAD_DIM:(h + 1) * HEAD_DIM] = out.T.astype(o_ref.dtype)


def _attn_prompt(qi_hm, wi_t, ki16, q_hm, k16, vt16, n_seq, seq, k_top):
    n = n_seq * seq
    tq = _tile(seq, 256)
    kc = tq
    nqb = seq // tq
    assert vt16.shape == (n_seq, seq // kc, D_KV, kc)
    kernel = functools.partial(_attn_prompt_kernel, tq=tq, kc=kc, k_top=k_top)
    return pl.pallas_call(
        kernel,
        grid=(n_seq, nqb),
        in_specs=[pl.BlockSpec((N_IDX_HEADS, tq, IDX_DIM), lambda b, q: (0, b * nqb + q, 0)),
                  pl.BlockSpec((N_IDX_HEADS, tq), lambda b, q: (0, b * nqb + q)),
                  pl.BlockSpec((seq, IDX_DIM), lambda b, q: (b, 0)),
                  pl.BlockSpec((N_HEADS, tq, HEAD_DIM), lambda b, q: (0, b * nqb + q, 0)),
                  pl.BlockSpec((seq, D_KV), lambda b, q: (b, 0)),
                  pl.BlockSpec((None, seq // kc, D_KV, kc), lambda b, q: (b, 0, 0, 0))],
        out_specs=pl.BlockSpec((tq, D_ATTN), lambda b, q: (b * nqb + q, 0)),
        out_shape=jax.ShapeDtypeStruct((n, D_ATTN), BF16),
        scratch_shapes=[pltpu.VMEM((seq // kc, kc, tq), F32),
                        pltpu.VMEM((KV_GROUP, 1, tq), F32),
                        pltpu.VMEM((KV_GROUP, 1, tq), F32),
                        pltpu.VMEM((KV_GROUP, HEAD_DIM, tq), F32)],
        compiler_params=_params("parallel", "arbitrary"),
        name="attn_prompt",
    )(qi_hm, wi_t, ki16, q_hm, k16, vt16)


def _merge_kernel(u_ref, a_ref, wc_ref, wa_ref, gc_ref, ga_ref, o_ref):
    yc = _dot(u_ref[...], wc_ref[...])
    ya = _dot(a_ref[...], wa_ref[...])
    gc = jax.nn.sigmoid(gc_ref[...].astype(F32))
    ga = jax.nn.sigmoid(ga_ref[...].astype(F32))
    o_ref[...] = (gc * yc + ga * ya).astype(o_ref.dtype)


def _merge(u, attn, wc, wa, gates):
    n, dc = u.shape
    da = attn.shape[1]
    d = wc.shape[1]
    tm = _tile(n, PROJ_TM)
    tn = _proj_tn(d)
    ncb = d // tn
    return pl.pallas_call(
        _merge_kernel,
        grid=(n // tm, ncb),
        in_specs=[pl.BlockSpec((tm, dc), lambda i, j: (i, 0)),
                  pl.BlockSpec((tm, da), lambda i, j: (i, 0)),
                  pl.BlockSpec((dc, tn), lambda i, j: (0, j)),
                  pl.BlockSpec((da, tn), lambda i, j: (0, j)),
                  pl.BlockSpec((tm, tn), lambda i, j: (i, j)),
                  pl.BlockSpec((tm, tn), lambda i, j: (i, ncb + j))],
        out_specs=pl.BlockSpec((tm, tn), lambda i, j: (i, j)),
        out_shape=jax.ShapeDtypeStruct((n, d), BF16),
        compiler_params=_params("parallel", "arbitrary"),
        name="merge",
    )(u, attn, wc, wa, gates, gates)


def _out_proj_epilogue(acc, extra, outs):
    x_ref, gt_ref = extra
    outs[0][...] = x_ref[...] + gt_ref[...] * acc


def _out_proj(a, w, x, mods3, rows):
    n = a.shape[0]
    d = w.shape[1]
    tm = _tile(min(n, rows.rows_per_group if not rows.per_token else n), PROJ_TM)
    tn = _proj_tn(d)
    tile = pl.BlockSpec((tm, tn), lambda i, j: (i, j))
    gate = rows.mod_spec(tm, d, 2, d // tn, tiled=True)
    return _proj(a, w, 0, d, tm=tm, tn=tn, epilogue=_out_proj_epilogue, extra=(x, mods3), extra_specs=(tile, gate),
                 out_shape=jax.ShapeDtypeStruct((n, d), F32), out_specs=tile, name="out_proj")


def _ffn_kernel(x_ref, g_ref, sc_ref, sh_ref, gt_ref, gf_ref, wu_ref, wd_ref, o_ref, h_ref):
    f = pl.program_id(1)

    @pl.when(f == 0)
    def _():
        h_ref[...] = _norm_mod(x_ref[...], g_ref[...], sc_ref[...], sh_ref[...]).astype(BF16)
        o_ref[...] = jnp.zeros(o_ref.shape, F32)

    u = jnp.maximum(_dot(h_ref[...], wu_ref[...]), 0.0)
    o_ref[...] += _dot((u * u).astype(BF16), wd_ref[...])

    @pl.when(f == pl.num_programs(1) - 1)
    def _():
        x2 = x_ref[...] + gt_ref[...] * o_ref[...]
        ms = jnp.mean(x2 * x2, axis=-1, keepdims=True)
        o_ref[...] = x2 * lax.rsqrt(ms + NORM_EPS) * gf_ref[...]


def _ffn(x1, g_ffn, g_final, mods3, rows, w_up, w_down):
    n, d = x1.shape
    dff = w_up.shape[1]
    tm = _tile(min(n, rows.rows_per_group if not rows.per_token else n), 512)
    tf = _tile(dff, 512, V7X_LANES)
    return pl.pallas_call(
        _ffn_kernel,
        grid=(n // tm, dff // tf),
        in_specs=[pl.BlockSpec((tm, d), lambda i, f: (i, 0), pipeline_mode=pl.Buffered(1)),
                  pl.BlockSpec((1, d), lambda i, f: (0, 0)),
                  rows.mod_spec(tm, d, 4),
                  rows.mod_spec(tm, d, 3),
                  rows.mod_spec(tm, d, 5),
                  pl.BlockSpec((1, d), lambda i, f: (0, 0)),
                  pl.BlockSpec((d, tf), lambda i, f: (0, f)),
                  pl.BlockSpec((tf, d), lambda i, f: (f, 0))],
        out_specs=pl.BlockSpec((tm, d), lambda i, f: (i, 0)),
        out_shape=jax.ShapeDtypeStruct((n, d), F32),
        scratch_shapes=[pltpu.VMEM((tm, d), BF16)],
        compiler_params=_params("parallel", "arbitrary"),
        name="ffn",
    )(x1, g_ffn.reshape(1, d), mods3, mods3, mods3, g_final.reshape(1, d), w_up, w_down)


QPAD = 8
INDEX_PAGES_PER_DOT = 8


def _sample_index_kernel(pt_ref, qi_ref, wi_ref, kin_ref, expand_ref, cache_ref, bias_ref,
                         kbuf_ref, sem_ref, score_ref, *, n_pages, page, t_len, k_top):
    b = pl.program_id(0)

    def page_copy(p):
        return pltpu.make_async_copy(cache_ref.at[pt_ref[b, p]], kbuf_ref.at[p], sem_ref.at[p])

    for p in range(n_pages):
        page_copy(p).start()

    qi = qi_ref[...]
    wi = wi_ref[...].reshape(N_IDX_HEADS, QPAD, 1)

    def head_sum(s):
        s3 = jnp.maximum(s, 0.0).reshape(N_IDX_HEADS, QPAD, s.shape[1]) * wi
        return jnp.sum(s3, axis=0)

    ppd = INDEX_PAGES_PER_DOT if n_pages % INDEX_PAGES_PER_DOT == 0 else 1
    for p0 in range(0, n_pages, ppd):
        for p in range(p0, p0 + ppd):
            page_copy(p).wait()
        kt = jnp.concatenate([kbuf_ref[p] for p in range(p0, p0 + ppd)], axis=1).astype(BF16)
        sc = head_sum(_dot(qi, kt))
        for j in range(ppd):
            score_ref[p0 + j] = sc[:, j * page:(j + 1) * page]

    qrow = lax.broadcasted_iota(jnp.int32, (QPAD, page), 0)
    lane = lax.broadcasted_iota(jnp.int32, (QPAD, page), 1)
    s_new = head_sum(_dot(qi, kin_ref[...]))
    score_ref[n_pages] = jnp.where((lane <= qrow) & (lane < t_len), s_new, NEG_INF)

    n_chunks = n_pages + 1
    thr = _topk_threshold(lambda c: score_ref[c], n_chunks, float(k_top), (QPAD, page), axis=1)

    sel = [jnp.where(score_ref[c] >= thr, 1.0, 0.0) for c in range(n_chunks)]
    sel += [jnp.zeros((QPAD, page), F32)] * (n_chunks % 2)
    hit = _dot(jnp.concatenate(sel, axis=0).astype(BF16), expand_ref[...])
    bias = jnp.where(hit > 0.5, 0.0, NEG_INF)
    for c in range(n_chunks):
        bias_ref[c] = bias[c * QPAD:(c + 1) * QPAD]


def _sample_index(page_table, qi_s, wi_s, ki_new_t, expand, cache_kidx_t, t_len, k_top):
    bsz, n_pages = page_table.shape
    page = cache_kidx_t.shape[2]
    kernel = functools.partial(_sample_index_kernel, n_pages=n_pages, page=page, t_len=t_len, k_top=k_top)
    grid_spec = pltpu.PrefetchScalarGridSpec(
        num_scalar_prefetch=1,
        grid=(bsz,),
        in_specs=[pl.BlockSpec((None, N_IDX_HEADS * QPAD, IDX_DIM), lambda b, pt: (b, 0, 0)),
                  pl.BlockSpec((None, N_IDX_HEADS * QPAD, 1), lambda b, pt: (b, 0, 0)),
                  pl.BlockSpec((None, IDX_DIM, page), lambda b, pt: (b, 0, 0)),
                  pl.BlockSpec(expand.shape, lambda b, pt: (0, 0)),
                  pl.BlockSpec(memory_space=pl.ANY)],
        out_specs=pl.BlockSpec((None, n_pages + 1, QPAD, N_KV_HEADS * page), lambda b, pt: (b, 0, 0, 0)),
        scratch_shapes=[pltpu.VMEM((n_pages, IDX_DIM, page), F32),
                        pltpu.SemaphoreType.DMA((n_pages,)),
                        pltpu.VMEM((n_pages + 1, QPAD, page), F32)],
    )
    return pl.pallas_call(
        kernel,
        grid_spec=grid_spec,
        out_shape=jax.ShapeDtypeStruct((bsz, n_pages + 1, QPAD, N_KV_HEADS * page), F32),
        compiler_params=_params("arbitrary"),
        name="sample_index",
    )(page_table, qi_s, wi_s, ki_new_t, expand, cache_kidx_t)


PAGES_PER_CHUNK = 8
KV_SLOTS = 3


def _sample_attn_kernel(pt_ref, q_ref, bias_ref, gmask_ref, kn_ref, vn_ref, ck_ref, cv_ref, o_ref,
                        kbuf_ref, vbuf_ref, sem_ref, m_ref, l_ref, acc_ref, *, n_pages, ppc):
    b = pl.program_id(0)
    n_chunks = n_pages // ppc
    rows = N_HEADS * QPAD
    prow = kbuf_ref.shape[2]

    n_total = pl.num_programs(0) * n_chunks

    def copies(g):
        bb, c, slot = g // n_chunks, g % n_chunks, g % KV_SLOTS
        out = []
        for j in range(ppc):
            pid = pt_ref[bb, c * ppc + j]
            out.append(pltpu.make_async_copy(ck_ref.at[pid], kbuf_ref.at[slot, j], sem_ref.at[0, slot, j]))
            out.append(pltpu.make_async_copy(cv_ref.at[pid], vbuf_ref.at[slot, j], sem_ref.at[1, slot, j]))
        return out

    def start(g):
        @pl.when(g < n_total)
        def _():
            for cp in copies(g):
                cp.start()

    @pl.when(b == 0)
    def _():
        for g in range(KV_SLOTS - 1):
            start(g)

    q = q_ref[...]
    gmask = gmask_ref[...]
    m_ref[...] = jnp.full(m_ref.shape, NEG_INF, F32)
    l_ref[...] = jnp.zeros(l_ref.shape, F32)
    acc_ref[...] = jnp.zeros(acc_ref.shape, F32)
    scale = HEAD_DIM ** -0.5 * LOG2_E

    def attend(kch, vch, biases):
        s = _dot_nt(q, kch) * scale
        parts = []
        for j, bias in enumerate(biases):
            sj = s[:, j * prow:(j + 1) * prow].reshape(N_HEADS, QPAD, prow) + bias[None]
            parts.append(sj.reshape(rows, prow) + gmask)
        s = parts[0] if len(parts) == 1 else jnp.concatenate(parts, axis=1)
        m_old = m_ref[...]
        m_new = jnp.maximum(m_old, jnp.max(s, axis=1, keepdims=True))
        m_safe = jnp.where(m_new == NEG_INF, 0.0, m_new)
        p = jnp.exp2(s - m_safe)
        alpha = jnp.exp2(m_old - m_safe)
        l_ref[...] = alpha * l_ref[...] + jnp.sum(p, axis=1, keepdims=True)
        acc_ref[...] = alpha * acc_ref[...] + _dot(p.astype(BF16), vch)
        m_ref[...] = m_new

    def chunk_body(c, _):
        g = b * n_chunks + c
        slot = g % KV_SLOTS
        start(g + KV_SLOTS - 1)
        for cp in copies(g):
            cp.wait()
        kch = kbuf_ref[slot].reshape(ppc * prow, HEAD_DIM).astype(BF16)
        vch = vbuf_ref[slot].reshape(ppc * prow, HEAD_DIM).astype(BF16)
        attend(kch, vch, [bias_ref[c * ppc + j] for j in range(ppc)])
        return 0

    lax.fori_loop(0, n_chunks, chunk_body, 0)
    attend(kn_ref[...], vn_ref[...], [bias_ref[n_pages]])
    o_ref[...] = acc_ref[...] / l_ref[...]


def _sample_attn(page_table, q_s, bias, gmask, k_new, v_new, cache_k, cache_v):
    bsz, n_pages = page_table.shape
    prow = cache_k.shape[1]
    ppc = PAGES_PER_CHUNK if n_pages % PAGES_PER_CHUNK == 0 else 1
    rows = N_HEADS * QPAD
    kernel = functools.partial(_sample_attn_kernel, n_pages=n_pages, ppc=ppc)
    grid_spec = pltpu.PrefetchScalarGridSpec(
        num_scalar_prefetch=1,
        grid=(bsz,),
        in_specs=[pl.BlockSpec((None, rows, HEAD_DIM), lambda b, pt: (b, 0, 0)),
                  pl.BlockSpec((None, n_pages + 1, QPAD, prow), lambda b, pt: (b, 0, 0, 0)),
                  pl.BlockSpec((rows, prow), lambda b, pt: (0, 0)),
                  pl.BlockSpec((None, prow, HEAD_DIM), lambda b, pt: (b, 0, 0)),
                  pl.BlockSpec((None, prow, HEAD_DIM), lambda b, pt: (b, 0, 0)),
                  pl.BlockSpec(memory_space=pl.ANY),
                  pl.BlockSpec(memory_space=pl.ANY)],
        out_specs=pl.BlockSpec((None, rows, HEAD_DIM), lambda b, pt: (b, 0, 0)),
        scratch_shapes=[pltpu.VMEM((KV_SLOTS, ppc, prow, HEAD_DIM), F32),
                        pltpu.VMEM((KV_SLOTS, ppc, prow, HEAD_DIM), F32),
                        pltpu.SemaphoreType.DMA((2, KV_SLOTS, ppc)),
                        pltpu.VMEM((rows, 1), F32),
                        pltpu.VMEM((rows, 1), F32),
                        pltpu.VMEM((rows, HEAD_DIM), F32)],
    )
    return pl.pallas_call(
        kernel,
        grid_spec=grid_spec,
        out_shape=jax.ShapeDtypeStruct((bsz, rows, HEAD_DIM), F32),
        compiler_params=_params("arbitrary"),
        name="sample_attn",
    )(page_table, q_s, bias, gmask, k_new, v_new, cache_k, cache_v)


def _rope_tables(pos, head_w):
    half = head_w // 2
    inv = ROPE_THETA ** (-jnp.arange(half, dtype=F32) / half)
    ang = pos.astype(F32)[:, None] * inv[None, :]
    cos, sin = jnp.cos(ang), jnp.sin(ang)
    reps = V7X_LANES // head_w
    cos_t = jnp.tile(jnp.concatenate([cos, cos], axis=1), (1, reps))
    sin_t = jnp.tile(jnp.concatenate([-sin, sin], axis=1), (1, reps))
    return cos_t, sin_t


class _InWeights:
    def __init__(self, w_in, d):
        dc = d // 2
        self.full = w_in.astype(BF16)
        self.conv = 0
        self.q = 3 * dc
        self.k = self.q + D_ATTN
        self.v = self.k + D_KV
        self.qi = self.v + D_KV
        tail = self.qi + D_QI
        n_kw = IDX_DIM + N_IDX_HEADS
        self.kiwi = jnp.pad(self.full[:, tail:tail + n_kw], ((0, 0), (0, V7X_LANES - n_kw)))
        self.gates = self.full[:, tail + n_kw:tail + n_kw + 2 * d]

    def group(self, col0, m):
        if col0 % _proj_tn(m) == 0:
            return self.full, col0
        return self.full[:, col0:col0 + m], 0


def _in_proj(h, w, pos, d):
    cos128, sin128 = _rope_tables(pos, HEAD_DIM)
    cos64, sin64 = _rope_tables(pos, IDX_DIM)
    lane = jnp.arange(V7X_LANES)
    cos_kw = jnp.where(lane < IDX_DIM, cos64, 1.0)
    sin_kw = jnp.where(lane < IDX_DIM, sin64, 0.0)
    dc3 = 3 * (d // 2)
    zc = _mm(h, *w.group(w.conv, dc3), dc3, BF16, "in_conv")
    gates = _mm(h, w.gates, 0, 2 * d, BF16, "in_gates")
    q_hm = _mm_rope_q(h, *w.group(w.q, D_ATTN), D_ATTN, cos128, sin128, HEAD_DIM, "in_q")
    qi_hm = _mm_rope_q(h, *w.group(w.qi, D_QI), D_QI, cos64, sin64, IDX_DIM, "in_qi")
    k32, k16 = _mm_rope_kv(h, *w.group(w.k, D_KV), D_KV, cos128, sin128, HEAD_DIM, True, "in_k")
    v32, v16 = _mm_rope_kv(h, *w.group(w.v, D_KV), D_KV, cos128, sin128, HEAD_DIM, False, "in_v")
    kw32, kw16 = _mm_rope_kv(h, w.kiwi, 0, V7X_LANES, cos_kw, sin_kw, IDX_DIM, True, "in_kiwi")
    ki32 = kw32[:, :IDX_DIM]
    ki16 = kw16[:, :IDX_DIM]
    wi = kw32[:, IDX_DIM:IDX_DIM + N_IDX_HEADS] * (N_IDX_HEADS ** -0.5) * (IDX_DIM ** -0.5)
    return zc, gates, q_hm, qi_hm, k32, k16, v32, v16, ki32, ki16, wi


def kernel(x_prompt, x_sample, cache_k, cache_v, cache_kidx, state_conv, page_table, c_prompt, c_sample,
           w_ada, b_ada, g_mix, w_in, conv_w, w_conv_out, w_attn_out, w_out, g_ffn, w_up, w_down, g_final):
    depth = w_ada.shape[0]
    assert depth == 1, "single-layer trunk"
    bp, sp, d = x_prompt.shape
    bs, ts, _ = x_sample.shape
    dc = d // 2
    n_pool, page = cache_k.shape[1], cache_k.shape[2]
    n_pages = page_table.shape[1]
    past = n_pages * page
    assert ts <= QPAD and ts <= page

    c_all = jnp.concatenate([c_prompt, c_sample], axis=0)
    n_c = c_all.shape[0]
    c_all = jnp.pad(c_all, ((0, (-n_c) % 8), (0, 0)))
    mods = _ada(c_all, w_ada[0], b_ada[0])
    mods_p = mods[:bp].reshape(bp, 1, N_MOD * d)
    mods_s = jnp.repeat(mods[bp:bp + bs], ts, axis=0).reshape(1, bs * ts, N_MOD * d)
    rows_p = _Rows(bp * sp, sp, per_token=False)
    rows_s = _Rows(bs * ts, bs * ts, per_token=True)

    w = _InWeights(w_in[0], d)
    wc, wa, wo = w_conv_out[0].astype(BF16), w_attn_out[0].astype(BF16), w_out[0].astype(BF16)
    wu16 = w_up[0].astype(BF16)
    wd16 = w_down[0].astype(BF16)

    xp = x_prompt.reshape(bp * sp, d)
    hp = _norm_mod_call(xp, g_mix[0], mods_p, rows_p, 1, 0)
    pos_p = jnp.tile(jnp.arange(sp), bp)
    zc, gates, q_hm, qi_hm, k32, k16, v32, v16, ki32, ki16, wi = _in_proj(hp, w, pos_p, d)
    u_p, conv_p = _conv_prompt(zc, conv_w[0], bp, sp)
    k_top_p = min(TOPK_MAX, sp // 4)
    kc_p = _tile(sp, 256)
    vt16 = v16.reshape(bp, sp // kc_p, kc_p, D_KV).transpose(0, 1, 3, 2)
    attn_p = _attn_prompt(qi_hm, wi.T, ki16, q_hm, k16, vt16, bp, sp, k_top_p)
    merged_p = _merge(u_p, attn_p, wc, wa, gates)
    x1_p = _out_proj(merged_p, wo, xp, mods_p, rows_p)
    y_p = _ffn(x1_p, g_ffn[0], g_final, mods_p, rows_p, wu16, wd16)

    ns = bs * ts
    xs = x_sample.reshape(ns, d)
    hs = _norm_mod_call(xs, g_mix[0], mods_s, rows_s, 1, 0)
    pos_s = jnp.tile(past + jnp.arange(ts), bs)
    zc_s, gates_s, q_hm_s, qi_hm_s, ks32, ks16, vs32, vs16, kis32, kis16, wi_s = _in_proj(hs, w, pos_s, d)

    zc_t = zc_s.reshape(bs, ts, 3 * dc).transpose(1, 0, 2)
    st_t = state_conv[0].transpose(1, 0, 2)
    u_t, nst_t = _conv_sample(zc_t, st_t, conv_w[0])
    u_s = u_t.transpose(1, 0, 2).reshape(ns, dc)
    conv_s = nst_t.transpose(1, 0, 2)

    k_top_s = min(TOPK_MAX, (past + ts) // 4)
    pad_q = ((0, 0), (0, 0), (0, QPAD - ts), (0, 0))
    qi_s = jnp.pad(qi_hm_s.reshape(N_IDX_HEADS, bs, ts, IDX_DIM).transpose(1, 0, 2, 3), pad_q)
    qi_s = qi_s.reshape(bs, N_IDX_HEADS * QPAD, IDX_DIM)
    wi_sp = jnp.pad(wi_s.reshape(bs, ts, N_IDX_HEADS).transpose(0, 2, 1), ((0, 0), (0, 0), (0, QPAD - ts)))
    wi_sp = wi_sp.reshape(bs, N_IDX_HEADS * QPAD, 1)
    ki_new_t = jnp.pad(kis16.reshape(bs, ts, IDX_DIM).transpose(0, 2, 1), ((0, 0), (0, 0), (0, page - ts)))
    expand = jnp.repeat(jnp.eye(page, dtype=BF16), N_KV_HEADS, axis=1)
    kidx_t = jnp.swapaxes(cache_kidx.reshape(n_pool, page, IDX_DIM), 1, 2)
    bias_s = _sample_index(page_table, qi_s, wi_sp, ki_new_t, expand, kidx_t, ts, k_top_s)

    q_s = jnp.pad(q_hm_s.reshape(N_HEADS, bs, ts, HEAD_DIM).transpose(1, 0, 2, 3), pad_q)
    q_s = q_s.reshape(bs, N_HEADS * QPAD, HEAD_DIM)
    prow = page * N_KV_HEADS
    col_head = jnp.arange(prow) % N_KV_HEADS
    row_head = jnp.arange(N_HEADS * QPAD) // (KV_GROUP * QPAD)
    gmask = jnp.where(row_head[:, None] == col_head[None, :], 0.0, NEG_INF).astype(F32)
    pad_new = ((0, 0), (0, page - ts), (0, 0), (0, 0))
    k_new = jnp.pad(ks16.reshape(bs, ts, N_KV_HEADS, HEAD_DIM), pad_new).reshape(bs, prow, HEAD_DIM)
    v_new = jnp.pad(vs16.reshape(bs, ts, N_KV_HEADS, HEAD_DIM), pad_new).reshape(bs, prow, HEAD_DIM)
    ck = cache_k.reshape(n_pool, prow, HEAD_DIM)
    cv = cache_v.reshape(n_pool, prow, HEAD_DIM)
    attn_hq = _sample_attn(page_table, q_s, bias_s, gmask, k_new, v_new, ck, cv)
    attn_s = attn_hq.reshape(bs, N_HEADS, QPAD, HEAD_DIM)[:, :, :ts].transpose(0, 2, 1, 3)
    attn_s = attn_s.reshape(ns, D_ATTN).astype(BF16)

    merged_s = _merge(u_s, attn_s, wc, wa, gates_s)
    x1_s = _out_proj(merged_s, wo, xs, mods_s, rows_s)
    y_s = _ffn(x1_s, g_ffn[0], g_final, mods_s, rows_s, wu16, wd16)

    return (y_p.reshape(bp, sp, d),
            y_s.reshape(bs, ts, d),
            k32.reshape(1, bp, sp, N_KV_HEADS, HEAD_DIM),
            v32.reshape(1, bp, sp, N_KV_HEADS, HEAD_DIM),
            ki32.reshape(1, bp, sp, IDX_DIM),
            conv_p.reshape(1, bp, CONV_W - 1, dc),
            ks32.reshape(1, bs, ts, N_KV_HEADS, HEAD_DIM),
            vs32.reshape(1, bs, ts, N_KV_HEADS, HEAD_DIM),
            kis32.reshape(1, bs, ts, IDX_DIM),
            conv_s.reshape(1, bs, CONV_W - 1, dc))
```

```python
import functools

import jax
import jax.numpy as jnp
from jax import lax
from jax.experimental import pallas as pl
from jax.experimental.pallas import tpu as pltpu

N_HEADS = 16
HEAD_DIM = 128
N_KV_HEADS = 4
KV_GROUP = N_HEADS // N_KV_HEADS
N_IDX_HEADS = 16
IDX_DIM = 64
TOPK_MAX = 256
CONV_W = 3
ROPE_THETA = 10000.0
NORM_EPS = 1e-6
N_MOD = 6
D_ATTN = N_HEADS * HEAD_DIM
D_KV = N_KV_HEADS * HEAD_DIM
D_QI = N_IDX_HEADS * IDX_DIM

V7X_LANES = 128
V7X_VMEM_LIMIT_BYTES = 56 * 1024 * 1024

BF16 = jnp.bfloat16
F32 = jnp.float32
NEG_INF = float("-inf")
LOG2_E = 1.4426950408889634

BISECT_STEPS = 17


def _tile(n, target, mult=8):
    if n <= target:
        return n
    t = (target // mult) * mult
    while t >= mult:
        if n % t == 0:
            return t
        t -= mult
    return n


def _params(*sem):
    return pltpu.CompilerParams(dimension_semantics=sem, vmem_limit_bytes=V7X_VMEM_LIMIT_BYTES)


def _dot(a, b):
    return jnp.dot(a, b, preferred_element_type=F32)


def _dot_nt(a, b):
    return lax.dot_general(a, b, (((1,), (1,)), ((), ())), preferred_element_type=F32)


def _ada_kernel(c_ref, w_ref, b_ref, o_ref):
    c = c_ref[...]
    a = (c * jax.nn.sigmoid(c)).astype(BF16)
    o_ref[...] = _dot(a, w_ref[...].astype(BF16)) + b_ref[...]


def _ada(c, w, b):
    m, d = c.shape
    n = w.shape[1]
    tn = _tile(n, 512, V7X_LANES)
    return pl.pallas_call(
        _ada_kernel,
        grid=(n // tn,),
        in_specs=[pl.BlockSpec((m, d), lambda j: (0, 0)),
                  pl.BlockSpec((d, tn), lambda j: (0, j)),
                  pl.BlockSpec((1, tn), lambda j: (0, j))],
        out_specs=pl.BlockSpec((m, tn), lambda j: (0, j)),
        out_shape=jax.ShapeDtypeStruct((m, n), F32),
        compiler_params=_params("arbitrary"),
        name="ada",
    )(c, w, b.reshape(1, n))


class _Rows:
    def __init__(self, n_rows, rows_per_group, per_token):
        self.n = n_rows
        self.rows_per_group = rows_per_group
        self.per_token = per_token

    def mod_spec(self, tm, d, which, ncol_blocks_per_mod=1, tiled=False):
        tn = d // ncol_blocks_per_mod
        rpg = self.rows_per_group
        if self.per_token:
            shape = (None, tm, tn)
            where = lambda i: (0, i)
        else:
            shape = (None, 1, tn)
            where = lambda i: ((i * tm) // rpg, 0)
        if not tiled:
            return pl.BlockSpec(shape, lambda i, *_: (*where(i), which * ncol_blocks_per_mod))
        return pl.BlockSpec(shape, lambda i, j: (*where(i), which * ncol_blocks_per_mod + j))


def _norm_mod(x, g, sc, sh):
    ms = jnp.mean(x * x, axis=-1, keepdims=True)
    y = x * lax.rsqrt(ms + NORM_EPS) * g
    return y * (1.0 + sc) + sh


def _norm_mod_kernel(x_ref, g_ref, sc_ref, sh_ref, o_ref):
    o_ref[...] = _norm_mod(x_ref[...], g_ref[...], sc_ref[...], sh_ref[...]).astype(o_ref.dtype)


def _norm_mod_call(x, g, mods3, rows, which_sc, which_sh):
    n, d = x.shape
    tm = _tile(min(n, rows.rows_per_group if not rows.per_token else n), 256)
    return pl.pallas_call(
        _norm_mod_kernel,
        grid=(n // tm,),
        in_specs=[pl.BlockSpec((tm, d), lambda i: (i, 0)),
                  pl.BlockSpec((1, d), lambda i: (0, 0)),
                  rows.mod_spec(tm, d, which_sc),
                  rows.mod_spec(tm, d, which_sh)],
        out_specs=pl.BlockSpec((tm, d), lambda i: (i, 0)),
        out_shape=jax.ShapeDtypeStruct((n, d), BF16),
        compiler_params=_params("parallel"),
        name="norm_mod",
    )(x, g.reshape(1, d), mods3, mods3)


def _proj_kernel(x_ref, w_ref, *rest, epilogue, n_extra):
    epilogue(_dot(x_ref[...], w_ref[...]), rest[:n_extra], rest[n_extra:])


def _proj(x, w, col0, m, *, tm, tn, epilogue, extra, extra_specs, out_shape, out_specs, name):
    n, k = x.shape
    assert n % tm == 0 and m % tn == 0 and col0 % tn == 0 and w.dtype == BF16
    jb = col0 // tn
    return pl.pallas_call(
        functools.partial(_proj_kernel, epilogue=epilogue, n_extra=len(extra)),
        grid=(n // tm, m // tn),
        in_specs=[pl.BlockSpec((tm, k), lambda i, j: (i, 0)),
                  pl.BlockSpec((k, tn), lambda i, j: (0, jb + j))] + list(extra_specs),
        out_specs=out_specs,
        out_shape=out_shape,
        compiler_params=_params("parallel", "arbitrary"),
        name=name,
    )(x, w, *extra)


PROJ_TM = 1024
PROJ_TN = 1024


def _proj_tn(m):
    return _tile(m, PROJ_TN, V7X_LANES)


def _cast_epilogue(acc, extra, outs):
    outs[0][...] = acc.astype(outs[0].dtype)


def _mm(x, w, col0, m, out_dtype, name):
    n = x.shape[0]
    tm, tn = _tile(n, PROJ_TM), _proj_tn(m)
    return _proj(x, w, col0, m, tm=tm, tn=tn, epilogue=_cast_epilogue, extra=(), extra_specs=(),
                 out_shape=jax.ShapeDtypeStruct((n, m), out_dtype),
                 out_specs=pl.BlockSpec((tm, tn), lambda i, j: (i, j)), name=name)


def _swap_halves(x, head_w):
    if head_w == V7X_LANES:
        return pltpu.roll(x, V7X_LANES // 2, axis=1)
    half = head_w // 2
    lane = lax.broadcasted_iota(jnp.int32, x.shape, 1)
    take_upper = (lane & (head_w - 1)) < half
    return jnp.where(take_upper, pltpu.roll(x, V7X_LANES - half, axis=1), pltpu.roll(x, half, axis=1))


def _rope_cols(acc, cos, sin, head_w):
    out = []
    for c in range(acc.shape[1] // V7X_LANES):
        x = acc[:, c * V7X_LANES:(c + 1) * V7X_LANES]
        out.append(x * cos + _swap_halves(x, head_w) * sin)
    return out


def _rope_q_epilogue(acc, extra, outs, *, head_w):
    cos_ref, sin_ref = extra
    o_ref = outs[0]
    cols = _rope_cols(acc, cos_ref[...], sin_ref[...], head_w)
    per = V7X_LANES // head_w
    for c, x in enumerate(cols):
        for p in range(per):
            o_ref[c * per + p] = x[:, p * head_w:(p + 1) * head_w].astype(o_ref.dtype)


def _mm_rope_q(x, w, col0, m, cos, sin, head_w, name):
    n = x.shape[0]
    tm, tn = _tile(min(n, cos.shape[0]), PROJ_TM), _proj_tn(m)
    hpt = tn // head_w
    table = _table_spec(cos, tm)
    return _proj(x, w, col0, m, tm=tm, tn=tn, epilogue=functools.partial(_rope_q_epilogue, head_w=head_w),
                 extra=(cos, sin), extra_specs=(table, table),
                 out_shape=jax.ShapeDtypeStruct((m // head_w, n, head_w), BF16),
                 out_specs=pl.BlockSpec((hpt, tm, head_w), lambda i, j: (j, i, 0)), name=name)


def _table_spec(table, tm):
    period = table.shape[0] // tm
    assert period * tm == table.shape[0]
    return pl.BlockSpec((tm, V7X_LANES), lambda i, j: (i % period, 0))


KVX_TM = 512


def _kvx_epilogue(acc, extra, outs):
    cos_ref, sin_ref, cos_x_ref, sin_x_ref = extra
    k32_ref, k16_ref, v32_ref, v16_ref, x32_ref, x16_ref = outs
    tm = acc.shape[0]
    for c, x in enumerate(_rope_cols(acc[:, :D_KV], cos_ref[...], sin_ref[...], HEAD_DIM)):
        k32_ref[pl.ds(c, tm, stride=N_KV_HEADS), :] = x
        k16_ref[:, c * V7X_LANES:(c + 1) * V7X_LANES] = x.astype(BF16)
    v = acc[:, D_KV:2 * D_KV]
    for c in range(N_KV_HEADS):
        v32_ref[pl.ds(c, tm, stride=N_KV_HEADS), :] = v[:, c * HEAD_DIM:(c + 1) * HEAD_DIM]
    v16_ref[...] = v.astype(BF16)
    (x,) = _rope_cols(acc[:, 2 * D_KV:], cos_x_ref[...], sin_x_ref[...], IDX_DIM)
    x32_ref[...] = x
    x16_ref[...] = x.astype(BF16)


def _mm_kvx(x, w_kvx, cos, sin, cos_x, sin_x):
    n = x.shape[0]
    m = w_kvx.shape[1]
    assert m == 2 * D_KV + V7X_LANES
    tm = _tile(min(n, cos.shape[0]), KVX_TM)
    table = _table_spec(cos, tm)
    kv32 = (N_KV_HEADS, HEAD_DIM)
    layouts = [kv32, (1, D_KV), kv32, (1, D_KV), (1, V7X_LANES), (1, V7X_LANES)]
    dtypes = [F32, BF16] * 3
    shapes = [jax.ShapeDtypeStruct((n * r, wd), dt) for (r, wd), dt in zip(layouts, dtypes)]
    specs = [pl.BlockSpec((tm * r, wd), lambda i, j: (i, 0)) for r, wd in layouts]
    return _proj(x, w_kvx, 0, m, tm=tm, tn=m, epilogue=_kvx_epilogue,
                 extra=(cos, sin, cos_x, sin_x), extra_specs=(table,) * 4,
                 out_shape=shapes, out_specs=specs, name="in_kvx")


PREV_ROWS = 16


def _conv_prompt_kernel(xin_ref, bg_ref, cg_ref, xin_prev_ref, cg_prev_ref, w_ref, u_ref, st_ref,
                        *, blocks_per_seq):
    i = pl.program_id(1)
    p = cg_ref[...].astype(F32) * xin_ref[...].astype(F32)
    prev = cg_prev_ref[...].astype(F32) * xin_prev_ref[...].astype(F32)
    prev = jnp.where(i % blocks_per_seq == 0, 0.0, prev)
    tm = p.shape[0]
    last = PREV_ROWS - 1
    row = lax.broadcasted_iota(jnp.int32, p.shape, 0)
    p1 = pltpu.roll(p, 1, axis=0)
    p1 = jnp.where(row == 0, prev[last:last + 1], p1)
    p2 = pltpu.roll(p, 2, axis=0)
    p2 = jnp.where(row == 0, prev[last - 1:last], jnp.where(row == 1, prev[last:last + 1], p2))
    w = w_ref[...]
    y = p2 * w[0:1] + p1 * w[1:2] + p * w[2:3]
    u_ref[...] = (bg_ref[...].astype(F32) * y).astype(u_ref.dtype)
    st_ref[...] = p[tm - (CONV_W - 1):, :]


def _conv_prompt(zc, conv_w, n_seq, seq):
    n = zc.shape[0]
    dc = zc.shape[1] // 3
    tm = _tile(seq, 512)
    tc = _tile(dc, 512, V7X_LANES)
    ncb = dc // tc
    bps = seq // tm
    prev_row_blk = lambda i: jnp.maximum(i * (tm // PREV_ROWS) - 1, 0)
    u, st = pl.pallas_call(
        functools.partial(_conv_prompt_kernel, blocks_per_seq=bps),
        grid=(ncb, n // tm),
        in_specs=[pl.BlockSpec((tm, tc), lambda j, i: (i, j)),
                  pl.BlockSpec((tm, tc), lambda j, i: (i, ncb + j)),
                  pl.BlockSpec((tm, tc), lambda j, i: (i, 2 * ncb + j)),
                  pl.BlockSpec((PREV_ROWS, tc), lambda j, i: (prev_row_blk(i), j)),
                  pl.BlockSpec((PREV_ROWS, tc), lambda j, i: (prev_row_blk(i), 2 * ncb + j)),
                  pl.BlockSpec((CONV_W, tc), lambda j, i: (0, j))],
        out_specs=[pl.BlockSpec((tm, tc), lambda j, i: (i, j)),
                   pl.BlockSpec((None, CONV_W - 1, tc), lambda j, i: (i // bps, 0, j))],
        out_shape=[jax.ShapeDtypeStruct((n, dc), BF16),
                   jax.ShapeDtypeStruct((n_seq, CONV_W - 1, dc), F32)],
        compiler_params=_params("parallel", "arbitrary"),
        name="conv_prompt",
    )(zc, zc, zc, zc, zc, conv_w)
    return u, st


def _conv_sample_kernel(xin_ref, bg_ref, cg_ref, st_ref, w_ref, u_ref, nst_ref):
    t_len = xin_ref.shape[0]
    w = w_ref[...]
    ext = [st_ref[j] for j in range(CONV_W - 1)]
    ext += [cg_ref[t].astype(F32) * xin_ref[t].astype(F32) for t in range(t_len)]
    for t in range(t_len):
        y = ext[t] * w[0:1]
        for j in range(1, CONV_W):
            y = y + ext[t + j] * w[j:j + 1]
        u_ref[t] = (bg_ref[t].astype(F32) * y).astype(u_ref.dtype)
    for j in range(CONV_W - 1):
        nst_ref[j] = ext[t_len + j]


def _conv_sample(zc_t, st_t, conv_w):
    t_len, b, dc3 = zc_t.shape
    dc = dc3 // 3
    tc = _tile(dc, 512, V7X_LANES)
    ncb = dc // tc
    return pl.pallas_call(
        _conv_sample_kernel,
        grid=(ncb,),
        in_specs=[pl.BlockSpec((t_len, b, tc), lambda j: (0, 0, j)),
                  pl.BlockSpec((t_len, b, tc), lambda j: (0, 0, ncb + j)),
                  pl.BlockSpec((t_len, b, tc), lambda j: (0, 0, 2 * ncb + j)),
                  pl.BlockSpec((CONV_W - 1, b, tc), lambda j: (0, 0, j)),
                  pl.BlockSpec((CONV_W, tc), lambda j: (0, j))],
        out_specs=[pl.BlockSpec((t_len, b, tc), lambda j: (0, 0, j)),
                   pl.BlockSpec((CONV_W - 1, b, tc), lambda j: (0, 0, j))],
        out_shape=[jax.ShapeDtypeStruct((t_len, b, dc), BF16),
                   jax.ShapeDtypeStruct((CONV_W - 1, b, dc), F32)],
        compiler_params=_params("parallel"),
        name="conv_sample",
    )(zc_t, zc_t, zc_t, st_t, conv_w)


_SUBLANES = 8
_REDUCERS = {jnp.add: jnp.sum, jnp.maximum: jnp.max, jnp.minimum: jnp.min}


def _fold_chunks(n_chunks, term, combine, init, shape, axis):
    if axis == 0 and shape[0] > _SUBLANES:
        full_term = term
        shape = (_SUBLANES, shape[1])
        term = lambda c: _REDUCERS[combine](full_term(c).reshape(-1, *shape), axis=0)
    if isinstance(n_chunks, int):
        vals = [term(c) for c in range(n_chunks)]
        while len(vals) > 1:
            vals = [combine(vals[i], vals[i + 1]) if i + 1 < len(vals) else vals[i]
                    for i in range(0, len(vals), 2)]
        return vals[0]
    return lax.fori_loop(0, n_chunks, lambda c, acc: combine(acc, term(c)), jnp.full(shape, init, F32))


def _topk_threshold(score_at, n_chunks, want, shape, axis):
    fold = functools.partial(_fold_chunks, n_chunks, shape=shape, axis=axis)

    def count_ge(thr):
        part = fold(lambda c: jnp.where(score_at(c) >= thr, 1.0, 0.0), jnp.add, 0.0)
        return jnp.sum(part, axis=axis, keepdims=True)

    hi = jnp.max(fold(score_at, jnp.maximum, NEG_INF), axis=axis, keepdims=True)
    lo = jnp.min(fold(lambda c: jnp.where(score_at(c) == NEG_INF, jnp.inf, score_at(c)), jnp.minimum, jnp.inf),
                 axis=axis, keepdims=True)
    cnt_lo = count_ge(lo)
    cnt_hi = count_ge(hi)
    top_ok = cnt_hi >= want
    lo = jnp.where(top_ok, hi, lo)
    cnt_lo = jnp.where(top_ok, cnt_hi, cnt_lo)

    def bisect(_, carry):
        lo, hi, cnt_lo = carry
        mid = lo + (hi - lo) * 0.5
        cnt = count_ge(mid)
        ok = cnt >= want
        return (jnp.where(ok, mid, lo), jnp.where(ok, hi, mid), jnp.where(ok, cnt, cnt_lo))

    lo, hi, cnt_lo = lax.fori_loop(0, BISECT_STEPS, bisect, (lo, hi, cnt_lo))

    def pending_of(cnt_lo, done):
        return jnp.max(jnp.where((cnt_lo > want) & (done == 0.0), 1.0, 0.0))

    def finish(carry):
        lo, hi, cnt_lo, done, _ = carry
        cand = jnp.max(fold(lambda c: jnp.where(score_at(c) < hi, score_at(c), NEG_INF), jnp.maximum, NEG_INF),
                       axis=axis, keepdims=True)
        cnt = count_ge(cand)
        active = (cnt_lo > want) & (done == 0.0)
        ok = active & (cnt >= want)
        lo = jnp.where(ok, cand, lo)
        cnt_lo = jnp.where(ok, cnt, cnt_lo)
        done = jnp.where(ok, 1.0, done)
        hi = jnp.where(active & (cnt < want), cand, hi)
        return lo, hi, cnt_lo, done, pending_of(cnt_lo, done)

    done0 = jnp.where(lo == hi, 1.0, 0.0)
    lo, *_ = lax.while_loop(lambda carry: carry[4] > 0.0, finish,
                            (lo, hi, cnt_lo, done0, pending_of(cnt_lo, done0)))
    return lo


ATTN_TQ = 256
ATTN_KC = 256


def _attn_prompt_kernel(qi_ref, wi_ref, ki_ref, q_ref, k_ref, vt_ref, o_ref,
                        score_ref, m_ref, l_ref, acc_ref, *, tq, kc, k_top):
    qb = pl.program_id(1)
    n_chunks = (qb * tq) // kc + tq // kc
    qpos = qb * tq + lax.broadcasted_iota(jnp.int32, (1, tq), 1)
    wi = wi_ref[...]

    def fill(c, _):
        kic = ki_ref[pl.ds(pl.multiple_of(c * kc, kc), kc), :]
        acc = jnp.zeros((kc, tq), F32)
        for h in range(N_IDX_HEADS):
            acc = acc + jnp.maximum(_dot_nt(kic, qi_ref[h]), 0.0) * wi[h:h + 1]
        kpos = c * kc + lax.broadcasted_iota(jnp.int32, (kc, tq), 0)
        score_ref[c] = jnp.where(kpos <= qpos, acc, NEG_INF)
        return 0

    lax.fori_loop(0, n_chunks, fill, 0)

    want = jnp.minimum(qpos + 1, k_top).astype(F32)
    thr = _topk_threshold(lambda c: score_ref[c], n_chunks, want, (kc, tq), axis=0)

    scale = HEAD_DIM ** -0.5 * LOG2_E
    for g in range(N_KV_HEADS):
        m_ref[...] = jnp.full(m_ref.shape, NEG_INF, F32)
        l_ref[...] = jnp.zeros(l_ref.shape, F32)
        acc_ref[...] = jnp.zeros(acc_ref.shape, F32)

        heads = range(KV_GROUP)

        def scores(c):
            kch = k_ref[pl.ds(pl.multiple_of(c * kc, kc), kc), g * HEAD_DIM:(g + 1) * HEAD_DIM]
            bias = jnp.where(score_ref[c] >= thr, 0.0, NEG_INF)
            return [_dot_nt(kch, q_ref[g * KV_GROUP + r]) * scale + bias for r in heads]

        def update(c, s):
            vt = vt_ref[c, g * HEAD_DIM:(g + 1) * HEAD_DIM, :]
            m_old = [m_ref[r] for r in heads]
            l_old = [l_ref[r] for r in heads]
            m_new = [jnp.maximum(m_old[r], jnp.max(s[r], axis=0, keepdims=True)) for r in heads]
            m_safe = [jnp.where(m_new[r] == NEG_INF, 0.0, m_new[r]) for r in heads]
            p = [jnp.exp2(s[r] - m_safe[r]) for r in heads]
            alpha = [jnp.exp2(m_old[r] - m_safe[r]) for r in heads]
            pv = [_dot(vt, p[r].astype(BF16)) for r in heads]
            for r in heads:
                l_ref[r] = alpha[r] * l_old[r] + jnp.sum(p[r], axis=0, keepdims=True)
                acc_ref[r] = alpha[r] * acc_ref[r] + pv[r]
                m_ref[r] = m_new[r]

        def attend_pair(i, _):
            sa, sb = scores(2 * i), scores(2 * i + 1)
            update(2 * i, sa)
            update(2 * i + 1, sb)
            return 0

        lax.fori_loop(0, n_chunks // 2, attend_pair, 0)

        @pl.when(n_chunks % 2 == 1)
        def _():
            update(n_chunks - 1, scores(n_chunks - 1))
        for r in range(KV_GROUP):
            h = g * KV_GROUP + r
            out = acc_ref[r] / l_ref[r]
            o_ref[:, h * HEAD_DIM:(h + 1) * HEAD_DIM] = out.T.astype(o_ref.dtype)


def _attn_prompt(qi_hm, wi_t, ki16, q_hm, k16, vt16, n_seq, seq, k_top):
    n = n_seq * seq
    tq = _tile(seq, ATTN_TQ)
    kc = _tile(tq, ATTN_KC)
    nqb = seq // tq
    assert vt16.shape == (n_seq, seq // kc, D_KV, kc)
    kernel = functools.partial(_attn_prompt_kernel, tq=tq, kc=kc, k_top=k_top)
    return pl.pallas_call(
        kernel,
        grid=(n_seq, nqb),
        in_specs=[pl.BlockSpec((N_IDX_HEADS, tq, IDX_DIM), lambda b, q: (0, b * nqb + q, 0)),
                  pl.BlockSpec((N_IDX_HEADS, tq), lambda b, q: (0, b * nqb + q)),
                  pl.BlockSpec((seq, IDX_DIM), lambda b, q: (b, 0)),
                  pl.BlockSpec((N_HEADS, tq, HEAD_DIM), lambda b, q: (0, b * nqb + q, 0)),
                  pl.BlockSpec((seq, D_KV), lambda b, q: (b, 0)),
                  pl.BlockSpec((None, seq // kc, D_KV, kc), lambda b, q: (b, 0, 0, 0))],
        out_specs=pl.BlockSpec((tq, D_ATTN), lambda b, q: (b * nqb + q, 0)),
        out_shape=jax.ShapeDtypeStruct((n, D_ATTN), BF16),
        scratch_shapes=[pltpu.VMEM((seq // kc, kc, tq), F32),
                        pltpu.VMEM((KV_GROUP, 1, tq), F32),
                        pltpu.VMEM((KV_GROUP, 1, tq), F32),
                        pltpu.VMEM((KV_GROUP, HEAD_DIM, tq), F32)],
        compiler_params=_params("parallel", "arbitrary"),
        name="attn_prompt",
    )(qi_hm, wi_t, ki16, q_hm, k16, vt16)


def _merge_kernel(u_ref, a_ref, wc_ref, wa_ref, gc_ref, ga_ref, o_ref):
    yc = _dot(u_ref[...], wc_ref[...])
    ya = _dot(a_ref[...], wa_ref[...])
    gc = jax.nn.sigmoid(gc_ref[...].astype(F32))
    ga = jax.nn.sigmoid(ga_ref[...].astype(F32))
    o_ref[...] = (gc * yc + ga * ya).astype(o_ref.dtype)


def _merge(u, attn, wc, wa, gates):
    n, dc = u.shape
    da = attn.shape[1]
    d = wc.shape[1]
    tm = _tile(n, PROJ_TM)
    tn = _proj_tn(d)
    ncb = d // tn
    return pl.pallas_call(
        _merge_kernel,
        grid=(n // tm, ncb),
        in_specs=[pl.BlockSpec((tm, dc), lambda i, j: (i, 0)),
                  pl.BlockSpec((tm, da), lambda i, j: (i, 0)),
                  pl.BlockSpec((dc, tn), lambda i, j: (0, j)),
                  pl.BlockSpec((da, tn), lambda i, j: (0, j)),
                  pl.BlockSpec((tm, tn), lambda i, j: (i, j)),
                  pl.BlockSpec((tm, tn), lambda i, j: (i, ncb + j))],
        out_specs=pl.BlockSpec((tm, tn), lambda i, j: (i, j)),
        out_shape=jax.ShapeDtypeStruct((n, d), BF16),
        compiler_params=_params("parallel", "arbitrary"),
        name="merge",
    )(u, attn, wc, wa, gates, gates)


def _out_proj_epilogue(acc, extra, outs):
    x_ref, gt_ref = extra
    outs[0][...] = x_ref[...] + gt_ref[...] * acc


def _out_proj(a, w, x, mods3, rows):
    n = a.shape[0]
    d = w.shape[1]
    tm = _tile(min(n, rows.rows_per_group if not rows.per_token else n), PROJ_TM)
    tn = _proj_tn(d)
    tile = pl.BlockSpec((tm, tn), lambda i, j: (i, j))
    gate = rows.mod_spec(tm, d, 2, d // tn, tiled=True)
    return _proj(a, w, 0, d, tm=tm, tn=tn, epilogue=_out_proj_epilogue, extra=(x, mods3), extra_specs=(tile, gate),
                 out_shape=jax.ShapeDtypeStruct((n, d), F32), out_specs=tile, name="out_proj")


def _ffn_kernel(x_ref, g_ref, sc_ref, sh_ref, gt_ref, gf_ref, wu_ref, wd_ref, o_ref, h_ref):
    f = pl.program_id(1)

    @pl.when(f == 0)
    def _():
        h_ref[...] = _norm_mod(x_ref[...], g_ref[...], sc_ref[...], sh_ref[...]).astype(BF16)
        o_ref[...] = jnp.zeros(o_ref.shape, F32)

    u = jnp.maximum(_dot(h_ref[...], wu_ref[...]), 0.0)
    o_ref[...] += _dot((u * u).astype(BF16), wd_ref[...])

    @pl.when(f == pl.num_programs(1) - 1)
    def _():
        x2 = x_ref[...] + gt_ref[...] * o_ref[...]
        ms = jnp.mean(x2 * x2, axis=-1, keepdims=True)
        o_ref[...] = x2 * lax.rsqrt(ms + NORM_EPS) * gf_ref[...]


def _ffn(x1, g_ffn, g_final, mods3, rows, w_up, w_down):
    n, d = x1.shape
    dff = w_up.shape[1]
    tm = _tile(min(n, rows.rows_per_group if not rows.per_token else n), 512)
    tf = _tile(dff, 512, V7X_LANES)
    return pl.pallas_call(
        _ffn_kernel,
        grid=(n // tm, dff // tf),
        in_specs=[pl.BlockSpec((tm, d), lambda i, f: (i, 0), pipeline_mode=pl.Buffered(1)),
                  pl.BlockSpec((1, d), lambda i, f: (0, 0)),
                  rows.mod_spec(tm, d, 4),
                  rows.mod_spec(tm, d, 3),
                  rows.mod_spec(tm, d, 5),
                  pl.BlockSpec((1, d), lambda i, f: (0, 0)),
                  pl.BlockSpec((d, tf), lambda i, f: (0, f)),
                  pl.BlockSpec((tf, d), lambda i, f: (f, 0))],
        out_specs=pl.BlockSpec((tm, d), lambda i, f: (i, 0)),
        out_shape=jax.ShapeDtypeStruct((n, d), F32),
        scratch_shapes=[pltpu.VMEM((tm, d), BF16)],
        compiler_params=_params("parallel", "arbitrary"),
        name="ffn",
    )(x1, g_ffn.reshape(1, d), mods3, mods3, mods3, g_final.reshape(1, d), w_up, w_down)


QPAD = 8
INDEX_PAGES_PER_DOT = 8


def _sample_index_kernel(pt_ref, qi_ref, wi_ref, kin_ref, expand_ref, cache_ref, bias_ref,
                         kbuf_ref, sem_ref, score_ref, *, n_pages, page, t_len, k_top):
    b = pl.program_id(0)

    slot = b % 2

    def page_copy(bb, p):
        return pltpu.make_async_copy(cache_ref.at[pt_ref[bb, p]], kbuf_ref.at[bb % 2, p], sem_ref.at[bb % 2, p])

    @pl.when(b == 0)
    def _():
        for p in range(n_pages):
            page_copy(0, p).start()

    @pl.when(b + 1 < pl.num_programs(0))
    def _():
        for p in range(n_pages):
            page_copy(b + 1, p).start()

    qi = qi_ref[...]
    wi = wi_ref[...].reshape(N_IDX_HEADS, QPAD, 1)

    def head_sum(s):
        s3 = jnp.maximum(s, 0.0).reshape(N_IDX_HEADS, QPAD, s.shape[1]) * wi
        return jnp.sum(s3, axis=0)

    ppd = INDEX_PAGES_PER_DOT if n_pages % INDEX_PAGES_PER_DOT == 0 else 1
    for p0 in range(0, n_pages, ppd):
        for p in range(p0, p0 + ppd):
            page_copy(b, p).wait()
        kt = jnp.concatenate([kbuf_ref[slot, p] for p in range(p0, p0 + ppd)], axis=1).astype(BF16)
        sc = head_sum(_dot(qi, kt))
        for j in range(ppd):
            score_ref[p0 + j] = sc[:, j * page:(j + 1) * page]

    qrow = lax.broadcasted_iota(jnp.int32, (QPAD, page), 0)
    lane = lax.broadcasted_iota(jnp.int32, (QPAD, page), 1)
    s_new = head_sum(_dot(qi, kin_ref[...]))
    score_ref[n_pages] = jnp.where((lane <= qrow) & (lane < t_len), s_new, NEG_INF)

    n_chunks = n_pages + 1
    thr = _topk_threshold(lambda c: score_ref[c], n_chunks, float(k_top), (QPAD, page), axis=1)

    sel = [jnp.where(score_ref[c] >= thr, 1.0, 0.0) for c in range(n_chunks)]
    sel += [jnp.zeros((QPAD, page), F32)] * (n_chunks % 2)
    hit = _dot(jnp.concatenate(sel, axis=0).astype(BF16), expand_ref[...])
    bias = jnp.where(hit > 0.5, 0.0, NEG_INF)
    for c in range(n_chunks):
        bias_ref[c] = bias[c * QPAD:(c + 1) * QPAD]


def _sample_index(page_table, qi_s, wi_s, ki_new_t, expand, cache_kidx_t, t_len, k_top):
    bsz, n_pages = page_table.shape
    page = cache_kidx_t.shape[2]
    kernel = functools.partial(_sample_index_kernel, n_pages=n_pages, page=page, t_len=t_len, k_top=k_top)
    grid_spec = pltpu.PrefetchScalarGridSpec(
        num_scalar_prefetch=1,
        grid=(bsz,),
        in_specs=[pl.BlockSpec((None, N_IDX_HEADS * QPAD, IDX_DIM), lambda b, pt: (b, 0, 0)),
                  pl.BlockSpec((None, N_IDX_HEADS * QPAD, 1), lambda b, pt: (b, 0, 0)),
                  pl.BlockSpec((None, IDX_DIM, page), lambda b, pt: (b, 0, 0)),
                  pl.BlockSpec(expand.shape, lambda b, pt: (0, 0)),
                  pl.BlockSpec(memory_space=pl.ANY)],
        out_specs=pl.BlockSpec((None, n_pages + 1, QPAD, N_KV_HEADS * page), lambda b, pt: (b, 0, 0, 0)),
        scratch_shapes=[pltpu.VMEM((2, n_pages, IDX_DIM, page), F32),
                        pltpu.SemaphoreType.DMA((2, n_pages)),
                        pltpu.VMEM((n_pages + 1, QPAD, page), F32)],
    )
    return pl.pallas_call(
        kernel,
        grid_spec=grid_spec,
        out_shape=jax.ShapeDtypeStruct((bsz, n_pages + 1, QPAD, N_KV_HEADS * page), F32),
        compiler_params=_params("arbitrary"),
        name="sample_index",
    )(page_table, qi_s, wi_s, ki_new_t, expand, cache_kidx_t)


PAGES_PER_CHUNK = 8
KV_SLOTS = 3


def _sample_attn_kernel(pt_ref, q_ref, bias_ref, gmask_ref, kn_ref, vn_ref, ck_ref, cv_ref, o_ref,
                        kbuf_ref, vbuf_ref, sem_ref, m_ref, l_ref, acc_ref, *, n_pages, ppc):
    b = pl.program_id(0)
    n_chunks = n_pages // ppc
    rows = N_HEADS * QPAD
    prow = kbuf_ref.shape[2]

    n_total = pl.num_programs(0) * n_chunks

    def copies(g):
        bb, c, slot = g // n_chunks, g % n_chunks, g % KV_SLOTS
        out = []
        for j in range(ppc):
            pid = pt_ref[bb, c * ppc + j]
            out.append(pltpu.make_async_copy(ck_ref.at[pid], kbuf_ref.at[slot, j], sem_ref.at[0, slot, j]))
            out.append(pltpu.make_async_copy(cv_ref.at[pid], vbuf_ref.at[slot, j], sem_ref.at[1, slot, j]))
        return out

    def start(g):
        @pl.when(g < n_total)
        def _():
            for cp in copies(g):
                cp.start()

    @pl.when(b == 0)
    def _():
        for g in range(KV_SLOTS - 1):
            start(g)

    q = q_ref[...]
    gmask = gmask_ref[...]
    m_ref[...] = jnp.full(m_ref.shape, NEG_INF, F32)
    l_ref[...] = jnp.zeros(l_ref.shape, F32)
    acc_ref[...] = jnp.zeros(acc_ref.shape, F32)
    scale = HEAD_DIM ** -0.5 * LOG2_E

    def attend(kch, vch, biases):
        s = _dot_nt(q, kch) * scale
        parts = []
        for j, bias in enumerate(biases):
            sj = s[:, j * prow:(j + 1) * prow].reshape(N_HEADS, QPAD, prow) + bias[None]
            parts.append(sj.reshape(rows, prow) + gmask)
        s = parts[0] if len(parts) == 1 else jnp.concatenate(parts, axis=1)
        m_old = m_ref[...]
        m_new = jnp.maximum(m_old, jnp.max(s, axis=1, keepdims=True))
        m_safe = jnp.where(m_new == NEG_INF, 0.0, m_new)
        p = jnp.exp2(s - m_safe)
        alpha = jnp.exp2(m_old - m_safe)
        l_ref[...] = alpha * l_ref[...] + jnp.sum(p, axis=1, keepdims=True)
        acc_ref[...] = alpha * acc_ref[...] + _dot(p.astype(BF16), vch)
        m_ref[...] = m_new

    def chunk_body(c, _):
        g = b * n_chunks + c
        slot = g % KV_SLOTS
        start(g + KV_SLOTS - 1)
        for cp in copies(g):
            cp.wait()
        kch = kbuf_ref[slot].reshape(ppc * prow, HEAD_DIM).astype(BF16)
        vch = vbuf_ref[slot].reshape(ppc * prow, HEAD_DIM).astype(BF16)
        attend(kch, vch, [bias_ref[c * ppc + j] for j in range(ppc)])
        return 0

    lax.fori_loop(0, n_chunks, chunk_body, 0)
    attend(kn_ref[...], vn_ref[...], [bias_ref[n_pages]])
    o_ref[...] = acc_ref[...] / l_ref[...]


def _sample_attn(page_table, q_s, bias, gmask, k_new, v_new, cache_k, cache_v):
    bsz, n_pages = page_table.shape
    prow = cache_k.shape[1]
    ppc = PAGES_PER_CHUNK if n_pages % PAGES_PER_CHUNK == 0 else 1
    rows = N_HEADS * QPAD
    kernel = functools.partial(_sample_attn_kernel, n_pages=n_pages, ppc=ppc)
    grid_spec = pltpu.PrefetchScalarGridSpec(
        num_scalar_prefetch=1,
        grid=(bsz,),
        in_specs=[pl.BlockSpec((None, rows, HEAD_DIM), lambda b, pt: (b, 0, 0)),
                  pl.BlockSpec((None, n_pages + 1, QPAD, prow), lambda b, pt: (b, 0, 0, 0)),
                  pl.BlockSpec((rows, prow), lambda b, pt: (0, 0)),
                  pl.BlockSpec((None, prow, HEAD_DIM), lambda b, pt: (b, 0, 0)),
                  pl.BlockSpec((None, prow, HEAD_DIM), lambda b, pt: (b, 0, 0)),
                  pl.BlockSpec(memory_space=pl.ANY),
                  pl.BlockSpec(memory_space=pl.ANY)],
        out_specs=pl.BlockSpec((None, rows, HEAD_DIM), lambda b, pt: (b, 0, 0)),
        scratch_shapes=[pltpu.VMEM((KV_SLOTS, ppc, prow, HEAD_DIM), F32),
                        pltpu.VMEM((KV_SLOTS, ppc, prow, HEAD_DIM), F32),
                        pltpu.SemaphoreType.DMA((2, KV_SLOTS, ppc)),
                        pltpu.VMEM((rows, 1), F32),
                        pltpu.VMEM((rows, 1), F32),
                        pltpu.VMEM((rows, HEAD_DIM), F32)],
    )
    return pl.pallas_call(
        kernel,
        grid_spec=grid_spec,
        out_shape=jax.ShapeDtypeStruct((bsz, rows, HEAD_DIM), F32),
        compiler_params=_params("arbitrary"),
        name="sample_attn",
    )(page_table, q_s, bias, gmask, k_new, v_new, cache_k, cache_v)


def _rope_tables(pos, head_w):
    half = head_w // 2
    inv = ROPE_THETA ** (-jnp.arange(half, dtype=F32) / half)
    ang = pos.astype(F32)[:, None] * inv[None, :]
    cos, sin = jnp.cos(ang), jnp.sin(ang)
    reps = V7X_LANES // head_w
    cos_t = jnp.tile(jnp.concatenate([cos, cos], axis=1), (1, reps))
    sin_t = jnp.tile(jnp.concatenate([-sin, sin], axis=1), (1, reps))
    return cos_t, sin_t


class _InWeights:
    def __init__(self, w_in, d):
        dc = d // 2
        self.full = w_in.astype(BF16)
        self.conv = 0
        self.q = 3 * dc
        self.k = self.q + D_ATTN
        self.v = self.k + D_KV
        self.qi = self.v + D_KV
        tail = self.qi + D_QI
        n_kw = IDX_DIM + N_IDX_HEADS
        kiwi = jnp.pad(self.full[:, tail:tail + n_kw], ((0, 0), (0, V7X_LANES - n_kw)))
        self.kvx = jnp.concatenate([self.full[:, self.k:self.k + 2 * D_KV], kiwi], axis=1)
        self.gates = self.full[:, tail + n_kw:tail + n_kw + 2 * d]

    def group(self, col0, m):
        if col0 % _proj_tn(m) == 0:
            return self.full, col0
        return self.full[:, col0:col0 + m], 0


def _in_proj(h, w, pos, d):
    cos128, sin128 = _rope_tables(pos, HEAD_DIM)
    cos64, sin64 = _rope_tables(pos, IDX_DIM)
    lane = jnp.arange(V7X_LANES)
    cos_kw = jnp.where(lane < IDX_DIM, cos64, 1.0)
    sin_kw = jnp.where(lane < IDX_DIM, sin64, 0.0)
    dc3 = 3 * (d // 2)
    zc = _mm(h, *w.group(w.conv, dc3), dc3, BF16, "in_conv")
    gates = _mm(h, w.gates, 0, 2 * d, BF16, "in_gates")
    q_hm = _mm_rope_q(h, *w.group(w.q, D_ATTN), D_ATTN, cos128, sin128, HEAD_DIM, "in_q")
    qi_hm = _mm_rope_q(h, *w.group(w.qi, D_QI), D_QI, cos64, sin64, IDX_DIM, "in_qi")
    k32, k16, v32, v16, kw32, kw16 = _mm_kvx(h, w.kvx, cos128, sin128, cos_kw, sin_kw)
    ki32 = kw32[:, :IDX_DIM]
    ki16 = kw16[:, :IDX_DIM]
    wi = kw32[:, IDX_DIM:IDX_DIM + N_IDX_HEADS] * (N_IDX_HEADS ** -0.5) * (IDX_DIM ** -0.5)
    return zc, gates, q_hm, qi_hm, k32, k16, v32, v16, ki32, ki16, wi


def kernel(x_prompt, x_sample, cache_k, cache_v, cache_kidx, state_conv, page_table, c_prompt, c_sample,
           w_ada, b_ada, g_mix, w_in, conv_w, w_conv_out, w_attn_out, w_out, g_ffn, w_up, w_down, g_final):
    depth = w_ada.shape[0]
    assert depth == 1, "single-layer trunk"
    bp, sp, d = x_prompt.shape
    bs, ts, _ = x_sample.shape
    dc = d // 2
    n_pool, page = cache_k.shape[1], cache_k.shape[2]
    n_pages = page_table.shape[1]
    past = n_pages * page
    assert ts <= QPAD and ts <= page

    c_all = jnp.concatenate([c_prompt, c_sample], axis=0)
    n_c = c_all.shape[0]
    c_all = jnp.pad(c_all, ((0, (-n_c) % 8), (0, 0)))
    mods = _ada(c_all, w_ada[0], b_ada[0])
    mods_p = mods[:bp].reshape(bp, 1, N_MOD * d)
    mods_s = jnp.repeat(mods[bp:bp + bs], ts, axis=0).reshape(1, bs * ts, N_MOD * d)
    rows_p = _Rows(bp * sp, sp, per_token=False)
    rows_s = _Rows(bs * ts, bs * ts, per_token=True)

    w = _InWeights(w_in[0], d)
    wc, wa, wo = w_conv_out[0].astype(BF16), w_attn_out[0].astype(BF16), w_out[0].astype(BF16)
    wu16 = w_up[0].astype(BF16)
    wd16 = w_down[0].astype(BF16)

    xp = x_prompt.reshape(bp * sp, d)
    hp = _norm_mod_call(xp, g_mix[0], mods_p, rows_p, 1, 0)
    pos_p = jnp.arange(sp)
    zc, gates, q_hm, qi_hm, k32, k16, v32, v16, ki32, ki16, wi = _in_proj(hp, w, pos_p, d)
    u_p, conv_p = _conv_prompt(zc, conv_w[0], bp, sp)
    k_top_p = min(TOPK_MAX, sp // 4)
    kc_p = _tile(_tile(sp, ATTN_TQ), ATTN_KC)
    vt16 = v16.reshape(bp, sp // kc_p, kc_p, D_KV).transpose(0, 1, 3, 2)
    attn_p = _attn_prompt(qi_hm, wi.T, ki16, q_hm, k16, vt16, bp, sp, k_top_p)
    merged_p = _merge(u_p, attn_p, wc, wa, gates)
    x1_p = _out_proj(merged_p, wo, xp, mods_p, rows_p)
    y_p = _ffn(x1_p, g_ffn[0], g_final, mods_p, rows_p, wu16, wd16)

    ns = bs * ts
    xs = x_sample.reshape(ns, d)
    hs = _norm_mod_call(xs, g_mix[0], mods_s, rows_s, 1, 0)
    pos_s = jnp.tile(past + jnp.arange(ts), bs)
    zc_s, gates_s, q_hm_s, qi_hm_s, ks32, ks16, vs32, vs16, kis32, kis16, wi_s = _in_proj(hs, w, pos_s, d)

    zc_t = zc_s.reshape(bs, ts, 3 * dc).transpose(1, 0, 2)
    st_t = state_conv[0].transpose(1, 0, 2)
    u_t, nst_t = _conv_sample(zc_t, st_t, conv_w[0])
    u_s = u_t.transpose(1, 0, 2).reshape(ns, dc)
    conv_s = nst_t.transpose(1, 0, 2)

    k_top_s = min(TOPK_MAX, (past + ts) // 4)
    pad_q = ((0, 0), (0, 0), (0, QPAD - ts), (0, 0))
    qi_s = jnp.pad(qi_hm_s.reshape(N_IDX_HEADS, bs, ts, IDX_DIM).transpose(1, 0, 2, 3), pad_q)
    qi_s = qi_s.reshape(bs, N_IDX_HEADS * QPAD, IDX_DIM)
    wi_sp = jnp.pad(wi_s.reshape(bs, ts, N_IDX_HEADS).transpose(0, 2, 1), ((0, 0), (0, 0), (0, QPAD - ts)))
    wi_sp = wi_sp.reshape(bs, N_IDX_HEADS * QPAD, 1)
    ki_new_t = jnp.pad(kis16.reshape(bs, ts, IDX_DIM).transpose(0, 2, 1), ((0, 0), (0, 0), (0, page - ts)))
    expand = jnp.repeat(jnp.eye(page, dtype=BF16), N_KV_HEADS, axis=1)
    kidx_t = jnp.swapaxes(cache_kidx.reshape(n_pool, page, IDX_DIM), 1, 2)
    bias_s = _sample_index(page_table, qi_s, wi_sp, ki_new_t, expand, kidx_t, ts, k_top_s)

    q_s = jnp.pad(q_hm_s.reshape(N_HEADS, bs, ts, HEAD_DIM).transpose(1, 0, 2, 3), pad_q)
    q_s = q_s.reshape(bs, N_HEADS * QPAD, HEAD_DIM)
    prow = page * N_KV_HEADS
    col_head = jnp.arange(prow) % N_KV_HEADS
    row_head = jnp.arange(N_HEADS * QPAD) // (KV_GROUP * QPAD)
    gmask = jnp.where(row_head[:, None] == col_head[None, :], 0.0, NEG_INF).astype(F32)
    pad_new = ((0, 0), (0, page - ts), (0, 0), (0, 0))
    k_new = jnp.pad(ks16.reshape(bs, ts, N_KV_HEADS, HEAD_DIM), pad_new).reshape(bs, prow, HEAD_DIM)
    v_new = jnp.pad(vs16.reshape(bs, ts, N_KV_HEADS, HEAD_DIM), pad_new).reshape(bs, prow, HEAD_DIM)
    ck = cache_k.reshape(n_pool, prow, HEAD_DIM)
    cv = cache_v.reshape(n_pool, prow, HEAD_DIM)
    attn_hq = _sample_attn(page_table, q_s, bias_s, gmask, k_new, v_new, ck, cv)
    attn_s = attn_hq.reshape(bs, N_HEADS, QPAD, HEAD_DIM)[:, :, :ts].transpose(0, 2, 1, 3)
    attn_s = attn_s.reshape(ns, D_ATTN).astype(BF16)

    merged_s = _merge(u_s, attn_s, wc, wa, gates_s)
    x1_s = _out_proj(merged_s, wo, xs, mods_s, rows_s)
    y_s = _ffn(x1_s, g_ffn[0], g_final, mods_s, rows_s, wu16, wd16)

    return (y_p.reshape(bp, sp, d),
            y_s.reshape(bs, ts, d),
            k32.reshape(1, bp, sp, N_KV_HEADS, HEAD_DIM),
            v32.reshape(1, bp, sp, N_KV_HEADS, HEAD_DIM),
            ki32.reshape(1, bp, sp, IDX_DIM),
            conv_p.reshape(1, bp, CONV_W - 1, dc),
            ks32.reshape(1, bs, ts, N_KV_HEADS, HEAD_DIM),
            vs32.reshape(1, bs, ts, N_KV_HEADS, HEAD_DIM),
            kis32.reshape(1, bs, ts, IDX_DIM),
            conv_s.reshape(1, bs, CONV_W - 1, dc))
```

```python
import functools

import jax
import jax.numpy as jnp
from jax import lax
from jax.experimental import pallas as pl
from jax.experimental.pallas import tpu as pltpu

N_HEADS = 16
HEAD_DIM = 128
N_KV_HEADS = 4
KV_GROUP = N_HEADS // N_KV_HEADS
N_IDX_HEADS = 16
IDX_DIM = 64
TOPK_MAX = 256
CONV_W = 3
ROPE_THETA = 10000.0
NORM_EPS = 1e-6
N_MOD = 6
D_ATTN = N_HEADS * HEAD_DIM
D_KV = N_KV_HEADS * HEAD_DIM
D_QI = N_IDX_HEADS * IDX_DIM

V7X_LANES = 128
V7X_VMEM_LIMIT_BYTES = 56 * 1024 * 1024

BF16 = jnp.bfloat16
F32 = jnp.float32
NEG_INF = float("-inf")
LOG2_E = 1.4426950408889634

BISECT_STEPS = 17


def _tile(n, target, mult=8):
    if n <= target:
        return n
    t = (target // mult) * mult
    while t >= mult:
        if n % t == 0:
            return t
        t -= mult
    return n


def _params(*sem):
    return pltpu.CompilerParams(dimension_semantics=sem, vmem_limit_bytes=V7X_VMEM_LIMIT_BYTES)


def _dot(a, b):
    return jnp.dot(a, b, preferred_element_type=F32)


def _dot_nt(a, b):
    return lax.dot_general(a, b, (((1,), (1,)), ((), ())), preferred_element_type=F32)


def _ada_kernel(c_ref, w_ref, b_ref, o_ref):
    c = c_ref[...]
    a = (c * jax.nn.sigmoid(c)).astype(BF16)
    o_ref[...] = _dot(a, w_ref[...].astype(BF16)) + b_ref[...]


def _ada(c, w, b):
    m, d = c.shape
    n = w.shape[1]
    tn = _tile(n, 512, V7X_LANES)
    return pl.pallas_call(
        _ada_kernel,
        grid=(n // tn,),
        in_specs=[pl.BlockSpec((m, d), lambda j: (0, 0)),
                  pl.BlockSpec((d, tn), lambda j: (0, j)),
                  pl.BlockSpec((1, tn), lambda j: (0, j))],
        out_specs=pl.BlockSpec((m, tn), lambda j: (0, j)),
        out_shape=jax.ShapeDtypeStruct((m, n), F32),
        compiler_params=_params("arbitrary"),
        name="ada",
    )(c, w, b.reshape(1, n))


class _Rows:
    def __init__(self, n_rows, rows_per_group, per_token):
        self.n = n_rows
        self.rows_per_group = rows_per_group
        self.per_token = per_token

    def mod_spec(self, tm, d, which, ncol_blocks_per_mod=1, tiled=False):
        tn = d // ncol_blocks_per_mod
        rpg = self.rows_per_group
        if self.per_token:
            shape = (None, tm, tn)
            where = lambda i: (0, i)
        else:
            shape = (None, 1, tn)
            where = lambda i: ((i * tm) // rpg, 0)
        if not tiled:
            return pl.BlockSpec(shape, lambda i, *_: (*where(i), which * ncol_blocks_per_mod))
        return pl.BlockSpec(shape, lambda i, j: (*where(i), which * ncol_blocks_per_mod + j))


def _norm_mod(x, g, sc, sh):
    ms = jnp.mean(x * x, axis=-1, keepdims=True)
    y = x * lax.rsqrt(ms + NORM_EPS) * g
    return y * (1.0 + sc) + sh


def _norm_mod_kernel(x_ref, g_ref, sc_ref, sh_ref, o_ref):
    o_ref[...] = _norm_mod(x_ref[...], g_ref[...], sc_ref[...], sh_ref[...]).astype(o_ref.dtype)


def _norm_mod_call(x, g, mods3, rows, which_sc, which_sh):
    n, d = x.shape
    tm = _tile(min(n, rows.rows_per_group if not rows.per_token else n), 256)
    return pl.pallas_call(
        _norm_mod_kernel,
        grid=(n // tm,),
        in_specs=[pl.BlockSpec((tm, d), lambda i: (i, 0)),
                  pl.BlockSpec((1, d), lambda i: (0, 0)),
                  rows.mod_spec(tm, d, which_sc),
                  rows.mod_spec(tm, d, which_sh)],
        out_specs=pl.BlockSpec((tm, d), lambda i: (i, 0)),
        out_shape=jax.ShapeDtypeStruct((n, d), BF16),
        compiler_params=_params("parallel"),
        name="norm_mod",
    )(x, g.reshape(1, d), mods3, mods3)


BF16_SUBLANE_TILE = 16


def _rows_spec(rows, k, row0_of):
    def index(i, j):
        row0 = row0_of(j)
        return (row0 if isinstance(row0, int) else pl.multiple_of(row0, BF16_SUBLANE_TILE), 0)

    return pl.BlockSpec((pl.Element(rows), pl.Element(k)), index)


def _proj_kernel(x_ref, w_ref, *rest, epilogue, n_extra, w_t):
    dot = _dot_nt if w_t else _dot
    epilogue(dot(x_ref[...], w_ref[...]), rest[:n_extra], rest[n_extra:])


def _proj(x, w, col0, m, *, tm, tn, epilogue, extra, extra_specs, out_shape, out_specs, name, w_t=False):
    n, k = x.shape
    assert n % tm == 0 and m % tn == 0 and w.dtype == BF16
    if w_t:
        assert col0 % BF16_SUBLANE_TILE == 0 and w.shape[1] == k
        w_spec = _rows_spec(tn, k, lambda j: col0 + j * tn)
    else:
        assert col0 % tn == 0 and w.shape[0] == k
        w_spec = pl.BlockSpec((k, tn), lambda i, j: (0, col0 // tn + j))
    return pl.pallas_call(
        functools.partial(_proj_kernel, epilogue=epilogue, n_extra=len(extra), w_t=w_t),
        grid=(n // tm, m // tn),
        in_specs=[pl.BlockSpec((tm, k), lambda i, j: (i, 0)), w_spec] + list(extra_specs),
        out_specs=out_specs,
        out_shape=out_shape,
        compiler_params=_params("parallel", "arbitrary"),
        name=name,
    )(x, w, *extra)


PROJ_TM = 1024
PROJ_TN = 1024


def _proj_tn(m):
    return _tile(m, PROJ_TN, V7X_LANES)


def _cast_epilogue(acc, extra, outs):
    outs[0][...] = acc.astype(outs[0].dtype)


def _mm(x, w, col0, m, out_dtype, name, w_t=False):
    n = x.shape[0]
    tm, tn = _tile(n, PROJ_TM), _proj_tn(m)
    return _proj(x, w, col0, m, tm=tm, tn=tn, epilogue=_cast_epilogue, extra=(), extra_specs=(),
                 out_shape=jax.ShapeDtypeStruct((n, m), out_dtype),
                 out_specs=pl.BlockSpec((tm, tn), lambda i, j: (i, j)), name=name, w_t=w_t)


def _swap_halves(x, head_w):
    if head_w == V7X_LANES:
        return pltpu.roll(x, V7X_LANES // 2, axis=1)
    half = head_w // 2
    lane = lax.broadcasted_iota(jnp.int32, x.shape, 1)
    take_upper = (lane & (head_w - 1)) < half
    return jnp.where(take_upper, pltpu.roll(x, V7X_LANES - half, axis=1), pltpu.roll(x, half, axis=1))


def _rope_cols(acc, cos, sin, head_w):
    out = []
    for c in range(acc.shape[1] // V7X_LANES):
        x = acc[:, c * V7X_LANES:(c + 1) * V7X_LANES]
        out.append(x * cos + _swap_halves(x, head_w) * sin)
    return out


def _rope_q_epilogue(acc, extra, outs, *, head_w):
    cos_ref, sin_ref = extra
    o_ref = outs[0]
    cols = _rope_cols(acc, cos_ref[...], sin_ref[...], head_w)
    per = V7X_LANES // head_w
    for c, x in enumerate(cols):
        for p in range(per):
            o_ref[c * per + p] = x[:, p * head_w:(p + 1) * head_w].astype(o_ref.dtype)


def _mm_rope_q(x, w, col0, m, cos, sin, head_w, name, w_t=False):
    n = x.shape[0]
    tm, tn = _tile(min(n, cos.shape[0]), PROJ_TM), _proj_tn(m)
    hpt = tn // head_w
    table = _table_spec(cos, tm)
    return _proj(x, w, col0, m, tm=tm, tn=tn, epilogue=functools.partial(_rope_q_epilogue, head_w=head_w),
                 extra=(cos, sin), extra_specs=(table, table),
                 out_shape=jax.ShapeDtypeStruct((m // head_w, n, head_w), BF16),
                 out_specs=pl.BlockSpec((hpt, tm, head_w), lambda i, j: (j, i, 0)), name=name, w_t=w_t)


def _table_spec(table, tm):
    period = table.shape[0] // tm
    assert period * tm == table.shape[0]
    return pl.BlockSpec((tm, V7X_LANES), lambda i, j: (i % period, 0))


KVX_TM = 512


def _kvx_kernel(h_ref, wkv_ref, wx_ref, cos_ref, sin_ref, cos_x_ref, sin_x_ref,
                k32_ref, k16_ref, v32_ref, v16_ref, x32_ref, x16_ref):
    h = h_ref[...]
    kv = _dot_nt(h, wkv_ref[...])
    tm = kv.shape[0]
    for c, k in enumerate(_rope_cols(kv[:, :D_KV], cos_ref[...], sin_ref[...], HEAD_DIM)):
        k32_ref[pl.ds(c, tm, stride=N_KV_HEADS), :] = k
        k16_ref[:, c * V7X_LANES:(c + 1) * V7X_LANES] = k.astype(BF16)
    v = kv[:, D_KV:]
    for c in range(N_KV_HEADS):
        v32_ref[pl.ds(c, tm, stride=N_KV_HEADS), :] = v[:, c * HEAD_DIM:(c + 1) * HEAD_DIM]
    v16_ref[...] = v.astype(BF16)
    (x,) = _rope_cols(_dot_nt(h, wx_ref[...]), cos_x_ref[...], sin_x_ref[...], IDX_DIM)
    x32_ref[...] = x
    x16_ref[...] = x.astype(BF16)


def _mm_kvx(x, w_t, row_kv, row_x, cos, sin, cos_x, sin_x):
    n, kdim = x.shape
    assert row_x + V7X_LANES <= w_t.shape[0]
    tm = _tile(min(n, cos.shape[0]), KVX_TM)
    table = _table_spec(cos, tm)
    kv32 = (N_KV_HEADS, HEAD_DIM)
    layouts = [kv32, (1, D_KV), kv32, (1, D_KV), (1, V7X_LANES), (1, V7X_LANES)]
    dtypes = [F32, BF16] * 3
    shapes = [jax.ShapeDtypeStruct((n * r, wd), dt) for (r, wd), dt in zip(layouts, dtypes)]
    specs = [pl.BlockSpec((tm * r, wd), lambda i, j: (i, 0)) for r, wd in layouts]
    return pl.pallas_call(
        _kvx_kernel,
        grid=(n // tm, 1),
        in_specs=[pl.BlockSpec((tm, kdim), lambda i, j: (i, 0)),
                  _rows_spec(2 * D_KV, kdim, lambda j: row_kv),
                  _rows_spec(V7X_LANES, kdim, lambda j: row_x)] + [table] * 4,
        out_specs=specs,
        out_shape=shapes,
        compiler_params=_params("parallel", "arbitrary"),
        name="in_kvx",
    )(x, w_t, w_t, cos, sin, cos_x, sin_x)


PREV_ROWS = 16


def _conv_prompt_kernel(xin_ref, bg_ref, cg_ref, xin_prev_ref, cg_prev_ref, w_ref, u_ref, st_ref,
                        *, blocks_per_seq):
    i = pl.program_id(1)
    p = cg_ref[...].astype(F32) * xin_ref[...].astype(F32)
    prev = cg_prev_ref[...].astype(F32) * xin_prev_ref[...].astype(F32)
    prev = jnp.where(i % blocks_per_seq == 0, 0.0, prev)
    tm = p.shape[0]
    last = PREV_ROWS - 1
    row = lax.broadcasted_iota(jnp.int32, p.shape, 0)
    p1 = pltpu.roll(p, 1, axis=0)
    p1 = jnp.where(row == 0, prev[last:last + 1], p1)
    p2 = pltpu.roll(p, 2, axis=0)
    p2 = jnp.where(row == 0, prev[last - 1:last], jnp.where(row == 1, prev[last:last + 1], p2))
    w = w_ref[...]
    y = p2 * w[0:1] + p1 * w[1:2] + p * w[2:3]
    u_ref[...] = (bg_ref[...].astype(F32) * y).astype(u_ref.dtype)
    st_ref[...] = p[tm - (CONV_W - 1):, :]


def _conv_prompt(zc, conv_w, n_seq, seq):
    n = zc.shape[0]
    dc = zc.shape[1] // 3
    tm = _tile(seq, 512)
    tc = _tile(dc, 512, V7X_LANES)
    ncb = dc // tc
    bps = seq // tm
    prev_row_blk = lambda i: jnp.maximum(i * (tm // PREV_ROWS) - 1, 0)
    u, st = pl.pallas_call(
        functools.partial(_conv_prompt_kernel, blocks_per_seq=bps),
        grid=(ncb, n // tm),
        in_specs=[pl.BlockSpec((tm, tc), lambda j, i: (i, j)),
                  pl.BlockSpec((tm, tc), lambda j, i: (i, ncb + j)),
                  pl.BlockSpec((tm, tc), lambda j, i: (i, 2 * ncb + j)),
                  pl.BlockSpec((PREV_ROWS, tc), lambda j, i: (prev_row_blk(i), j)),
                  pl.BlockSpec((PREV_ROWS, tc), lambda j, i: (prev_row_blk(i), 2 * ncb + j)),
                  pl.BlockSpec((CONV_W, tc), lambda j, i: (0, j))],
        out_specs=[pl.BlockSpec((tm, tc), lambda j, i: (i, j)),
                   pl.BlockSpec((None, CONV_W - 1, tc), lambda j, i: (i // bps, 0, j))],
        out_shape=[jax.ShapeDtypeStruct((n, dc), BF16),
                   jax.ShapeDtypeStruct((n_seq, CONV_W - 1, dc), F32)],
        compiler_params=_params("parallel", "arbitrary"),
        name="conv_prompt",
    )(zc, zc, zc, zc, zc, conv_w)
    return u, st


def _conv_sample_kernel(xin_ref, bg_ref, cg_ref, st_ref, w_ref, u_ref, nst_ref):
    t_len = xin_ref.shape[0]
    w = w_ref[...]
    ext = [st_ref[j] for j in range(CONV_W - 1)]
    ext += [cg_ref[t].astype(F32) * xin_ref[t].astype(F32) for t in range(t_len)]
    for t in range(t_len):
        y = ext[t] * w[0:1]
        for j in range(1, CONV_W):
            y = y + ext[t + j] * w[j:j + 1]
        u_ref[t] = (bg_ref[t].astype(F32) * y).astype(u_ref.dtype)
    for j in range(CONV_W - 1):
        nst_ref[j] = ext[t_len + j]


def _conv_sample(zc_t, st_t, conv_w):
    t_len, b, dc3 = zc_t.shape
    dc = dc3 // 3
    tc = _tile(dc, 512, V7X_LANES)
    ncb = dc // tc
    return pl.pallas_call(
        _conv_sample_kernel,
        grid=(ncb,),
        in_specs=[pl.BlockSpec((t_len, b, tc), lambda j: (0, 0, j)),
                  pl.BlockSpec((t_len, b, tc), lambda j: (0, 0, ncb + j)),
                  pl.BlockSpec((t_len, b, tc), lambda j: (0, 0, 2 * ncb + j)),
                  pl.BlockSpec((CONV_W - 1, b, tc), lambda j: (0, 0, j)),
                  pl.BlockSpec((CONV_W, tc), lambda j: (0, j))],
        out_specs=[pl.BlockSpec((t_len, b, tc), lambda j: (0, 0, j)),
                   pl.BlockSpec((CONV_W - 1, b, tc), lambda j: (0, 0, j))],
        out_shape=[jax.ShapeDtypeStruct((t_len, b, dc), BF16),
                   jax.ShapeDtypeStruct((CONV_W - 1, b, dc), F32)],
        compiler_params=_params("parallel"),
        name="conv_sample",
    )(zc_t, zc_t, zc_t, st_t, conv_w)


_SUBLANES = 8
_REDUCERS = {jnp.add: jnp.sum, jnp.maximum: jnp.max, jnp.minimum: jnp.min}


def _fold_chunks(n_chunks, term, combine, init, shape, axis):
    if axis == 0 and shape[0] > _SUBLANES:
        full_term = term
        shape = (_SUBLANES, shape[1])
        term = lambda c: _REDUCERS[combine](full_term(c).reshape(-1, *shape), axis=0)
    if isinstance(n_chunks, int):
        vals = [term(c) for c in range(n_chunks)]
        while len(vals) > 1:
            vals = [combine(vals[i], vals[i + 1]) if i + 1 < len(vals) else vals[i]
                    for i in range(0, len(vals), 2)]
        return vals[0]
    return lax.fori_loop(0, n_chunks, lambda c, acc: combine(acc, term(c)), jnp.full(shape, init, F32))


def _topk_threshold(score_at, n_chunks, want, shape, axis):
    fold = functools.partial(_fold_chunks, n_chunks, shape=shape, axis=axis)

    def count_ge(thr):
        part = fold(lambda c: jnp.where(score_at(c) >= thr, 1.0, 0.0), jnp.add, 0.0)
        return jnp.sum(part, axis=axis, keepdims=True)

    hi = jnp.max(fold(score_at, jnp.maximum, NEG_INF), axis=axis, keepdims=True)
    lo = jnp.min(fold(lambda c: jnp.where(score_at(c) == NEG_INF, jnp.inf, score_at(c)), jnp.minimum, jnp.inf),
                 axis=axis, keepdims=True)
    cnt_lo = count_ge(lo)
    cnt_hi = count_ge(hi)
    top_ok = cnt_hi >= want
    lo = jnp.where(top_ok, hi, lo)
    cnt_lo = jnp.where(top_ok, cnt_hi, cnt_lo)

    def bisect(_, carry):
        lo, hi, cnt_lo = carry
        mid = lo + (hi - lo) * 0.5
        cnt = count_ge(mid)
        ok = cnt >= want
        return (jnp.where(ok, mid, lo), jnp.where(ok, hi, mid), jnp.where(ok, cnt, cnt_lo))

    lo, hi, cnt_lo = lax.fori_loop(0, BISECT_STEPS, bisect, (lo, hi, cnt_lo))

    def pending_of(cnt_lo, done):
        return jnp.max(jnp.where((cnt_lo > want) & (done == 0.0), 1.0, 0.0))

    def finish(carry):
        lo, hi, cnt_lo, done, _ = carry
        cand = jnp.max(fold(lambda c: jnp.where(score_at(c) < hi, score_at(c), NEG_INF), jnp.maximum, NEG_INF),
                       axis=axis, keepdims=True)
        cnt = count_ge(cand)
        active = (cnt_lo > want) & (done == 0.0)
        ok = active & (cnt >= want)
        lo = jnp.where(ok, cand, lo)
        cnt_lo = jnp.where(ok, cnt, cnt_lo)
        done = jnp.where(ok, 1.0, done)
        hi = jnp.where(active & (cnt < want), cand, hi)
        return lo, hi, cnt_lo, done, pending_of(cnt_lo, done)

    done0 = jnp.where(lo == hi, 1.0, 0.0)
    lo, *_ = lax.while_loop(lambda carry: carry[4] > 0.0, finish,
                            (lo, hi, cnt_lo, done0, pending_of(cnt_lo, done0)))
    return lo


ATTN_TQ = 256
ATTN_KC = 256


def _attn_prompt_kernel(qi_ref, wi_ref, ki_ref, q_ref, k_ref, vt_ref, o_ref,
                        score_ref, m_ref, l_ref, acc_ref, *, tq, kc, k_top):
    qb = pl.program_id(1)
    n_chunks = (qb * tq) // kc + tq // kc
    qpos = qb * tq + lax.broadcasted_iota(jnp.int32, (1, tq), 1)
    wi = wi_ref[...]

    def fill(c, _):
        kic = ki_ref[pl.ds(pl.multiple_of(c * kc, kc), kc), :]
        acc = jnp.zeros((kc, tq), F32)
        for h in range(N_IDX_HEADS):
            acc = acc + jnp.maximum(_dot_nt(kic, qi_ref[h]), 0.0) * wi[h:h + 1]
        kpos = c * kc + lax.broadcasted_iota(jnp.int32, (kc, tq), 0)
        score_ref[c] = jnp.where(kpos <= qpos, acc, NEG_INF)
        return 0

    lax.fori_loop(0, n_chunks, fill, 0)

    want = jnp.minimum(qpos + 1, k_top).astype(F32)
    thr = _topk_threshold(lambda c: score_ref[c], n_chunks, want, (kc, tq), axis=0)

    scale = HEAD_DIM ** -0.5 * LOG2_E
    for g in range(N_KV_HEADS):
        m_ref[...] = jnp.full(m_ref.shape, NEG_INF, F32)
        l_ref[...] = jnp.zeros(l_ref.shape, F32)
        acc_ref[...] = jnp.zeros(acc_ref.shape, F32)

        heads = range(KV_GROUP)

        def scores(c):
            kch = k_ref[pl.ds(pl.multiple_of(c * kc, kc), kc), g * HEAD_DIM:(g + 1) * HEAD_DIM]
            bias = jnp.where(score_ref[c] >= thr, 0.0, NEG_INF)
            return [_dot_nt(kch, q_ref[g * KV_GROUP + r]) * scale + bias for r in heads]

        def update(c, s):
            vt = vt_ref[c, g * HEAD_DIM:(g + 1) * HEAD_DIM, :]
            m_old = [m_ref[r] for r in heads]
            l_old = [l_ref[r] for r in heads]
            m_new = [jnp.maximum(m_old[r], jnp.max(s[r], axis=0, keepdims=True)) for r in heads]
            m_safe = [jnp.where(m_new[r] == NEG_INF, 0.0, m_new[r]) for r in heads]
            p = [jnp.exp2(s[r] - m_safe[r]) for r in heads]
            alpha = [jnp.exp2(m_old[r] - m_safe[r]) for r in heads]
            pv = [_dot(vt, p[r].astype(BF16)) for r in heads]
            for r in heads:
                l_ref[r] = alpha[r] * l_old[r] + jnp.sum(p[r], axis=0, keepdims=True)
                acc_ref[r] = alpha[r] * acc_ref[r] + pv[r]
                m_ref[r] = m_new[r]

        def attend_pair(i, _):
            sa, sb = scores(2 * i), scores(2 * i + 1)
            update(2 * i, sa)
            update(2 * i + 1, sb)
            return 0

        lax.fori_loop(0, n_chunks // 2, attend_pair, 0)

        @pl.when(n_chunks % 2 == 1)
        def _():
            update(n_chunks - 1, scores(n_chunks - 1))
        for r in range(KV_GROUP):
            h = g * KV_GROUP + r
            out = acc_ref[r] / l_ref[r]
            o_ref[:, h * HEAD_DIM:(h + 1) * HEAD_DIM] = out.T.astype(o_ref.dtype)


def _attn_prompt(qi_hm, wi_t, ki16, q_hm, k16, vt16, n_seq, seq, k_top):
    n = n_seq * seq
    tq = _tile(seq, ATTN_TQ)
    kc = _tile(tq, ATTN_KC)
    nqb = seq // tq
    assert vt16.shape == (n_seq, seq // kc, D_KV, kc)
    kernel = functools.partial(_attn_prompt_kernel, tq=tq, kc=kc, k_top=k_top)
    return pl.pallas_call(
        kernel,
        grid=(n_seq, nqb),
        in_specs=[pl.BlockSpec((N_IDX_HEADS, tq, IDX_DIM), lambda b, q: (0, b * nqb + q, 0)),
                  pl.BlockSpec((N_IDX_HEADS, tq), lambda b, q: (0, b * nqb + q)),
                  pl.BlockSpec((seq, IDX_DIM), lambda b, q: (b, 0)),
                  pl.BlockSpec((N_HEADS, tq, HEAD_DIM), lambda b, q: (0, b * nqb + q, 0)),
                  pl.BlockSpec((seq, D_KV), lambda b, q: (b, 0)),
                  pl.BlockSpec((None, seq // kc, D_KV, kc), lambda b, q: (b, 0, 0, 0))],
        out_specs=pl.BlockSpec((tq, D_ATTN), lambda b, q: (b * nqb + q, 0)),
        out_shape=jax.ShapeDtypeStruct((n, D_ATTN), BF16),
        scratch_shapes=[pltpu.VMEM((seq // kc, kc, tq), F32),
                        pltpu.VMEM((KV_GROUP, 1, tq), F32),
                        pltpu.VMEM((KV_GROUP, 1, tq), F32),
                        pltpu.VMEM((KV_GROUP, HEAD_DIM, tq), F32)],
        compiler_params=_params("parallel", "arbitrary"),
        name="attn_prompt",
    )(qi_hm, wi_t, ki16, q_hm, k16, vt16)


def _merge_kernel(u_ref, a_ref, wc_ref, wa_ref, gc_ref, ga_ref, o_ref):
    yc = _dot(u_ref[...], wc_ref[...])
    ya = _dot(a_ref[...], wa_ref[...])
    gc = jax.nn.sigmoid(gc_ref[...].astype(F32))
    ga = jax.nn.sigmoid(ga_ref[...].astype(F32))
    o_ref[...] = (gc * yc + ga * ya).astype(o_ref.dtype)


def _merge(u, attn, wc, wa, gates):
    n, dc = u.shape
    da = attn.shape[1]
    d = wc.shape[1]
    tm = _tile(n, PROJ_TM)
    tn = _proj_tn(d)
    ncb = d // tn
    return pl.pallas_call(
        _merge_kernel,
        grid=(n // tm, ncb),
        in_specs=[pl.BlockSpec((tm, dc), lambda i, j: (i, 0)),
                  pl.BlockSpec((tm, da), lambda i, j: (i, 0)),
                  pl.BlockSpec((dc, tn), lambda i, j: (0, j)),
                  pl.BlockSpec((da, tn), lambda i, j: (0, j)),
                  pl.BlockSpec((tm, tn), lambda i, j: (i, j)),
                  pl.BlockSpec((tm, tn), lambda i, j: (i, ncb + j))],
        out_specs=pl.BlockSpec((tm, tn), lambda i, j: (i, j)),
        out_shape=jax.ShapeDtypeStruct((n, d), BF16),
        compiler_params=_params("parallel", "arbitrary"),
        name="merge",
    )(u, attn, wc, wa, gates, gates)


def _out_proj_epilogue(acc, extra, outs):
    x_ref, gt_ref = extra
    outs[0][...] = x_ref[...] + gt_ref[...] * acc


def _out_proj(a, w, x, mods3, rows):
    n = a.shape[0]
    d = w.shape[1]
    tm = _tile(min(n, rows.rows_per_group if not rows.per_token else n), PROJ_TM)
    tn = _proj_tn(d)
    tile = pl.BlockSpec((tm, tn), lambda i, j: (i, j))
    gate = rows.mod_spec(tm, d, 2, d // tn, tiled=True)
    return _proj(a, w, 0, d, tm=tm, tn=tn, epilogue=_out_proj_epilogue, extra=(x, mods3), extra_specs=(tile, gate),
                 out_shape=jax.ShapeDtypeStruct((n, d), F32), out_specs=tile, name="out_proj")


def _ffn_kernel(x_ref, g_ref, sc_ref, sh_ref, gt_ref, gf_ref, wu_ref, wd_ref, o_ref, h_ref):
    f = pl.program_id(1)

    @pl.when(f == 0)
    def _():
        h_ref[...] = _norm_mod(x_ref[...], g_ref[...], sc_ref[...], sh_ref[...]).astype(BF16)
        o_ref[...] = jnp.zeros(o_ref.shape, F32)

    u = jnp.maximum(_dot(h_ref[...], wu_ref[...]), 0.0)
    o_ref[...] += _dot((u * u).astype(BF16), wd_ref[...])

    @pl.when(f == pl.num_programs(1) - 1)
    def _():
        x2 = x_ref[...] + gt_ref[...] * o_ref[...]
        ms = jnp.mean(x2 * x2, axis=-1, keepdims=True)
        o_ref[...] = x2 * lax.rsqrt(ms + NORM_EPS) * gf_ref[...]


def _ffn(x1, g_ffn, g_final, mods3, rows, w_up, w_down):
    n, d = x1.shape
    dff = w_up.shape[1]
    tm = _tile(min(n, rows.rows_per_group if not rows.per_token else n), 512)
    tf = _tile(dff, 512, V7X_LANES)
    return pl.pallas_call(
        _ffn_kernel,
        grid=(n // tm, dff // tf),
        in_specs=[pl.BlockSpec((tm, d), lambda i, f: (i, 0), pipeline_mode=pl.Buffered(1)),
                  pl.BlockSpec((1, d), lambda i, f: (0, 0)),
                  rows.mod_spec(tm, d, 4),
                  rows.mod_spec(tm, d, 3),
                  rows.mod_spec(tm, d, 5),
                  pl.BlockSpec((1, d), lambda i, f: (0, 0)),
                  pl.BlockSpec((d, tf), lambda i, f: (0, f)),
                  pl.BlockSpec((tf, d), lambda i, f: (f, 0))],
        out_specs=pl.BlockSpec((tm, d), lambda i, f: (i, 0)),
        out_shape=jax.ShapeDtypeStruct((n, d), F32),
        scratch_shapes=[pltpu.VMEM((tm, d), BF16)],
        compiler_params=_params("parallel", "arbitrary"),
        name="ffn",
    )(x1, g_ffn.reshape(1, d), mods3, mods3, mods3, g_final.reshape(1, d), w_up, w_down)


QPAD = 8
INDEX_PAGES_PER_DOT = 8


def _sample_index_kernel(pt_ref, qi_ref, wi_ref, kin_ref, expand_ref, cache_ref, bias_ref,
                         kbuf_ref, sem_ref, score_ref, *, n_pages, page, t_len, k_top):
    b = pl.program_id(0)

    slot = b % 2

    def page_copy(bb, p):
        return pltpu.make_async_copy(cache_ref.at[pt_ref[bb, p]], kbuf_ref.at[bb % 2, p], sem_ref.at[bb % 2, p])

    @pl.when(b == 0)
    def _():
        for p in range(n_pages):
            page_copy(0, p).start()

    @pl.when(b + 1 < pl.num_programs(0))
    def _():
        for p in range(n_pages):
            page_copy(b + 1, p).start()

    qi = qi_ref[...]
    wi = wi_ref[...].reshape(N_IDX_HEADS, QPAD, 1)

    def head_sum(s):
        s3 = jnp.maximum(s, 0.0).reshape(N_IDX_HEADS, QPAD, s.shape[1]) * wi
        return jnp.sum(s3, axis=0)

    ppd = INDEX_PAGES_PER_DOT if n_pages % INDEX_PAGES_PER_DOT == 0 else 1
    for p0 in range(0, n_pages, ppd):
        for p in range(p0, p0 + ppd):
            page_copy(b, p).wait()
        kt = jnp.concatenate([kbuf_ref[slot, p] for p in range(p0, p0 + ppd)], axis=1).astype(BF16)
        sc = head_sum(_dot(qi, kt))
        for j in range(ppd):
            score_ref[p0 + j] = sc[:, j * page:(j + 1) * page]

    qrow = lax.broadcasted_iota(jnp.int32, (QPAD, page), 0)
    lane = lax.broadcasted_iota(jnp.int32, (QPAD, page), 1)
    s_new = head_sum(_dot(qi, kin_ref[...]))
    score_ref[n_pages] = jnp.where((lane <= qrow) & (lane < t_len), s_new, NEG_INF)

    n_chunks = n_pages + 1
    thr = _topk_threshold(lambda c: score_ref[c], n_chunks, float(k_top), (QPAD, page), axis=1)

    sel = [jnp.where(score_ref[c] >= thr, 1.0, 0.0) for c in range(n_chunks)]
    sel += [jnp.zeros((QPAD, page), F32)] * (n_chunks % 2)
    hit = _dot(jnp.concatenate(sel, axis=0).astype(BF16), expand_ref[...])
    bias = jnp.where(hit > 0.5, 0.0, NEG_INF)
    for c in range(n_chunks):
        bias_ref[c] = bias[c * QPAD:(c + 1) * QPAD]


def _sample_index(page_table, qi_s, wi_s, ki_new_t, expand, cache_kidx_t, t_len, k_top):
    bsz, n_pages = page_table.shape
    page = cache_kidx_t.shape[2]
    kernel = functools.partial(_sample_index_kernel, n_pages=n_pages, page=page, t_len=t_len, k_top=k_top)
    grid_spec = pltpu.PrefetchScalarGridSpec(
        num_scalar_prefetch=1,
        grid=(bsz,),
        in_specs=[pl.BlockSpec((None, N_IDX_HEADS * QPAD, IDX_DIM), lambda b, pt: (b, 0, 0)),
                  pl.BlockSpec((None, N_IDX_HEADS * QPAD, 1), lambda b, pt: (b, 0, 0)),
                  pl.BlockSpec((None, IDX_DIM, page), lambda b, pt: (b, 0, 0)),
                  pl.BlockSpec(expand.shape, lambda b, pt: (0, 0)),
                  pl.BlockSpec(memory_space=pl.ANY)],
        out_specs=pl.BlockSpec((None, n_pages + 1, QPAD, N_KV_HEADS * page), lambda b, pt: (b, 0, 0, 0)),
        scratch_shapes=[pltpu.VMEM((2, n_pages, IDX_DIM, page), F32),
                        pltpu.SemaphoreType.DMA((2, n_pages)),
                        pltpu.VMEM((n_pages + 1, QPAD, page), F32)],
    )
    return pl.pallas_call(
        kernel,
        grid_spec=grid_spec,
        out_shape=jax.ShapeDtypeStruct((bsz, n_pages + 1, QPAD, N_KV_HEADS * page), F32),
        compiler_params=_params("arbitrary"),
        name="sample_index",
    )(page_table, qi_s, wi_s, ki_new_t, expand, cache_kidx_t)


PAGES_PER_CHUNK = 8
KV_SLOTS = 3


def _sample_attn_kernel(pt_ref, q_ref, bias_ref, gmask_ref, kn_ref, vn_ref, ck_ref, cv_ref, o_ref,
                        kbuf_ref, vbuf_ref, sem_ref, m_ref, l_ref, acc_ref, *, n_pages, ppc):
    b = pl.program_id(0)
    n_chunks = n_pages // ppc
    rows = N_HEADS * QPAD
    prow = kbuf_ref.shape[2]

    n_total = pl.num_programs(0) * n_chunks

    def copies(g):
        bb, c, slot = g // n_chunks, g % n_chunks, g % KV_SLOTS
        out = []
        for j in range(ppc):
            pid = pt_ref[bb, c * ppc + j]
            out.append(pltpu.make_async_copy(ck_ref.at[pid], kbuf_ref.at[slot, j], sem_ref.at[0, slot, j]))
            out.append(pltpu.make_async_copy(cv_ref.at[pid], vbuf_ref.at[slot, j], sem_ref.at[1, slot, j]))
        return out

    def start(g):
        @pl.when(g < n_total)
        def _():
            for cp in copies(g):
                cp.start()

    @pl.when(b == 0)
    def _():
        for g in range(KV_SLOTS - 1):
            start(g)

    q = q_ref[...]
    gmask = gmask_ref[...]
    m_ref[...] = jnp.full(m_ref.shape, NEG_INF, F32)
    l_ref[...] = jnp.zeros(l_ref.shape, F32)
    acc_ref[...] = jnp.zeros(acc_ref.shape, F32)
    scale = HEAD_DIM ** -0.5 * LOG2_E

    def attend(kch, vch, biases):
        s = _dot_nt(q, kch) * scale
        parts = []
        for j, bias in enumerate(biases):
            sj = s[:, j * prow:(j + 1) * prow].reshape(N_HEADS, QPAD, prow) + bias[None]
            parts.append(sj.reshape(rows, prow) + gmask)
        s = parts[0] if len(parts) == 1 else jnp.concatenate(parts, axis=1)
        m_old = m_ref[...]
        m_new = jnp.maximum(m_old, jnp.max(s, axis=1, keepdims=True))
        m_safe = jnp.where(m_new == NEG_INF, 0.0, m_new)
        p = jnp.exp2(s - m_safe)
        alpha = jnp.exp2(m_old - m_safe)
        l_ref[...] = alpha * l_ref[...] + jnp.sum(p, axis=1, keepdims=True)
        acc_ref[...] = alpha * acc_ref[...] + _dot(p.astype(BF16), vch)
        m_ref[...] = m_new

    def chunk_body(c, _):
        g = b * n_chunks + c
        slot = g % KV_SLOTS
        start(g + KV_SLOTS - 1)
        for cp in copies(g):
            cp.wait()
        kch = kbuf_ref[slot].reshape(ppc * prow, HEAD_DIM).astype(BF16)
        vch = vbuf_ref[slot].reshape(ppc * prow, HEAD_DIM).astype(BF16)
        attend(kch, vch, [bias_ref[c * ppc + j] for j in range(ppc)])
        return 0

    lax.fori_loop(0, n_chunks, chunk_body, 0)
    attend(kn_ref[...], vn_ref[...], [bias_ref[n_pages]])
    o_ref[...] = acc_ref[...] / l_ref[...]


def _sample_attn(page_table, q_s, bias, gmask, k_new, v_new, cache_k, cache_v):
    bsz, n_pages = page_table.shape
    prow = cache_k.shape[1]
    ppc = PAGES_PER_CHUNK if n_pages % PAGES_PER_CHUNK == 0 else 1
    rows = N_HEADS * QPAD
    kernel = functools.partial(_sample_attn_kernel, n_pages=n_pages, ppc=ppc)
    grid_spec = pltpu.PrefetchScalarGridSpec(
        num_scalar_prefetch=1,
        grid=(bsz,),
        in_specs=[pl.BlockSpec((None, rows, HEAD_DIM), lambda b, pt: (b, 0, 0)),
                  pl.BlockSpec((None, n_pages + 1, QPAD, prow), lambda b, pt: (b, 0, 0, 0)),
                  pl.BlockSpec((rows, prow), lambda b, pt: (0, 0)),
                  pl.BlockSpec((None, prow, HEAD_DIM), lambda b, pt: (b, 0, 0)),
                  pl.BlockSpec((None, prow, HEAD_DIM), lambda b, pt: (b, 0, 0)),
                  pl.BlockSpec(memory_space=pl.ANY),
                  pl.BlockSpec(memory_space=pl.ANY)],
        out_specs=pl.BlockSpec((None, rows, HEAD_DIM), lambda b, pt: (b, 0, 0)),
        scratch_shapes=[pltpu.VMEM((KV_SLOTS, ppc, prow, HEAD_DIM), F32),
                        pltpu.VMEM((KV_SLOTS, ppc, prow, HEAD_DIM), F32),
                        pltpu.SemaphoreType.DMA((2, KV_SLOTS, ppc)),
                        pltpu.VMEM((rows, 1), F32),
                        pltpu.VMEM((rows, 1), F32),
                        pltpu.VMEM((rows, HEAD_DIM), F32)],
    )
    return pl.pallas_call(
        kernel,
        grid_spec=grid_spec,
        out_shape=jax.ShapeDtypeStruct((bsz, rows, HEAD_DIM), F32),
        compiler_params=_params("arbitrary"),
        name="sample_attn",
    )(page_table, q_s, bias, gmask, k_new, v_new, cache_k, cache_v)


def _rope_tables(pos, head_w):
    half = head_w // 2
    inv = ROPE_THETA ** (-jnp.arange(half, dtype=F32) / half)
    ang = pos.astype(F32)[:, None] * inv[None, :]
    cos, sin = jnp.cos(ang), jnp.sin(ang)
    reps = V7X_LANES // head_w
    cos_t = jnp.tile(jnp.concatenate([cos, cos], axis=1), (1, reps))
    sin_t = jnp.tile(jnp.concatenate([-sin, sin], axis=1), (1, reps))
    return cos_t, sin_t


class _InWeights:
    def __init__(self, w_in, d):
        dc = d // 2
        self.t = w_in.T.astype(BF16)
        self.conv = 0
        self.q = 3 * dc
        self.kv = self.q + D_ATTN
        self.qi = self.kv + 2 * D_KV
        self.kiwi = self.qi + D_QI
        self.gates = self.kiwi + IDX_DIM + N_IDX_HEADS
        assert all(r % BF16_SUBLANE_TILE == 0 for r in (self.q, self.kv, self.qi, self.kiwi, self.gates))


def _in_proj(h, w, pos, d):
    cos128, sin128 = _rope_tables(pos, HEAD_DIM)
    cos64, sin64 = _rope_tables(pos, IDX_DIM)
    lane = jnp.arange(V7X_LANES)
    cos_kw = jnp.where(lane < IDX_DIM, cos64, 1.0)
    sin_kw = jnp.where(lane < IDX_DIM, sin64, 0.0)
    dc3 = 3 * (d // 2)
    zc = _mm(h, w.t, w.conv, dc3, BF16, "in_conv", w_t=True)
    gates = _mm(h, w.t, w.gates, 2 * d, BF16, "in_gates", w_t=True)
    q_hm = _mm_rope_q(h, w.t, w.q, D_ATTN, cos128, sin128, HEAD_DIM, "in_q", w_t=True)
    qi_hm = _mm_rope_q(h, w.t, w.qi, D_QI, cos64, sin64, IDX_DIM, "in_qi", w_t=True)
    k32, k16, v32, v16, kw32, kw16 = _mm_kvx(h, w.t, w.kv, w.kiwi, cos128, sin128, cos_kw, sin_kw)
    ki32 = kw32[:, :IDX_DIM]
    ki16 = kw16[:, :IDX_DIM]
    wi = kw32[:, IDX_DIM:IDX_DIM + N_IDX_HEADS] * (N_IDX_HEADS ** -0.5) * (IDX_DIM ** -0.5)
    return zc, gates, q_hm, qi_hm, k32, k16, v32, v16, ki32, ki16, wi


def kernel(x_prompt, x_sample, cache_k, cache_v, cache_kidx, state_conv, page_table, c_prompt, c_sample,
           w_ada, b_ada, g_mix, w_in, conv_w, w_conv_out, w_attn_out, w_out, g_ffn, w_up, w_down, g_final):
    depth = w_ada.shape[0]
    assert depth == 1, "single-layer trunk"
    bp, sp, d = x_prompt.shape
    bs, ts, _ = x_sample.shape
    dc = d // 2
    n_pool, page = cache_k.shape[1], cache_k.shape[2]
    n_pages = page_table.shape[1]
    past = n_pages * page
    assert ts <= QPAD and ts <= page

    c_all = jnp.concatenate([c_prompt, c_sample], axis=0)
    n_c = c_all.shape[0]
    c_all = jnp.pad(c_all, ((0, (-n_c) % 8), (0, 0)))
    mods = _ada(c_all, w_ada[0], b_ada[0])
    mods_p = mods[:bp].reshape(bp, 1, N_MOD * d)
    mods_s = jnp.repeat(mods[bp:bp + bs], ts, axis=0).reshape(1, bs * ts, N_MOD * d)
    rows_p = _Rows(bp * sp, sp, per_token=False)
    rows_s = _Rows(bs * ts, bs * ts, per_token=True)

    w = _InWeights(w_in[0], d)
    wc, wa, wo = w_conv_out[0].astype(BF16), w_attn_out[0].astype(BF16), w_out[0].astype(BF16)
    wu16 = w_up[0].astype(BF16)
    wd16 = w_down[0].astype(BF16)

    xp = x_prompt.reshape(bp * sp, d)
    hp = _norm_mod_call(xp, g_mix[0], mods_p, rows_p, 1, 0)
    pos_p = jnp.arange(sp)
    zc, gates, q_hm, qi_hm, k32, k16, v32, v16, ki32, ki16, wi = _in_proj(hp, w, pos_p, d)
    u_p, conv_p = _conv_prompt(zc, conv_w[0], bp, sp)
    k_top_p = min(TOPK_MAX, sp // 4)
    kc_p = _tile(_tile(sp, ATTN_TQ), ATTN_KC)
    vt16 = v16.reshape(bp, sp // kc_p, kc_p, D_KV).transpose(0, 1, 3, 2)
    attn_p = _attn_prompt(qi_hm, wi.T, ki16, q_hm, k16, vt16, bp, sp, k_top_p)
    merged_p = _merge(u_p, attn_p, wc, wa, gates)
    x1_p = _out_proj(merged_p, wo, xp, mods_p, rows_p)
    y_p = _ffn(x1_p, g_ffn[0], g_final, mods_p, rows_p, wu16, wd16)

    ns = bs * ts
    xs = x_sample.reshape(ns, d)
    hs = _norm_mod_call(xs, g_mix[0], mods_s, rows_s, 1, 0)
    pos_s = jnp.tile(past + jnp.arange(ts), bs)
    zc_s, gates_s, q_hm_s, qi_hm_s, ks32, ks16, vs32, vs16, kis32, kis16, wi_s = _in_proj(hs, w, pos_s, d)

    zc_t = zc_s.reshape(bs, ts, 3 * dc).transpose(1, 0, 2)
    st_t = state_conv[0].transpose(1, 0, 2)
    u_t, nst_t = _conv_sample(zc_t, st_t, conv_w[0])
    u_s = u_t.transpose(1, 0, 2).reshape(ns, dc)
    conv_s = nst_t.transpose(1, 0, 2)

    k_top_s = min(TOPK_MAX, (past + ts) // 4)
    pad_q = ((0, 0), (0, 0), (0, QPAD - ts), (0, 0))
    qi_s = jnp.pad(qi_hm_s.reshape(N_IDX_HEADS, bs, ts, IDX_DIM).transpose(1, 0, 2, 3), pad_q)
    qi_s = qi_s.reshape(bs, N_IDX_HEADS * QPAD, IDX_DIM)
    wi_sp = jnp.pad(wi_s.reshape(bs, ts, N_IDX_HEADS).transpose(0, 2, 1), ((0, 0), (0, 0), (0, QPAD - ts)))
    wi_sp = wi_sp.reshape(bs, N_IDX_HEADS * QPAD, 1)
    ki_new_t = jnp.pad(kis16.reshape(bs, ts, IDX_DIM).transpose(0, 2, 1), ((0, 0), (0, 0), (0, page - ts)))
    expand = jnp.repeat(jnp.eye(page, dtype=BF16), N_KV_HEADS, axis=1)
    kidx_t = jnp.swapaxes(cache_kidx.reshape(n_pool, page, IDX_DIM), 1, 2)
    bias_s = _sample_index(page_table, qi_s, wi_sp, ki_new_t, expand, kidx_t, ts, k_top_s)

    q_s = jnp.pad(q_hm_s.reshape(N_HEADS, bs, ts, HEAD_DIM).transpose(1, 0, 2, 3), pad_q)
    q_s = q_s.reshape(bs, N_HEADS * QPAD, HEAD_DIM)
    prow = page * N_KV_HEADS
    col_head = jnp.arange(prow) % N_KV_HEADS
    row_head = jnp.arange(N_HEADS * QPAD) // (KV_GROUP * QPAD)
    gmask = jnp.where(row_head[:, None] == col_head[None, :], 0.0, NEG_INF).astype(F32)
    pad_new = ((0, 0), (0, page - ts), (0, 0), (0, 0))
    k_new = jnp.pad(ks16.reshape(bs, ts, N_KV_HEADS, HEAD_DIM), pad_new).reshape(bs, prow, HEAD_DIM)
    v_new = jnp.pad(vs16.reshape(bs, ts, N_KV_HEADS, HEAD_DIM), pad_new).reshape(bs, prow, HEAD_DIM)
    ck = cache_k.reshape(n_pool, prow, HEAD_DIM)
    cv = cache_v.reshape(n_pool, prow, HEAD_DIM)
    attn_hq = _sample_attn(page_table, q_s, bias_s, gmask, k_new, v_new, ck, cv)
    attn_s = attn_hq.reshape(bs, N_HEADS, QPAD, HEAD_DIM)[:, :, :ts].transpose(0, 2, 1, 3)
    attn_s = attn_s.reshape(ns, D_ATTN).astype(BF16)

    merged_s = _merge(u_s, attn_s, wc, wa, gates_s)
    x1_s = _out_proj(merged_s, wo, xs, mods_s, rows_s)
    y_s = _ffn(x1_s, g_ffn[0], g_final, mods_s, rows_s, wu16, wd16)

    return (y_p.reshape(bp, sp, d),
            y_s.reshape(bs, ts, d),
            k32.reshape(1, bp, sp, N_KV_HEADS, HEAD_DIM),
            v32.reshape(1, bp, sp, N_KV_HEADS, HEAD_DIM),
            ki32.reshape(1, bp, sp, IDX_DIM),
            conv_p.reshape(1, bp, CONV_W - 1, dc),
            ks32.reshape(1, bs, ts, N_KV_HEADS, HEAD_DIM),
            vs32.reshape(1, bs, ts, N_KV_HEADS, HEAD_DIM),
            kis32.reshape(1, bs, ts, IDX_DIM),
            conv_s.reshape(1, bs, CONV_W - 1, dc))
```

```python
import functools

import jax
import jax.numpy as jnp
from jax import lax
from jax.experimental import pallas as pl
from jax.experimental.pallas import tpu as pltpu

N_HEADS = 16
HEAD_DIM = 128
N_KV_HEADS = 4
KV_GROUP = N_HEADS // N_KV_HEADS
N_IDX_HEADS = 16
IDX_DIM = 64
TOPK_MAX = 256
CONV_W = 3
ROPE_THETA = 10000.0
NORM_EPS = 1e-6
N_MOD = 6
D_ATTN = N_HEADS * HEAD_DIM
D_KV = N_KV_HEADS * HEAD_DIM
D_QI = N_IDX_HEADS * IDX_DIM

V7X_LANES = 128
V7X_VMEM_LIMIT_BYTES = 56 * 1024 * 1024

BF16 = jnp.bfloat16
F32 = jnp.float32
NEG_INF = float("-inf")
LOG2_E = 1.4426950408889634

BISECT_STEPS = 17


def _tile(n, target, mult=8):
    if n <= target:
        return n
    t = (target // mult) * mult
    while t >= mult:
        if n % t == 0:
            return t
        t -= mult
    return n


def _params(*sem):
    return pltpu.CompilerParams(dimension_semantics=sem, vmem_limit_bytes=V7X_VMEM_LIMIT_BYTES)


def _dot(a, b):
    return jnp.dot(a, b, preferred_element_type=F32)


def _dot_nt(a, b):
    return lax.dot_general(a, b, (((1,), (1,)), ((), ())), preferred_element_type=F32)


def _ada_kernel(c_ref, w_ref, b_ref, o_ref):
    c = c_ref[...]
    a = (c * jax.nn.sigmoid(c)).astype(BF16)
    o_ref[...] = _dot(a, w_ref[...].astype(BF16)) + b_ref[...]


def _ada(c, w, b):
    m, d = c.shape
    n = w.shape[1]
    tn = _tile(n, 512, V7X_LANES)
    return pl.pallas_call(
        _ada_kernel,
        grid=(n // tn,),
        in_specs=[pl.BlockSpec((m, d), lambda j: (0, 0)),
                  pl.BlockSpec((d, tn), lambda j: (0, j)),
                  pl.BlockSpec((1, tn), lambda j: (0, j))],
        out_specs=pl.BlockSpec((m, tn), lambda j: (0, j)),
        out_shape=jax.ShapeDtypeStruct((m, n), F32),
        compiler_params=_params("arbitrary"),
        name="ada",
    )(c, w, b.reshape(1, n))


class _Rows:
    def __init__(self, n_rows, rows_per_group, per_token):
        self.n = n_rows
        self.rows_per_group = rows_per_group
        self.per_token = per_token

    def mod_spec(self, tm, d, which, ncol_blocks_per_mod=1, tiled=False):
        tn = d // ncol_blocks_per_mod
        rpg = self.rows_per_group
        if self.per_token:
            shape = (None, tm, tn)
            where = lambda i: (0, i)
        else:
            shape = (None, 1, tn)
            where = lambda i: ((i * tm) // rpg, 0)
        if not tiled:
            return pl.BlockSpec(shape, lambda i, *_: (*where(i), which * ncol_blocks_per_mod))
        return pl.BlockSpec(shape, lambda i, j: (*where(i), which * ncol_blocks_per_mod + j))


def _norm_mod(x, g, sc, sh):
    ms = jnp.mean(x * x, axis=-1, keepdims=True)
    y = x * lax.rsqrt(ms + NORM_EPS) * g
    return y * (1.0 + sc) + sh


def _norm_mod_kernel(x_ref, g_ref, sc_ref, sh_ref, o_ref):
    o_ref[...] = _norm_mod(x_ref[...], g_ref[...], sc_ref[...], sh_ref[...]).astype(o_ref.dtype)


def _norm_mod_call(x, g, mods3, rows, which_sc, which_sh):
    n, d = x.shape
    tm = _tile(min(n, rows.rows_per_group if not rows.per_token else n), 256)
    return pl.pallas_call(
        _norm_mod_kernel,
        grid=(n // tm,),
        in_specs=[pl.BlockSpec((tm, d), lambda i: (i, 0)),
                  pl.BlockSpec((1, d), lambda i: (0, 0)),
                  rows.mod_spec(tm, d, which_sc),
                  rows.mod_spec(tm, d, which_sh)],
        out_specs=pl.BlockSpec((tm, d), lambda i: (i, 0)),
        out_shape=jax.ShapeDtypeStruct((n, d), BF16),
        compiler_params=_params("parallel"),
        name="norm_mod",
    )(x, g.reshape(1, d), mods3, mods3)


BF16_SUBLANE_TILE = 16


def _rows_spec(rows, k, row0_of):
    def index(i, j):
        row0 = row0_of(j)
        return (row0 if isinstance(row0, int) else pl.multiple_of(row0, BF16_SUBLANE_TILE), 0)

    return pl.BlockSpec((pl.Element(rows), pl.Element(k)), index)


def _proj_kernel(x_ref, w_ref, *rest, epilogue, n_extra, w_t):
    dot = _dot_nt if w_t else _dot
    epilogue(dot(x_ref[...], w_ref[...]), rest[:n_extra], rest[n_extra:])


def _proj(x, w, col0, m, *, tm, tn, epilogue, extra, extra_specs, out_shape, out_specs, name, w_t=False):
    n, k = x.shape
    assert n % tm == 0 and m % tn == 0 and w.dtype == BF16
    if w_t:
        assert col0 % BF16_SUBLANE_TILE == 0 and w.shape[1] == k
        w_spec = _rows_spec(tn, k, lambda j: col0 + j * tn)
    else:
        assert col0 % tn == 0 and w.shape[0] == k
        w_spec = pl.BlockSpec((k, tn), lambda i, j: (0, col0 // tn + j))
    return pl.pallas_call(
        functools.partial(_proj_kernel, epilogue=epilogue, n_extra=len(extra), w_t=w_t),
        grid=(n // tm, m // tn),
        in_specs=[pl.BlockSpec((tm, k), lambda i, j: (i, 0)), w_spec] + list(extra_specs),
        out_specs=out_specs,
        out_shape=out_shape,
        compiler_params=_params("parallel", "arbitrary"),
        name=name,
    )(x, w, *extra)


PROJ_TM = 1024
PROJ_TN = 1024


def _proj_tn(m):
    return _tile(m, PROJ_TN, V7X_LANES)


def _cast_epilogue(acc, extra, outs):
    outs[0][...] = acc.astype(outs[0].dtype)


def _mm(x, w, col0, m, out_dtype, name, w_t=False):
    n = x.shape[0]
    tm, tn = _tile(n, PROJ_TM), _proj_tn(m)
    return _proj(x, w, col0, m, tm=tm, tn=tn, epilogue=_cast_epilogue, extra=(), extra_specs=(),
                 out_shape=jax.ShapeDtypeStruct((n, m), out_dtype),
                 out_specs=pl.BlockSpec((tm, tn), lambda i, j: (i, j)), name=name, w_t=w_t)


def _swap_halves(x, head_w):
    if head_w == V7X_LANES:
        return pltpu.roll(x, V7X_LANES // 2, axis=1)
    half = head_w // 2
    lane = lax.broadcasted_iota(jnp.int32, x.shape, 1)
    take_upper = (lane & (head_w - 1)) < half
    return jnp.where(take_upper, pltpu.roll(x, V7X_LANES - half, axis=1), pltpu.roll(x, half, axis=1))


def _rope_cols(acc, cos, sin, head_w):
    out = []
    for c in range(acc.shape[1] // V7X_LANES):
        x = acc[:, c * V7X_LANES:(c + 1) * V7X_LANES]
        out.append(x * cos + _swap_halves(x, head_w) * sin)
    return out


def _rope_q_epilogue(acc, extra, outs, *, head_w):
    cos_ref, sin_ref = extra
    o_ref = outs[0]
    cols = _rope_cols(acc, cos_ref[...], sin_ref[...], head_w)
    per = V7X_LANES // head_w
    for c, x in enumerate(cols):
        for p in range(per):
            o_ref[c * per + p] = x[:, p * head_w:(p + 1) * head_w].astype(o_ref.dtype)


def _mm_rope_q(x, w, col0, m, cos, sin, head_w, name, w_t=False):
    n = x.shape[0]
    tm, tn = _tile(min(n, cos.shape[0]), PROJ_TM), _proj_tn(m)
    hpt = tn // head_w
    table = _table_spec(cos, tm)
    return _proj(x, w, col0, m, tm=tm, tn=tn, epilogue=functools.partial(_rope_q_epilogue, head_w=head_w),
                 extra=(cos, sin), extra_specs=(table, table),
                 out_shape=jax.ShapeDtypeStruct((m // head_w, n, head_w), BF16),
                 out_specs=pl.BlockSpec((hpt, tm, head_w), lambda i, j: (j, i, 0)), name=name, w_t=w_t)


def _table_spec(table, tm):
    period = table.shape[0] // tm
    assert period * tm == table.shape[0]
    return pl.BlockSpec((tm, V7X_LANES), lambda i, j: (i % period, 0))


KVX_TM = 512


def _kvx_kernel(h_ref, wkv_ref, wx_ref, cos_ref, sin_ref, cos_x_ref, sin_x_ref,
                k32_ref, k16_ref, v32_ref, v16_ref, x32_ref, x16_ref):
    h = h_ref[...]
    kv = _dot_nt(h, wkv_ref[...])
    tm = kv.shape[0]
    for c, k in enumerate(_rope_cols(kv[:, :D_KV], cos_ref[...], sin_ref[...], HEAD_DIM)):
        k32_ref[pl.ds(c, tm, stride=N_KV_HEADS), :] = k
        k16_ref[:, c * V7X_LANES:(c + 1) * V7X_LANES] = k.astype(BF16)
    v = kv[:, D_KV:]
    for c in range(N_KV_HEADS):
        v32_ref[pl.ds(c, tm, stride=N_KV_HEADS), :] = v[:, c * HEAD_DIM:(c + 1) * HEAD_DIM]
    v16_ref[...] = v.astype(BF16)
    (x,) = _rope_cols(_dot_nt(h, wx_ref[...]), cos_x_ref[...], sin_x_ref[...], IDX_DIM)
    x32_ref[...] = x
    x16_ref[...] = x.astype(BF16)


def _mm_kvx(x, w_t, row_kv, row_x, cos, sin, cos_x, sin_x):
    n, kdim = x.shape
    assert row_x + V7X_LANES <= w_t.shape[0]
    tm = _tile(min(n, cos.shape[0]), KVX_TM)
    table = _table_spec(cos, tm)
    kv32 = (N_KV_HEADS, HEAD_DIM)
    layouts = [kv32, (1, D_KV), kv32, (1, D_KV), (1, V7X_LANES), (1, V7X_LANES)]
    dtypes = [F32, BF16] * 3
    shapes = [jax.ShapeDtypeStruct((n * r, wd), dt) for (r, wd), dt in zip(layouts, dtypes)]
    specs = [pl.BlockSpec((tm * r, wd), lambda i, j: (i, 0)) for r, wd in layouts]
    return pl.pallas_call(
        _kvx_kernel,
        grid=(n // tm, 1),
        in_specs=[pl.BlockSpec((tm, kdim), lambda i, j: (i, 0)),
                  _rows_spec(2 * D_KV, kdim, lambda j: row_kv),
                  _rows_spec(V7X_LANES, kdim, lambda j: row_x)] + [table] * 4,
        out_specs=specs,
        out_shape=shapes,
        compiler_params=_params("parallel", "arbitrary"),
        name="in_kvx",
    )(x, w_t, w_t, cos, sin, cos_x, sin_x)


PREV_ROWS = 16


def _conv_prompt_kernel(xin_ref, bg_ref, cg_ref, xin_prev_ref, cg_prev_ref, w_ref, u_ref, st_ref,
                        *, blocks_per_seq):
    i = pl.program_id(1)
    p = cg_ref[...].astype(F32) * xin_ref[...].astype(F32)
    prev = cg_prev_ref[...].astype(F32) * xin_prev_ref[...].astype(F32)
    prev = jnp.where(i % blocks_per_seq == 0, 0.0, prev)
    tm = p.shape[0]
    last = PREV_ROWS - 1
    row = lax.broadcasted_iota(jnp.int32, p.shape, 0)
    p1 = pltpu.roll(p, 1, axis=0)
    p1 = jnp.where(row == 0, prev[last:last + 1], p1)
    p2 = pltpu.roll(p, 2, axis=0)
    p2 = jnp.where(row == 0, prev[last - 1:last], jnp.where(row == 1, prev[last:last + 1], p2))
    w = w_ref[...]
    y = p2 * w[0:1] + p1 * w[1:2] + p * w[2:3]
    u_ref[...] = (bg_ref[...].astype(F32) * y).astype(u_ref.dtype)
    st_ref[...] = p[tm - (CONV_W - 1):, :]


def _conv_prompt(zc, conv_w, n_seq, seq):
    n = zc.shape[0]
    dc = zc.shape[1] // 3
    tm = _tile(seq, 512)
    tc = _tile(dc, 512, V7X_LANES)
    ncb = dc // tc
    bps = seq // tm
    prev_row_blk = lambda i: jnp.maximum(i * (tm // PREV_ROWS) - 1, 0)
    u, st = pl.pallas_call(
        functools.partial(_conv_prompt_kernel, blocks_per_seq=bps),
        grid=(ncb, n // tm),
        in_specs=[pl.BlockSpec((tm, tc), lambda j, i: (i, j)),
                  pl.BlockSpec((tm, tc), lambda j, i: (i, ncb + j)),
                  pl.BlockSpec((tm, tc), lambda j, i: (i, 2 * ncb + j)),
                  pl.BlockSpec((PREV_ROWS, tc), lambda j, i: (prev_row_blk(i), j)),
                  pl.BlockSpec((PREV_ROWS, tc), lambda j, i: (prev_row_blk(i), 2 * ncb + j)),
                  pl.BlockSpec((CONV_W, tc), lambda j, i: (0, j))],
        out_specs=[pl.BlockSpec((tm, tc), lambda j, i: (i, j)),
                   pl.BlockSpec((None, CONV_W - 1, tc), lambda j, i: (i // bps, 0, j))],
        out_shape=[jax.ShapeDtypeStruct((n, dc), BF16),
                   jax.ShapeDtypeStruct((n_seq, CONV_W - 1, dc), F32)],
        compiler_params=_params("parallel", "arbitrary"),
        name="conv_prompt",
    )(zc, zc, zc, zc, zc, conv_w)
    return u, st


def _conv_sample_kernel(xin_ref, bg_ref, cg_ref, st_ref, w_ref, u_ref, nst_ref):
    t_len = xin_ref.shape[0]
    w = w_ref[...]
    ext = [st_ref[j] for j in range(CONV_W - 1)]
    ext += [cg_ref[t].astype(F32) * xin_ref[t].astype(F32) for t in range(t_len)]
    for t in range(t_len):
        y = ext[t] * w[0:1]
        for j in range(1, CONV_W):
            y = y + ext[t + j] * w[j:j + 1]
        u_ref[t] = (bg_ref[t].astype(F32) * y).astype(u_ref.dtype)
    for j in range(CONV_W - 1):
        nst_ref[j] = ext[t_len + j]


def _conv_sample(zc_t, st_t, conv_w):
    t_len, b, dc3 = zc_t.shape
    dc = dc3 // 3
    tc = _tile(dc, 512, V7X_LANES)
    ncb = dc // tc
    return pl.pallas_call(
        _conv_sample_kernel,
        grid=(ncb,),
        in_specs=[pl.BlockSpec((t_len, b, tc), lambda j: (0, 0, j)),
                  pl.BlockSpec((t_len, b, tc), lambda j: (0, 0, ncb + j)),
                  pl.BlockSpec((t_len, b, tc), lambda j: (0, 0, 2 * ncb + j)),
                  pl.BlockSpec((CONV_W - 1, b, tc), lambda j: (0, 0, j)),
                  pl.BlockSpec((CONV_W, tc), lambda j: (0, j))],
        out_specs=[pl.BlockSpec((t_len, b, tc), lambda j: (0, 0, j)),
                   pl.BlockSpec((CONV_W - 1, b, tc), lambda j: (0, 0, j))],
        out_shape=[jax.ShapeDtypeStruct((t_len, b, dc), BF16),
                   jax.ShapeDtypeStruct((CONV_W - 1, b, dc), F32)],
        compiler_params=_params("parallel"),
        name="conv_sample",
    )(zc_t, zc_t, zc_t, st_t, conv_w)


_SUBLANES = 8
_REDUCERS = {jnp.add: jnp.sum, jnp.maximum: jnp.max, jnp.minimum: jnp.min}


def _fold_chunks(n_chunks, term, combine, init, shape, axis):
    if axis == 0 and shape[0] > _SUBLANES:
        full_term = term
        shape = (_SUBLANES, shape[1])
        term = lambda c: _REDUCERS[combine](full_term(c).reshape(-1, *shape), axis=0)
    if isinstance(n_chunks, int):
        vals = [term(c) for c in range(n_chunks)]
        while len(vals) > 1:
            vals = [combine(vals[i], vals[i + 1]) if i + 1 < len(vals) else vals[i]
                    for i in range(0, len(vals), 2)]
        return vals[0]
    return lax.fori_loop(0, n_chunks, lambda c, acc: combine(acc, term(c)), jnp.full(shape, init, F32))


def _topk_threshold(score_at, index_at, n_chunks, n_keys_max, want, shape, axis):
    fold = functools.partial(_fold_chunks, n_chunks, shape=shape, axis=axis)

    def count_ge(thr):
        part = fold(lambda c: jnp.where(score_at(c) >= thr, 1.0, 0.0), jnp.add, 0.0)
        return jnp.sum(part, axis=axis, keepdims=True)

    hi = jnp.max(fold(score_at, jnp.maximum, NEG_INF), axis=axis, keepdims=True)
    lo = jnp.min(fold(lambda c: jnp.where(score_at(c) == NEG_INF, jnp.inf, score_at(c)), jnp.minimum, jnp.inf),
                 axis=axis, keepdims=True)
    cnt_lo = count_ge(lo)
    cnt_hi = count_ge(hi)
    top_ok = cnt_hi >= want
    lo = jnp.where(top_ok, hi, lo)
    cnt_lo = jnp.where(top_ok, cnt_hi, cnt_lo)

    def bisect(_, carry):
        lo, hi, cnt_lo = carry
        mid = lo + (hi - lo) * 0.5
        cnt = count_ge(mid)
        ok = cnt >= want
        return (jnp.where(ok, mid, lo), jnp.where(ok, hi, mid), jnp.where(ok, cnt, cnt_lo))

    lo, hi, cnt_lo = lax.fori_loop(0, BISECT_STEPS, bisect, (lo, hi, cnt_lo))

    def pending_of(cnt_lo, done):
        return jnp.max(jnp.where((cnt_lo > want) & (done == 0.0), 1.0, 0.0))

    def finish(carry):
        lo, hi, cnt_lo, done, _ = carry
        cand = jnp.max(fold(lambda c: jnp.where(score_at(c) < hi, score_at(c), NEG_INF), jnp.maximum, NEG_INF),
                       axis=axis, keepdims=True)
        cnt = count_ge(cand)
        active = (cnt_lo > want) & (done == 0.0)
        ok = active & (cnt >= want)
        lo = jnp.where(ok, cand, lo)
        cnt_lo = jnp.where(ok, cnt, cnt_lo)
        done = jnp.where(ok, 1.0, done)
        hi = jnp.where(active & (cnt < want), cand, hi)
        return lo, hi, cnt_lo, done, pending_of(cnt_lo, done)

    done0 = jnp.where(lo == hi, 1.0, 0.0)
    lo, _, cnt_lo, *_ = lax.while_loop(lambda carry: carry[4] > 0.0, finish,
                                       (lo, hi, cnt_lo, done0, pending_of(cnt_lo, done0)))

    excess = cnt_lo - want
    tied = jnp.max(jnp.where(excess > 0.0, 1.0, 0.0))

    def count_gt(carry):
        part = fold(lambda c: jnp.where(score_at(c) > lo, 1.0, 0.0), jnp.add, 0.0)
        return jnp.sum(part, axis=axis, keepdims=True), 0.0

    cnt_gt, _ = lax.while_loop(lambda carry: carry[1] > 0.0, count_gt, (jnp.zeros_like(lo), tied))
    need = want - cnt_gt

    def tie_step(carry):
        lo_j, hi_j, left = carry
        mid = jnp.floor((lo_j + hi_j) * 0.5)
        part = fold(lambda c: jnp.where((score_at(c) == lo) & (index_at(c) <= mid), 1.0, 0.0), jnp.add, 0.0)
        ok = jnp.sum(part, axis=axis, keepdims=True) >= need
        return jnp.where(ok, lo_j, mid), jnp.where(ok, mid, hi_j), left - 1.0

    steps = float(max(1, (n_keys_max - 1).bit_length() + 1))
    _, hi_j, _ = lax.while_loop(lambda carry: carry[2] > 0.0, tie_step,
                                (jnp.full_like(lo, -1.0), jnp.full_like(lo, n_keys_max - 1.0), tied * steps))
    tie_j = jnp.where(excess > 0.0, hi_j, jnp.inf)
    return lo, tie_j


def _selected(score, index, lo, tie_j):
    return (score > lo) | ((score == lo) & (index <= tie_j))


ATTN_TQ = 256
ATTN_KC = 256


def _attn_prompt_kernel(qi_ref, wi_ref, ki_ref, q_ref, k_ref, vt_ref, o_ref,
                        score_ref, m_ref, l_ref, acc_ref, *, tq, kc, k_top):
    qb = pl.program_id(1)
    n_chunks = (qb * tq) // kc + tq // kc
    qpos = qb * tq + lax.broadcasted_iota(jnp.int32, (1, tq), 1)
    wi = wi_ref[...]

    def fill(c, _):
        kic = ki_ref[pl.ds(pl.multiple_of(c * kc, kc), kc), :]
        acc = jnp.zeros((kc, tq), F32)
        for h in range(N_IDX_HEADS):
            acc = acc + jnp.maximum(_dot_nt(kic, qi_ref[h]), 0.0) * wi[h:h + 1]
        kpos = c * kc + lax.broadcasted_iota(jnp.int32, (kc, tq), 0)
        score_ref[c] = jnp.where(kpos <= qpos, acc, NEG_INF)
        return 0

    lax.fori_loop(0, n_chunks, fill, 0)

    want = jnp.minimum(qpos + 1, k_top).astype(F32)
    def key_index(c):
        return (c * kc + lax.broadcasted_iota(jnp.int32, (kc, tq), 0)).astype(F32)

    n_keys_max = score_ref.shape[0] * kc
    thr, tie_j = _topk_threshold(lambda c: score_ref[c], key_index, n_chunks, n_keys_max, want, (kc, tq), axis=0)

    def to_bias(c, _):
        score_ref[c] = jnp.where(_selected(score_ref[c], key_index(c), thr, tie_j), 0.0, NEG_INF)
        return 0

    lax.fori_loop(0, n_chunks, to_bias, 0)

    scale = HEAD_DIM ** -0.5 * LOG2_E
    for g in range(N_KV_HEADS):
        m_ref[...] = jnp.full(m_ref.shape, NEG_INF, F32)
        l_ref[...] = jnp.zeros(l_ref.shape, F32)
        acc_ref[...] = jnp.zeros(acc_ref.shape, F32)

        heads = range(KV_GROUP)

        def scores(c):
            kch = k_ref[pl.ds(pl.multiple_of(c * kc, kc), kc), g * HEAD_DIM:(g + 1) * HEAD_DIM]
            bias = score_ref[c]
            return [_dot_nt(kch, q_ref[g * KV_GROUP + r]) * scale + bias for r in heads]

        def update(c, s):
            vt = vt_ref[c, g * HEAD_DIM:(g + 1) * HEAD_DIM, :]
            m_old = [m_ref[r] for r in heads]
            l_old = [l_ref[r] for r in heads]
            m_new = [jnp.maximum(m_old[r], jnp.max(s[r], axis=0, keepdims=True)) for r in heads]
            m_safe = [jnp.where(m_new[r] == NEG_INF, 0.0, m_new[r]) for r in heads]
            p = [jnp.exp2(s[r] - m_safe[r]) for r in heads]
            alpha = [jnp.exp2(m_old[r] - m_safe[r]) for r in heads]
            pv = [_dot(vt, p[r].astype(BF16)) for r in heads]
            for r in heads:
                l_ref[r] = alpha[r] * l_old[r] + jnp.sum(p[r], axis=0, keepdims=True)
                acc_ref[r] = alpha[r] * acc_ref[r] + pv[r]
                m_ref[r] = m_new[r]

        def attend_pair(i, _):
            sa, sb = scores(2 * i), scores(2 * i + 1)
            update(2 * i, sa)
            update(2 * i + 1, sb)
            return 0

        lax.fori_loop(0, n_chunks // 2, attend_pair, 0)

        @pl.when(n_chunks % 2 == 1)
        def _():
            update(n_chunks - 1, scores(n_chunks - 1))
        for r in range(KV_GROUP):
            h = g * KV_GROUP + r
            out = acc_ref[r] / l_ref[r]
            o_ref[:, h * HEAD_DIM:(h + 1) * HEAD_DIM] = out.T.astype(o_ref.dtype)


def _attn_prompt(qi_hm, wi_t, ki16, q_hm, k16, vt16, n_seq, seq, k_top):
    n = n_seq * seq
    tq = _tile(seq, ATTN_TQ)
    kc = _tile(tq, ATTN_KC)
    nqb = seq // tq
    assert vt16.shape == (n_seq, seq // kc, D_KV, kc)
    kernel = functools.partial(_attn_prompt_kernel, tq=tq, kc=kc, k_top=k_top)
    return pl.pallas_call(
        kernel,
        grid=(n_seq, nqb),
        in_specs=[pl.BlockSpec((N_IDX_HEADS, tq, IDX_DIM), lambda b, q: (0, b * nqb + q, 0)),
                  pl.BlockSpec((N_IDX_HEADS, tq), lambda b, q: (0, b * nqb + q)),
                  pl.BlockSpec((seq, IDX_DIM), lambda b, q: (b, 0)),
                  pl.BlockSpec((N_HEADS, tq, HEAD_DIM), lambda b, q: (0, b * nqb + q, 0)),
                  pl.BlockSpec((seq, D_KV), lambda b, q: (b, 0)),
                  pl.BlockSpec((None, seq // kc, D_KV, kc), lambda b, q: (b, 0, 0, 0))],
        out_specs=pl.BlockSpec((tq, D_ATTN), lambda b, q: (b * nqb + q, 0)),
        out_shape=jax.ShapeDtypeStruct((n, D_ATTN), BF16),
        scratch_shapes=[pltpu.VMEM((seq // kc, kc, tq), F32),
                        pltpu.VMEM((KV_GROUP, 1, tq), F32),
                        pltpu.VMEM((KV_GROUP, 1, tq), F32),
                        pltpu.VMEM((KV_GROUP, HEAD_DIM, tq), F32)],
        compiler_params=_params("parallel", "arbitrary"),
        name="attn_prompt",
    )(qi_hm, wi_t, ki16, q_hm, k16, vt16)


def _merge_kernel(u_ref, a_ref, wc_ref, wa_ref, gc_ref, ga_ref, o_ref):
    yc = _dot(u_ref[...], wc_ref[...])
    ya = _dot(a_ref[...], wa_ref[...])
    gc = jax.nn.sigmoid(gc_ref[...].astype(F32))
    ga = jax.nn.sigmoid(ga_ref[...].astype(F32))
    o_ref[...] = (gc * yc + ga * ya).astype(o_ref.dtype)


def _merge(u, attn, wc, wa, gates):
    n, dc = u.shape
    da = attn.shape[1]
    d = wc.shape[1]
    tm = _tile(n, PROJ_TM)
    tn = _proj_tn(d)
    ncb = d // tn
    return pl.pallas_call(
        _merge_kernel,
        grid=(n // tm, ncb),
        in_specs=[pl.BlockSpec((tm, dc), lambda i, j: (i, 0)),
                  pl.BlockSpec((tm, da), lambda i, j: (i, 0)),
                  pl.BlockSpec((dc, tn), lambda i, j: (0, j)),
                  pl.BlockSpec((da, tn), lambda i, j: (0, j)),
                  pl.BlockSpec((tm, tn), lambda i, j: (i, j)),
                  pl.BlockSpec((tm, tn), lambda i, j: (i, ncb + j))],
        out_specs=pl.BlockSpec((tm, tn), lambda i, j: (i, j)),
        out_shape=jax.ShapeDtypeStruct((n, d), BF16),
        compiler_params=_params("parallel", "arbitrary"),
        name="merge",
    )(u, attn, wc, wa, gates, gates)


def _out_proj_epilogue(acc, extra, outs):
    x_ref, gt_ref = extra
    outs[0][...] = x_ref[...] + gt_ref[...] * acc


def _out_proj(a, w, x, mods3, rows):
    n = a.shape[0]
    d = w.shape[1]
    tm = _tile(min(n, rows.rows_per_group if not rows.per_token else n), PROJ_TM)
    tn = _proj_tn(d)
    tile = pl.BlockSpec((tm, tn), lambda i, j: (i, j))
    gate = rows.mod_spec(tm, d, 2, d // tn, tiled=True)
    return _proj(a, w, 0, d, tm=tm, tn=tn, epilogue=_out_proj_epilogue, extra=(x, mods3), extra_specs=(tile, gate),
                 out_shape=jax.ShapeDtypeStruct((n, d), F32), out_specs=tile, name="out_proj")


def _ffn_kernel(x_ref, g_ref, sc_ref, sh_ref, gt_ref, gf_ref, wu_ref, wd_ref, o_ref, h_ref):
    f = pl.program_id(1)

    @pl.when(f == 0)
    def _():
        h_ref[...] = _norm_mod(x_ref[...], g_ref[...], sc_ref[...], sh_ref[...]).astype(BF16)
        o_ref[...] = jnp.zeros(o_ref.shape, F32)

    u = jnp.maximum(_dot(h_ref[...], wu_ref[...]), 0.0)
    o_ref[...] += _dot((u * u).astype(BF16), wd_ref[...])

    @pl.when(f == pl.num_programs(1) - 1)
    def _():
        x2 = x_ref[...] + gt_ref[...] * o_ref[...]
        ms = jnp.mean(x2 * x2, axis=-1, keepdims=True)
        o_ref[...] = x2 * lax.rsqrt(ms + NORM_EPS) * gf_ref[...]


def _ffn(x1, g_ffn, g_final, mods3, rows, w_up, w_down):
    n, d = x1.shape
    dff = w_up.shape[1]
    tm = _tile(min(n, rows.rows_per_group if not rows.per_token else n), 512)
    tf = _tile(dff, 512, V7X_LANES)
    return pl.pallas_call(
        _ffn_kernel,
        grid=(n // tm, dff // tf),
        in_specs=[pl.BlockSpec((tm, d), lambda i, f: (i, 0), pipeline_mode=pl.Buffered(1)),
                  pl.BlockSpec((1, d), lambda i, f: (0, 0)),
                  rows.mod_spec(tm, d, 4),
                  rows.mod_spec(tm, d, 3),
                  rows.mod_spec(tm, d, 5),
                  pl.BlockSpec((1, d), lambda i, f: (0, 0)),
                  pl.BlockSpec((d, tf), lambda i, f: (0, f)),
                  pl.BlockSpec((tf, d), lambda i, f: (f, 0))],
        out_specs=pl.BlockSpec((tm, d), lambda i, f: (i, 0)),
        out_shape=jax.ShapeDtypeStruct((n, d), F32),
        scratch_shapes=[pltpu.VMEM((tm, d), BF16)],
        compiler_params=_params("parallel", "arbitrary"),
        name="ffn",
    )(x1, g_ffn.reshape(1, d), mods3, mods3, mods3, g_final.reshape(1, d), w_up, w_down)


QPAD = 8
INDEX_PAGES_PER_DOT = 8


def _sample_index_kernel(pt_ref, qi_ref, wi_ref, kin_ref, expand_ref, cache_ref, bias_ref,
                         kbuf_ref, sem_ref, score_ref, *, n_pages, page, t_len, k_top):
    b = pl.program_id(0)

    slot = b % 2

    def page_copy(bb, p):
        return pltpu.make_async_copy(cache_ref.at[pt_ref[bb, p]], kbuf_ref.at[bb % 2, p], sem_ref.at[bb % 2, p])

    @pl.when(b == 0)
    def _():
        for p in range(n_pages):
            page_copy(0, p).start()

    @pl.when(b + 1 < pl.num_programs(0))
    def _():
        for p in range(n_pages):
            page_copy(b + 1, p).start()

    qi = qi_ref[...]
    wi = wi_ref[...].reshape(N_IDX_HEADS, QPAD, 1)

    def head_sum(s):
        s3 = jnp.maximum(s, 0.0).reshape(N_IDX_HEADS, QPAD, s.shape[1]) * wi
        return jnp.sum(s3, axis=0)

    ppd = INDEX_PAGES_PER_DOT if n_pages % INDEX_PAGES_PER_DOT == 0 else 1
    for p0 in range(0, n_pages, ppd):
        for p in range(p0, p0 + ppd):
            page_copy(b, p).wait()
        kt = jnp.concatenate([kbuf_ref[slot, p] for p in range(p0, p0 + ppd)], axis=1).astype(BF16)
        sc = head_sum(_dot(qi, kt))
        for j in range(ppd):
            score_ref[p0 + j] = sc[:, j * page:(j + 1) * page]

    qrow = lax.broadcasted_iota(jnp.int32, (QPAD, page), 0)
    lane = lax.broadcasted_iota(jnp.int32, (QPAD, page), 1)
    s_new = head_sum(_dot(qi, kin_ref[...]))
    score_ref[n_pages] = jnp.where((lane <= qrow) & (lane < t_len), s_new, NEG_INF)

    n_chunks = n_pages + 1
    lane_f = lane.astype(F32)
    key_index = lambda c: lane_f + float(c * page)
    is_query = lax.broadcasted_iota(jnp.int32, (QPAD, 1), 0) < t_len
    want = jnp.where(is_query, float(k_top), float(n_pages * page + t_len))
    thr, tie_j = _topk_threshold(lambda c: score_ref[c], key_index, n_chunks, n_chunks * page, want,
                                 (QPAD, page), axis=1)

    sel = [jnp.where(_selected(score_ref[c], key_index(c), thr, tie_j), 1.0, 0.0) for c in range(n_chunks)]
    sel += [jnp.zeros((QPAD, page), F32)] * (n_chunks % 2)
    hit = _dot(jnp.concatenate(sel, axis=0).astype(BF16), expand_ref[...])
    bias = jnp.where(hit > 0.5, 0.0, NEG_INF)
    for c in range(n_chunks):
        bias_ref[c] = bias[c * QPAD:(c + 1) * QPAD]


def _sample_index(page_table, qi_s, wi_s, ki_new_t, expand, cache_kidx_t, t_len, k_top):
    bsz, n_pages = page_table.shape
    page = cache_kidx_t.shape[2]
    kernel = functools.partial(_sample_index_kernel, n_pages=n_pages, page=page, t_len=t_len, k_top=k_top)
    grid_spec = pltpu.PrefetchScalarGridSpec(
        num_scalar_prefetch=1,
        grid=(bsz,),
        in_specs=[pl.BlockSpec((None, N_IDX_HEADS * QPAD, IDX_DIM), lambda b, pt: (b, 0, 0)),
                  pl.BlockSpec((None, N_IDX_HEADS * QPAD, 1), lambda b, pt: (b, 0, 0)),
                  pl.BlockSpec((None, IDX_DIM, page), lambda b, pt: (b, 0, 0)),
                  pl.BlockSpec(expand.shape, lambda b, pt: (0, 0)),
                  pl.BlockSpec(memory_space=pl.ANY)],
        out_specs=pl.BlockSpec((None, n_pages + 1, QPAD, N_KV_HEADS * page), lambda b, pt: (b, 0, 0, 0)),
        scratch_shapes=[pltpu.VMEM((2, n_pages, IDX_DIM, page), F32),
                        pltpu.SemaphoreType.DMA((2, n_pages)),
                        pltpu.VMEM((n_pages + 1, QPAD, page), F32)],
    )
    return pl.pallas_call(
        kernel,
        grid_spec=grid_spec,
        out_shape=jax.ShapeDtypeStruct((bsz, n_pages + 1, QPAD, N_KV_HEADS * page), F32),
        compiler_params=_params("arbitrary"),
        name="sample_index",
    )(page_table, qi_s, wi_s, ki_new_t, expand, cache_kidx_t)


PAGES_PER_CHUNK = 8
KV_SLOTS = 3


def _sample_attn_kernel(pt_ref, q_ref, bias_ref, gmask_ref, kn_ref, vn_ref, ck_ref, cv_ref, o_ref,
                        kbuf_ref, vbuf_ref, sem_ref, m_ref, l_ref, acc_ref, *, n_pages, ppc):
    b = pl.program_id(0)
    n_chunks = n_pages // ppc
    rows = N_HEADS * QPAD
    prow = kbuf_ref.shape[2]

    n_total = pl.num_programs(0) * n_chunks

    def copies(g):
        bb, c, slot = g // n_chunks, g % n_chunks, g % KV_SLOTS
        out = []
        for j in range(ppc):
            pid = pt_ref[bb, c * ppc + j]
            out.append(pltpu.make_async_copy(ck_ref.at[pid], kbuf_ref.at[slot, j], sem_ref.at[0, slot, j]))
            out.append(pltpu.make_async_copy(cv_ref.at[pid], vbuf_ref.at[slot, j], sem_ref.at[1, slot, j]))
        return out

    def start(g):
        @pl.when(g < n_total)
        def _():
            for cp in copies(g):
                cp.start()

    @pl.when(b == 0)
    def _():
        for g in range(KV_SLOTS - 1):
            start(g)

    q = q_ref[...]
    gmask = gmask_ref[...]
    m_ref[...] = jnp.full(m_ref.shape, NEG_INF, F32)
    l_ref[...] = jnp.zeros(l_ref.shape, F32)
    acc_ref[...] = jnp.zeros(acc_ref.shape, F32)
    scale = HEAD_DIM ** -0.5 * LOG2_E

    def attend(kch, vch, biases):
        s = _dot_nt(q, kch) * scale
        parts = []
        for j, bias in enumerate(biases):
            sj = s[:, j * prow:(j + 1) * prow].reshape(N_HEADS, QPAD, prow) + bias[None]
            parts.append(sj.reshape(rows, prow) + gmask)
        s = parts[0] if len(parts) == 1 else jnp.concatenate(parts, axis=1)
        m_old = m_ref[...]
        m_new = jnp.maximum(m_old, jnp.max(s, axis=1, keepdims=True))
        m_safe = jnp.where(m_new == NEG_INF, 0.0, m_new)
        p = jnp.exp2(s - m_safe)
        alpha = jnp.exp2(m_old - m_safe)
        l_ref[...] = alpha * l_ref[...] + jnp.sum(p, axis=1, keepdims=True)
        acc_ref[...] = alpha * acc_ref[...] + _dot(p.astype(BF16), vch)
        m_ref[...] = m_new

    def chunk_body(c, _):
        g = b * n_chunks + c
        slot = g % KV_SLOTS
        start(g + KV_SLOTS - 1)
        for cp in copies(g):
            cp.wait()
        kch = kbuf_ref[slot].reshape(ppc * prow, HEAD_DIM).astype(BF16)
        vch = vbuf_ref[slot].reshape(ppc * prow, HEAD_DIM).astype(BF16)
        attend(kch, vch, [bias_ref[c * ppc + j] for j in range(ppc)])
        return 0

    lax.fori_loop(0, n_chunks, chunk_body, 0)
    attend(kn_ref[...], vn_ref[...], [bias_ref[n_pages]])
    o_ref[...] = acc_ref[...] / l_ref[...]


def _sample_attn(page_table, q_s, bias, gmask, k_new, v_new, cache_k, cache_v):
    bsz, n_pages = page_table.shape
    prow = cache_k.shape[1]
    ppc = PAGES_PER_CHUNK if n_pages % PAGES_PER_CHUNK == 0 else 1
    rows = N_HEADS * QPAD
    kernel = functools.partial(_sample_attn_kernel, n_pages=n_pages, ppc=ppc)
    grid_spec = pltpu.PrefetchScalarGridSpec(
        num_scalar_prefetch=1,
        grid=(bsz,),
        in_specs=[pl.BlockSpec((None, rows, HEAD_DIM), lambda b, pt: (b, 0, 0)),
                  pl.BlockSpec((None, n_pages + 1, QPAD, prow), lambda b, pt: (b, 0, 0, 0)),
                  pl.BlockSpec((rows, prow), lambda b, pt: (0, 0)),
                  pl.BlockSpec((None, prow, HEAD_DIM), lambda b, pt: (b, 0, 0)),
                  pl.BlockSpec((None, prow, HEAD_DIM), lambda b, pt: (b, 0, 0)),
                  pl.BlockSpec(memory_space=pl.ANY),
                  pl.BlockSpec(memory_space=pl.ANY)],
        out_specs=pl.BlockSpec((None, rows, HEAD_DIM), lambda b, pt: (b, 0, 0)),
        scratch_shapes=[pltpu.VMEM((KV_SLOTS, ppc, prow, HEAD_DIM), F32),
                        pltpu.VMEM((KV_SLOTS, ppc, prow, HEAD_DIM), F32),
                        pltpu.SemaphoreType.DMA((2, KV_SLOTS, ppc)),
                        pltpu.VMEM((rows, 1), F32),
                        pltpu.VMEM((rows, 1), F32),
                        pltpu.VMEM((rows, HEAD_DIM), F32)],
    )
    return pl.pallas_call(
        kernel,
        grid_spec=grid_spec,
        out_shape=jax.ShapeDtypeStruct((bsz, rows, HEAD_DIM), F32),
        compiler_params=_params("arbitrary"),
        name="sample_attn",
    )(page_table, q_s, bias, gmask, k_new, v_new, cache_k, cache_v)


def _rope_tables(pos, head_w):
    half = head_w // 2
    inv = ROPE_THETA ** (-jnp.arange(half, dtype=F32) / half)
    ang = pos.astype(F32)[:, None] * inv[None, :]
    cos, sin = jnp.cos(ang), jnp.sin(ang)
    reps = V7X_LANES // head_w
    cos_t = jnp.tile(jnp.concatenate([cos, cos], axis=1), (1, reps))
    sin_t = jnp.tile(jnp.concatenate([-sin, sin], axis=1), (1, reps))
    return cos_t, sin_t


class _InWeights:
    def __init__(self, w_in, d):
        dc = d // 2
        self.t = w_in.T.astype(BF16)
        self.conv = 0
        self.q = 3 * dc
        self.kv = self.q + D_ATTN
        self.qi = self.kv + 2 * D_KV
        self.kiwi = self.qi + D_QI
        self.gates = self.kiwi + IDX_DIM + N_IDX_HEADS
        assert all(r % BF16_SUBLANE_TILE == 0 for r in (self.q, self.kv, self.qi, self.kiwi, self.gates))


def _in_proj(h, w, pos, d):
    cos128, sin128 = _rope_tables(pos, HEAD_DIM)
    cos64, sin64 = _rope_tables(pos, IDX_DIM)
    lane = jnp.arange(V7X_LANES)
    cos_kw = jnp.where(lane < IDX_DIM, cos64, 1.0)
    sin_kw = jnp.where(lane < IDX_DIM, sin64, 0.0)
    dc3 = 3 * (d // 2)
    zc = _mm(h, w.t, w.conv, dc3, BF16, "in_conv", w_t=True)
    gates = _mm(h, w.t, w.gates, 2 * d, BF16, "in_gates", w_t=True)
    q_hm = _mm_rope_q(h, w.t, w.q, D_ATTN, cos128, sin128, HEAD_DIM, "in_q", w_t=True)
    qi_hm = _mm_rope_q(h, w.t, w.qi, D_QI, cos64, sin64, IDX_DIM, "in_qi", w_t=True)
    k32, k16, v32, v16, kw32, kw16 = _mm_kvx(h, w.t, w.kv, w.kiwi, cos128, sin128, cos_kw, sin_kw)
    ki32 = kw32[:, :IDX_DIM]
    ki16 = kw16[:, :IDX_DIM]
    wi = kw32[:, IDX_DIM:IDX_DIM + N_IDX_HEADS] * (N_IDX_HEADS ** -0.5) * (IDX_DIM ** -0.5)
    return zc, gates, q_hm, qi_hm, k32, k16, v32, v16, ki32, ki16, wi


def kernel(x_prompt, x_sample, cache_k, cache_v, cache_kidx, state_conv, page_table, c_prompt, c_sample,
           w_ada, b_ada, g_mix, w_in, conv_w, w_conv_out, w_attn_out, w_out, g_ffn, w_up, w_down, g_final):
    depth = w_ada.shape[0]
    assert depth == 1, "single-layer trunk"
    bp, sp, d = x_prompt.shape
    bs, ts, _ = x_sample.shape
    dc = d // 2
    n_pool, page = cache_k.shape[1], cache_k.shape[2]
    n_pages = page_table.shape[1]
    past = n_pages * page
    assert ts <= QPAD and ts <= page

    c_all = jnp.concatenate([c_prompt, c_sample], axis=0)
    n_c = c_all.shape[0]
    c_all = jnp.pad(c_all, ((0, (-n_c) % 8), (0, 0)))
    mods = _ada(c_all, w_ada[0], b_ada[0])
    mods_p = mods[:bp].reshape(bp, 1, N_MOD * d)
    mods_s = jnp.repeat(mods[bp:bp + bs], ts, axis=0).reshape(1, bs * ts, N_MOD * d)
    rows_p = _Rows(bp * sp, sp, per_token=False)
    rows_s = _Rows(bs * ts, bs * ts, per_token=True)

    w = _InWeights(w_in[0], d)
    wc, wa, wo = w_conv_out[0].astype(BF16), w_attn_out[0].astype(BF16), w_out[0].astype(BF16)
    wu16 = w_up[0].astype(BF16)
    wd16 = w_down[0].astype(BF16)

    xp = x_prompt.reshape(bp * sp, d)
    hp = _norm_mod_call(xp, g_mix[0], mods_p, rows_p, 1, 0)
    pos_p = jnp.arange(sp)
    zc, gates, q_hm, qi_hm, k32, k16, v32, v16, ki32, ki16, wi = _in_proj(hp, w, pos_p, d)
    u_p, conv_p = _conv_prompt(zc, conv_w[0], bp, sp)
    k_top_p = min(TOPK_MAX, sp // 4)
    kc_p = _tile(_tile(sp, ATTN_TQ), ATTN_KC)
    vt16 = v16.reshape(bp, sp // kc_p, kc_p, D_KV).transpose(0, 1, 3, 2)
    attn_p = _attn_prompt(qi_hm, wi.T, ki16, q_hm, k16, vt16, bp, sp, k_top_p)
    merged_p = _merge(u_p, attn_p, wc, wa, gates)
    x1_p = _out_proj(merged_p, wo, xp, mods_p, rows_p)
    y_p = _ffn(x1_p, g_ffn[0], g_final, mods_p, rows_p, wu16, wd16)

    ns = bs * ts
    xs = x_sample.reshape(ns, d)
    hs = _norm_mod_call(xs, g_mix[0], mods_s, rows_s, 1, 0)
    pos_s = jnp.tile(past + jnp.arange(ts), bs)
    zc_s, gates_s, q_hm_s, qi_hm_s, ks32, ks16, vs32, vs16, kis32, kis16, wi_s = _in_proj(hs, w, pos_s, d)

    zc_t = zc_s.reshape(bs, ts, 3 * dc).transpose(1, 0, 2)
    st_t = state_conv[0].transpose(1, 0, 2)
    u_t, nst_t = _conv_sample(zc_t, st_t, conv_w[0])
    u_s = u_t.transpose(1, 0, 2).reshape(ns, dc)
    conv_s = nst_t.transpose(1, 0, 2)

    k_top_s = min(TOPK_MAX, (past + ts) // 4)
    pad_q = ((0, 0), (0, 0), (0, QPAD - ts), (0, 0))
    qi_s = jnp.pad(qi_hm_s.reshape(N_IDX_HEADS, bs, ts, IDX_DIM).transpose(1, 0, 2, 3), pad_q)
    qi_s = qi_s.reshape(bs, N_IDX_HEADS * QPAD, IDX_DIM)
    wi_sp = jnp.pad(wi_s.reshape(bs, ts, N_IDX_HEADS).transpose(0, 2, 1), ((0, 0), (0, 0), (0, QPAD - ts)))
    wi_sp = wi_sp.reshape(bs, N_IDX_HEADS * QPAD, 1)
    ki_new_t = jnp.pad(kis16.reshape(bs, ts, IDX_DIM).transpose(0, 2, 1), ((0, 0), (0, 0), (0, page - ts)))
    expand = jnp.repeat(jnp.eye(page, dtype=BF16), N_KV_HEADS, axis=1)
    kidx_t = jnp.swapaxes(cache_kidx.reshape(n_pool, page, IDX_DIM), 1, 2)
    bias_s = _sample_index(page_table, qi_s, wi_sp, ki_new_t, expand, kidx_t, ts, k_top_s)

    q_s = jnp.pad(q_hm_s.reshape(N_HEADS, bs, ts, HEAD_DIM).transpose(1, 0, 2, 3), pad_q)
    q_s = q_s.reshape(bs, N_HEADS * QPAD, HEAD_DIM)
    prow = page * N_KV_HEADS
    col_head = jnp.arange(prow) % N_KV_HEADS
    row_head = jnp.arange(N_HEADS * QPAD) // (KV_GROUP * QPAD)
    gmask = jnp.where(row_head[:, None] == col_head[None, :], 0.0, NEG_INF).astype(F32)
    pad_new = ((0, 0), (0, page - ts), (0, 0), (0, 0))
    k_new = jnp.pad(ks16.reshape(bs, ts, N_KV_HEADS, HEAD_DIM), pad_new).reshape(bs, prow, HEAD_DIM)
    v_new = jnp.pad(vs16.reshape(bs, ts, N_KV_HEADS, HEAD_DIM), pad_new).reshape(bs, prow, HEAD_DIM)
    ck = cache_k.reshape(n_pool, prow, HEAD_DIM)
    cv = cache_v.reshape(n_pool, prow, HEAD_DIM)
    attn_hq = _sample_attn(page_table, q_s, bias_s, gmask, k_new, v_new, ck, cv)
    attn_s = attn_hq.reshape(bs, N_HEADS, QPAD, HEAD_DIM)[:, :, :ts].transpose(0, 2, 1, 3)
    attn_s = attn_s.reshape(ns, D_ATTN).astype(BF16)

    merged_s = _merge(u_s, attn_s, wc, wa, gates_s)
    x1_s = _out_proj(merged_s, wo, xs, mods_s, rows_s)
    y_s = _ffn(x1_s, g_ffn[0], g_final, mods_s, rows_s, wu16, wd16)

    return (y_p.reshape(bp, sp, d),
            y_s.reshape(bs, ts, d),
            k32.reshape(1, bp, sp, N_KV_HEADS, HEAD_DIM),
            v32.reshape(1, bp, sp, N_KV_HEADS, HEAD_DIM),
            ki32.reshape(1, bp, sp, IDX_DIM),
            conv_p.reshape(1, bp, CONV_W - 1, dc),
            ks32.reshape(1, bs, ts, N_KV_HEADS, HEAD_DIM),
            vs32.reshape(1, bs, ts, N_KV_HEADS, HEAD_DIM),
            kis32.reshape(1, bs, ts, IDX_DIM),
            conv_s.reshape(1, bs, CONV_W - 1, dc))
```

```python
import functools

import jax
import jax.numpy as jnp
from jax import lax
from jax.experimental import pallas as pl
from jax.experimental.pallas import tpu as pltpu

N_HEADS = 16
HEAD_DIM = 128
N_KV_HEADS = 4
KV_GROUP = N_HEADS // N_KV_HEADS
N_IDX_HEADS = 16
IDX_DIM = 64
TOPK_MAX = 256
CONV_W = 3
ROPE_THETA = 10000.0
NORM_EPS = 1e-6
N_MOD = 6
D_ATTN = N_HEADS * HEAD_DIM
D_KV = N_KV_HEADS * HEAD_DIM
D_QI = N_IDX_HEADS * IDX_DIM

V7X_LANES = 128
V7X_VMEM_LIMIT_BYTES = 56 * 1024 * 1024

BF16 = jnp.bfloat16
F32 = jnp.float32
NEG_INF = float("-inf")
LOG2_E = 1.4426950408889634

BISECT_STEPS = 17


def _tile(n, target, mult=8):
    if n <= target:
        return n
    t = (target // mult) * mult
    while t >= mult:
        if n % t == 0:
            return t
        t -= mult
    return n


def _params(*sem):
    return pltpu.CompilerParams(dimension_semantics=sem, vmem_limit_bytes=V7X_VMEM_LIMIT_BYTES)


def _dot(a, b):
    return jnp.dot(a, b, preferred_element_type=F32)


def _dot_nt(a, b):
    return lax.dot_general(a, b, (((1,), (1,)), ((), ())), preferred_element_type=F32)


def _ada_kernel(c_ref, w_ref, b_ref, o_ref):
    c = c_ref[...]
    a = (c * jax.nn.sigmoid(c)).astype(BF16)
    o_ref[...] = _dot(a, w_ref[...].astype(BF16)) + b_ref[...]


def _ada(c, w, b):
    m, d = c.shape
    n = w.shape[1]
    tn = _tile(n, 512, V7X_LANES)
    return pl.pallas_call(
        _ada_kernel,
        grid=(n // tn,),
        in_specs=[pl.BlockSpec((m, d), lambda j: (0, 0)),
                  pl.BlockSpec((d, tn), lambda j: (0, j)),
                  pl.BlockSpec((1, tn), lambda j: (0, j))],
        out_specs=pl.BlockSpec((m, tn), lambda j: (0, j)),
        out_shape=jax.ShapeDtypeStruct((m, n), F32),
        compiler_params=_params("arbitrary"),
        name="ada",
    )(c, w, b.reshape(1, n))


class _Rows:
    def __init__(self, n_rows, rows_per_group, per_token):
        self.n = n_rows
        self.rows_per_group = rows_per_group
        self.per_token = per_token

    def mod_spec(self, tm, d, which, ncol_blocks_per_mod=1, tiled=False):
        tn = d // ncol_blocks_per_mod
        rpg = self.rows_per_group
        if self.per_token:
            shape = (None, tm, tn)
            where = lambda i: (0, i)
        else:
            shape = (None, 1, tn)
            where = lambda i: ((i * tm) // rpg, 0)
        if not tiled:
            return pl.BlockSpec(shape, lambda i, *_: (*where(i), which * ncol_blocks_per_mod))
        return pl.BlockSpec(shape, lambda i, j: (*where(i), which * ncol_blocks_per_mod + j))


def _norm_mod(x, g, sc, sh):
    ms = jnp.mean(x * x, axis=-1, keepdims=True)
    y = x * lax.rsqrt(ms + NORM_EPS) * g
    return y * (1.0 + sc) + sh


def _norm_mod_kernel(x_ref, g_ref, sc_ref, sh_ref, o_ref):
    o_ref[...] = _norm_mod(x_ref[...], g_ref[...], sc_ref[...], sh_ref[...]).astype(o_ref.dtype)


def _norm_mod_call(x, g, mods3, rows, which_sc, which_sh):
    n, d = x.shape
    tm = _tile(min(n, rows.rows_per_group if not rows.per_token else n), 256)
    return pl.pallas_call(
        _norm_mod_kernel,
        grid=(n // tm,),
        in_specs=[pl.BlockSpec((tm, d), lambda i: (i, 0)),
                  pl.BlockSpec((1, d), lambda i: (0, 0)),
                  rows.mod_spec(tm, d, which_sc),
                  rows.mod_spec(tm, d, which_sh)],
        out_specs=pl.BlockSpec((tm, d), lambda i: (i, 0)),
        out_shape=jax.ShapeDtypeStruct((n, d), BF16),
        compiler_params=_params("parallel"),
        name="norm_mod",
    )(x, g.reshape(1, d), mods3, mods3)


BF16_SUBLANE_TILE = 16


def _rows_spec(rows, k, row0_of):
    def index(i, j):
        row0 = row0_of(j)
        return (row0 if isinstance(row0, int) else pl.multiple_of(row0, BF16_SUBLANE_TILE), 0)

    return pl.BlockSpec((pl.Element(rows), pl.Element(k)), index)


def _proj_kernel(x_ref, w_ref, *rest, epilogue, n_extra, w_t):
    dot = _dot_nt if w_t else _dot
    epilogue(dot(x_ref[...], w_ref[...]), rest[:n_extra], rest[n_extra:])


def _proj(x, w, col0, m, *, tm, tn, epilogue, extra, extra_specs, out_shape, out_specs, name, w_t=False):
    n, k = x.shape
    assert n % tm == 0 and m % tn == 0 and w.dtype == BF16
    if w_t:
        assert col0 % BF16_SUBLANE_TILE == 0 and w.shape[1] == k
        w_spec = _rows_spec(tn, k, lambda j: col0 + j * tn)
    else:
        assert col0 % tn == 0 and w.shape[0] == k
        w_spec = pl.BlockSpec((k, tn), lambda i, j: (0, col0 // tn + j))
    return pl.pallas_call(
        functools.partial(_proj_kernel, epilogue=epilogue, n_extra=len(extra), w_t=w_t),
        grid=(n // tm, m // tn),
        in_specs=[pl.BlockSpec((tm, k), lambda i, j: (i, 0)), w_spec] + list(extra_specs),
        out_specs=out_specs,
        out_shape=out_shape,
        compiler_params=_params("parallel", "arbitrary"),
        name=name,
    )(x, w, *extra)


PROJ_TM = 1024
PROJ_TN = 1024


def _proj_tn(m):
    return _tile(m, PROJ_TN, V7X_LANES)


def _cast_epilogue(acc, extra, outs):
    outs[0][...] = acc.astype(outs[0].dtype)


def _mm(x, w, col0, m, out_dtype, name, w_t=False):
    n = x.shape[0]
    tm, tn = _tile(n, PROJ_TM), _proj_tn(m)
    return _proj(x, w, col0, m, tm=tm, tn=tn, epilogue=_cast_epilogue, extra=(), extra_specs=(),
                 out_shape=jax.ShapeDtypeStruct((n, m), out_dtype),
                 out_specs=pl.BlockSpec((tm, tn), lambda i, j: (i, j)), name=name, w_t=w_t)


def _swap_halves(x, head_w):
    if head_w == V7X_LANES:
        return pltpu.roll(x, V7X_LANES // 2, axis=1)
    half = head_w // 2
    lane = lax.broadcasted_iota(jnp.int32, x.shape, 1)
    take_upper = (lane & (head_w - 1)) < half
    return jnp.where(take_upper, pltpu.roll(x, V7X_LANES - half, axis=1), pltpu.roll(x, half, axis=1))


def _rope_cols(acc, cos, sin, head_w):
    out = []
    for c in range(acc.shape[1] // V7X_LANES):
        x = acc[:, c * V7X_LANES:(c + 1) * V7X_LANES]
        out.append(x * cos + _swap_halves(x, head_w) * sin)
    return out


def _rope_q_epilogue(acc, extra, outs, *, head_w):
    cos_ref, sin_ref = extra
    o_ref = outs[0]
    cols = _rope_cols(acc, cos_ref[...], sin_ref[...], head_w)
    per = V7X_LANES // head_w
    for c, x in enumerate(cols):
        for p in range(per):
            o_ref[c * per + p] = x[:, p * head_w:(p + 1) * head_w].astype(o_ref.dtype)


def _mm_rope_q(x, w, col0, m, cos, sin, head_w, name, w_t=False):
    n = x.shape[0]
    tm, tn = _tile(min(n, cos.shape[0]), PROJ_TM), _proj_tn(m)
    hpt = tn // head_w
    table = _table_spec(cos, tm)
    return _proj(x, w, col0, m, tm=tm, tn=tn, epilogue=functools.partial(_rope_q_epilogue, head_w=head_w),
                 extra=(cos, sin), extra_specs=(table, table),
                 out_shape=jax.ShapeDtypeStruct((m // head_w, n, head_w), BF16),
                 out_specs=pl.BlockSpec((hpt, tm, head_w), lambda i, j: (j, i, 0)), name=name, w_t=w_t)


def _table_spec(table, tm):
    period = table.shape[0] // tm
    assert period * tm == table.shape[0]
    return pl.BlockSpec((tm, V7X_LANES), lambda i, j: (i % period, 0))


KVX_TM = 512


def _kvx_kernel(h_ref, wkv_ref, wx_ref, cos_ref, sin_ref, cos_x_ref, sin_x_ref,
                k32_ref, k16_ref, v32_ref, v16_ref, x32_ref, x16_ref):
    h = h_ref[...]
    kv = _dot_nt(h, wkv_ref[...])
    tm = kv.shape[0]
    for c, k in enumerate(_rope_cols(kv[:, :D_KV], cos_ref[...], sin_ref[...], HEAD_DIM)):
        k32_ref[pl.ds(c, tm, stride=N_KV_HEADS), :] = k
        k16_ref[:, c * V7X_LANES:(c + 1) * V7X_LANES] = k.astype(BF16)
    v = kv[:, D_KV:]
    for c in range(N_KV_HEADS):
        v32_ref[pl.ds(c, tm, stride=N_KV_HEADS), :] = v[:, c * HEAD_DIM:(c + 1) * HEAD_DIM]
    v16_ref[...] = v.astype(BF16)
    (x,) = _rope_cols(_dot_nt(h, wx_ref[...]), cos_x_ref[...], sin_x_ref[...], IDX_DIM)
    x32_ref[...] = x
    x16_ref[...] = x.astype(BF16)


def _mm_kvx(x, w_t, row_kv, row_x, cos, sin, cos_x, sin_x):
    n, kdim = x.shape
    assert row_x + V7X_LANES <= w_t.shape[0]
    tm = _tile(min(n, cos.shape[0]), KVX_TM)
    table = _table_spec(cos, tm)
    kv32 = (N_KV_HEADS, HEAD_DIM)
    layouts = [kv32, (1, D_KV), kv32, (1, D_KV), (1, V7X_LANES), (1, V7X_LANES)]
    dtypes = [F32, BF16] * 3
    shapes = [jax.ShapeDtypeStruct((n * r, wd), dt) for (r, wd), dt in zip(layouts, dtypes)]
    specs = [pl.BlockSpec((tm * r, wd), lambda i, j: (i, 0)) for r, wd in layouts]
    return pl.pallas_call(
        _kvx_kernel,
        grid=(n // tm, 1),
        in_specs=[pl.BlockSpec((tm, kdim), lambda i, j: (i, 0)),
                  _rows_spec(2 * D_KV, kdim, lambda j: row_kv),
                  _rows_spec(V7X_LANES, kdim, lambda j: row_x)] + [table] * 4,
        out_specs=specs,
        out_shape=shapes,
        compiler_params=_params("parallel", "arbitrary"),
        name="in_kvx",
    )(x, w_t, w_t, cos, sin, cos_x, sin_x)


PREV_ROWS = 16


def _conv_prompt_kernel(xin_ref, bg_ref, cg_ref, xin_prev_ref, cg_prev_ref, w_ref, u_ref, st_ref,
                        *, blocks_per_seq):
    i = pl.program_id(1)
    p = cg_ref[...].astype(F32) * xin_ref[...].astype(F32)
    prev = cg_prev_ref[...].astype(F32) * xin_prev_ref[...].astype(F32)
    prev = jnp.where(i % blocks_per_seq == 0, 0.0, prev)
    tm = p.shape[0]
    last = PREV_ROWS - 1
    row = lax.broadcasted_iota(jnp.int32, p.shape, 0)
    p1 = pltpu.roll(p, 1, axis=0)
    p1 = jnp.where(row == 0, prev[last:last + 1], p1)
    p2 = pltpu.roll(p, 2, axis=0)
    p2 = jnp.where(row == 0, prev[last - 1:last], jnp.where(row == 1, prev[last:last + 1], p2))
    w = w_ref[...]
    y = p2 * w[0:1] + p1 * w[1:2] + p * w[2:3]
    u_ref[...] = (bg_ref[...].astype(F32) * y).astype(u_ref.dtype)
    st_ref[...] = p[tm - (CONV_W - 1):, :]


def _conv_prompt(zc, conv_w, n_seq, seq):
    n = zc.shape[0]
    dc = zc.shape[1] // 3
    tm = _tile(seq, 512)
    tc = _tile(dc, 512, V7X_LANES)
    ncb = dc // tc
    bps = seq // tm
    prev_row_blk = lambda i: jnp.maximum(i * (tm // PREV_ROWS) - 1, 0)
    u, st = pl.pallas_call(
        functools.partial(_conv_prompt_kernel, blocks_per_seq=bps),
        grid=(ncb, n // tm),
        in_specs=[pl.BlockSpec((tm, tc), lambda j, i: (i, j)),
                  pl.BlockSpec((tm, tc), lambda j, i: (i, ncb + j)),
                  pl.BlockSpec((tm, tc), lambda j, i: (i, 2 * ncb + j)),
                  pl.BlockSpec((PREV_ROWS, tc), lambda j, i: (prev_row_blk(i), j)),
                  pl.BlockSpec((PREV_ROWS, tc), lambda j, i: (prev_row_blk(i), 2 * ncb + j)),
                  pl.BlockSpec((CONV_W, tc), lambda j, i: (0, j))],
        out_specs=[pl.BlockSpec((tm, tc), lambda j, i: (i, j)),
                   pl.BlockSpec((None, CONV_W - 1, tc), lambda j, i: (i // bps, 0, j))],
        out_shape=[jax.ShapeDtypeStruct((n, dc), BF16),
                   jax.ShapeDtypeStruct((n_seq, CONV_W - 1, dc), F32)],
        compiler_params=_params("parallel", "arbitrary"),
        name="conv_prompt",
    )(zc, zc, zc, zc, zc, conv_w)
    return u, st


def _conv_sample_kernel(xin_ref, bg_ref, cg_ref, st_ref, w_ref, u_ref, nst_ref):
    t_len = xin_ref.shape[0]
    w = w_ref[...]
    ext = [st_ref[j] for j in range(CONV_W - 1)]
    ext += [cg_ref[t].astype(F32) * xin_ref[t].astype(F32) for t in range(t_len)]
    for t in range(t_len):
        y = ext[t] * w[0:1]
        for j in range(1, CONV_W):
            y = y + ext[t + j] * w[j:j + 1]
        u_ref[t] = (bg_ref[t].astype(F32) * y).astype(u_ref.dtype)
    for j in range(CONV_W - 1):
        nst_ref[j] = ext[t_len + j]


def _conv_sample(zc_t, st_t, conv_w):
    t_len, b, dc3 = zc_t.shape
    dc = dc3 // 3
    tc = _tile(dc, 512, V7X_LANES)
    ncb = dc // tc
    return pl.pallas_call(
        _conv_sample_kernel,
        grid=(ncb,),
        in_specs=[pl.BlockSpec((t_len, b, tc), lambda j: (0, 0, j)),
                  pl.BlockSpec((t_len, b, tc), lambda j: (0, 0, ncb + j)),
                  pl.BlockSpec((t_len, b, tc), lambda j: (0, 0, 2 * ncb + j)),
                  pl.BlockSpec((CONV_W - 1, b, tc), lambda j: (0, 0, j)),
                  pl.BlockSpec((CONV_W, tc), lambda j: (0, j))],
        out_specs=[pl.BlockSpec((t_len, b, tc), lambda j: (0, 0, j)),
                   pl.BlockSpec((CONV_W - 1, b, tc), lambda j: (0, 0, j))],
        out_shape=[jax.ShapeDtypeStruct((t_len, b, dc), BF16),
                   jax.ShapeDtypeStruct((CONV_W - 1, b, dc), F32)],
        compiler_params=_params("parallel"),
        name="conv_sample",
    )(zc_t, zc_t, zc_t, st_t, conv_w)


_SUBLANES = 8
_REDUCERS = {jnp.add: jnp.sum, jnp.maximum: jnp.max, jnp.minimum: jnp.min}


def _fold_chunks(n_chunks, term, combine, init, shape, axis):
    if axis == 0 and shape[0] > _SUBLANES:
        full_term = term
        shape = (_SUBLANES, shape[1])
        term = lambda c: _REDUCERS[combine](full_term(c).reshape(-1, *shape), axis=0)
    if isinstance(n_chunks, int):
        vals = [term(c) for c in range(n_chunks)]
        while len(vals) > 1:
            vals = [combine(vals[i], vals[i + 1]) if i + 1 < len(vals) else vals[i]
                    for i in range(0, len(vals), 2)]
        return vals[0]
    return lax.fori_loop(0, n_chunks, lambda c, acc: combine(acc, term(c)), jnp.full(shape, init, F32))


def _topk_threshold(score_at, index_at, n_chunks, n_keys_max, want, shape, axis):
    fold = functools.partial(_fold_chunks, n_chunks, shape=shape, axis=axis)

    def count_ge(thr):
        part = fold(lambda c: jnp.where(score_at(c) >= thr, 1.0, 0.0), jnp.add, 0.0)
        return jnp.sum(part, axis=axis, keepdims=True)

    hi = jnp.max(fold(score_at, jnp.maximum, NEG_INF), axis=axis, keepdims=True)
    lo = jnp.min(fold(lambda c: jnp.where(score_at(c) == NEG_INF, jnp.inf, score_at(c)), jnp.minimum, jnp.inf),
                 axis=axis, keepdims=True)
    cnt_lo = count_ge(lo)
    cnt_hi = count_ge(hi)
    top_ok = cnt_hi >= want
    lo = jnp.where(top_ok, hi, lo)
    cnt_lo = jnp.where(top_ok, cnt_hi, cnt_lo)

    def bisect(_, carry):
        lo, hi, cnt_lo = carry
        mid = lo + (hi - lo) * 0.5
        cnt = count_ge(mid)
        ok = cnt >= want
        return (jnp.where(ok, mid, lo), jnp.where(ok, hi, mid), jnp.where(ok, cnt, cnt_lo))

    lo, hi, cnt_lo = lax.fori_loop(0, BISECT_STEPS, bisect, (lo, hi, cnt_lo))

    def pending_of(cnt_lo, done):
        return jnp.max(jnp.where((cnt_lo > want) & (done == 0.0), 1.0, 0.0))

    def finish(carry):
        lo, hi, cnt_lo, done, _ = carry
        cand = jnp.max(fold(lambda c: jnp.where(score_at(c) < hi, score_at(c), NEG_INF), jnp.maximum, NEG_INF),
                       axis=axis, keepdims=True)
        cnt = count_ge(cand)
        active = (cnt_lo > want) & (done == 0.0)
        ok = active & (cnt >= want)
        lo = jnp.where(ok, cand, lo)
        cnt_lo = jnp.where(ok, cnt, cnt_lo)
        done = jnp.where(ok, 1.0, done)
        hi = jnp.where(active & (cnt < want), cand, hi)
        return lo, hi, cnt_lo, done, pending_of(cnt_lo, done)

    done0 = jnp.where(lo == hi, 1.0, 0.0)
    lo, _, cnt_lo, *_ = lax.while_loop(lambda carry: carry[4] > 0.0, finish,
                                       (lo, hi, cnt_lo, done0, pending_of(cnt_lo, done0)))

    excess = cnt_lo - want
    tied = jnp.max(jnp.where(excess > 0.0, 1.0, 0.0))

    def count_gt(carry):
        part = fold(lambda c: jnp.where(score_at(c) > lo, 1.0, 0.0), jnp.add, 0.0)
        return jnp.sum(part, axis=axis, keepdims=True), 0.0

    cnt_gt, _ = lax.while_loop(lambda carry: carry[1] > 0.0, count_gt, (jnp.zeros_like(lo), tied))
    need = want - cnt_gt

    def tie_step(carry):
        lo_j, hi_j, left = carry
        mid = jnp.floor((lo_j + hi_j) * 0.5)
        part = fold(lambda c: jnp.where((score_at(c) == lo) & (index_at(c) <= mid), 1.0, 0.0), jnp.add, 0.0)
        ok = jnp.sum(part, axis=axis, keepdims=True) >= need
        return jnp.where(ok, lo_j, mid), jnp.where(ok, mid, hi_j), left - 1.0

    steps = float(max(1, (n_keys_max - 1).bit_length() + 1))
    _, hi_j, _ = lax.while_loop(lambda carry: carry[2] > 0.0, tie_step,
                                (jnp.full_like(lo, -1.0), jnp.full_like(lo, n_keys_max - 1.0), tied * steps))
    tie_j = jnp.where(excess > 0.0, hi_j, jnp.inf)
    return lo, tie_j


def _selected(score, index, lo, tie_j):
    return (score > lo) | ((score == lo) & (index <= tie_j))


ATTN_TQ = 256
ATTN_KC = 256


def _attn_prompt_kernel(qi_ref, wi_ref, ki_ref, q_ref, k_ref, vt_ref, o_ref,
                        score_ref, m_ref, l_ref, acc_ref, *, tq, kc, k_top):
    qb = pl.program_id(1)
    n_chunks = (qb * tq) // kc + tq // kc
    qpos = qb * tq + lax.broadcasted_iota(jnp.int32, (1, tq), 1)
    wi = wi_ref[...]

    def fill(c, _):
        kic = ki_ref[pl.ds(pl.multiple_of(c * kc, kc), kc), :]
        acc = jnp.zeros((kc, tq), F32)
        for h in range(N_IDX_HEADS):
            acc = acc + jnp.maximum(_dot_nt(kic, qi_ref[h]), 0.0) * wi[h:h + 1]
        kpos = c * kc + lax.broadcasted_iota(jnp.int32, (kc, tq), 0)
        score_ref[c] = jnp.where(kpos <= qpos, acc, NEG_INF)
        return 0

    lax.fori_loop(0, n_chunks, fill, 0)

    want = jnp.minimum(qpos + 1, k_top).astype(F32)
    def key_index(c):
        return (c * kc + lax.broadcasted_iota(jnp.int32, (kc, tq), 0)).astype(F32)

    n_keys_max = score_ref.shape[0] * kc
    thr, tie_j = _topk_threshold(lambda c: score_ref[c], key_index, n_chunks, n_keys_max, want, (kc, tq), axis=0)

    def to_bias(c, _):
        score_ref[c] = jnp.where(_selected(score_ref[c], key_index(c), thr, tie_j), 0.0, NEG_INF)
        return 0

    lax.fori_loop(0, n_chunks, to_bias, 0)

    scale = HEAD_DIM ** -0.5 * LOG2_E
    for g in range(N_KV_HEADS):
        m_ref[...] = jnp.full(m_ref.shape, NEG_INF, F32)
        l_ref[...] = jnp.zeros(l_ref.shape, F32)
        acc_ref[...] = jnp.zeros(acc_ref.shape, F32)

        heads = range(KV_GROUP)

        def scores(c):
            kch = k_ref[pl.ds(pl.multiple_of(c * kc, kc), kc), g * HEAD_DIM:(g + 1) * HEAD_DIM]
            bias = score_ref[c]
            return [_dot_nt(kch, q_ref[g * KV_GROUP + r]) * scale + bias for r in heads]

        def update(c, s):
            vt = vt_ref[c, g * HEAD_DIM:(g + 1) * HEAD_DIM, :]
            m_old = [m_ref[r] for r in heads]
            l_old = [l_ref[r] for r in heads]
            m_new = [jnp.maximum(m_old[r], jnp.max(s[r], axis=0, keepdims=True)) for r in heads]
            m_safe = [jnp.where(m_new[r] == NEG_INF, 0.0, m_new[r]) for r in heads]
            p = [jnp.exp2(s[r] - m_safe[r]) for r in heads]
            alpha = [jnp.exp2(m_old[r] - m_safe[r]) for r in heads]
            pv = [_dot(vt, p[r].astype(BF16)) for r in heads]
            for r in heads:
                l_ref[r] = alpha[r] * l_old[r] + jnp.sum(p[r], axis=0, keepdims=True)
                acc_ref[r] = alpha[r] * acc_ref[r] + pv[r]
                m_ref[r] = m_new[r]

        def attend(first, count):
            s = [scores(first + u) for u in range(count)]
            for u in range(count):
                update(first + u, s[u])

        def attend_four(i, _):
            attend(4 * i, 4)
            return 0

        lax.fori_loop(0, n_chunks // 4, attend_four, 0)
        rest = n_chunks % 4
        done = n_chunks - rest

        @pl.when(rest >= 2)
        def _():
            attend(done, 2)

        @pl.when(rest % 2 == 1)
        def _():
            attend(n_chunks - 1, 1)
        for r in range(KV_GROUP):
            h = g * KV_GROUP + r
            out = acc_ref[r] / l_ref[r]
            o_ref[:, h * HEAD_DIM:(h + 1) * HEAD_DIM] = out.T.astype(o_ref.dtype)


def _attn_prompt(qi_hm, wi_t, ki16, q_hm, k16, vt16, n_seq, seq, k_top):
    n = n_seq * seq
    tq = _tile(seq, ATTN_TQ)
    kc = _tile(tq, ATTN_KC)
    nqb = seq // tq
    assert vt16.shape == (n_seq, seq // kc, D_KV, kc)
    kernel = functools.partial(_attn_prompt_kernel, tq=tq, kc=kc, k_top=k_top)
    return pl.pallas_call(
        kernel,
        grid=(n_seq, nqb),
        in_specs=[pl.BlockSpec((N_IDX_HEADS, tq, IDX_DIM), lambda b, q: (0, b * nqb + q, 0)),
                  pl.BlockSpec((N_IDX_HEADS, tq), lambda b, q: (0, b * nqb + q)),
                  pl.BlockSpec((seq, IDX_DIM), lambda b, q: (b, 0)),
                  pl.BlockSpec((N_HEADS, tq, HEAD_DIM), lambda b, q: (0, b * nqb + q, 0)),
                  pl.BlockSpec((seq, D_KV), lambda b, q: (b, 0)),
                  pl.BlockSpec((None, seq // kc, D_KV, kc), lambda b, q: (b, 0, 0, 0))],
        out_specs=pl.BlockSpec((tq, D_ATTN), lambda b, q: (b * nqb + q, 0)),
        out_shape=jax.ShapeDtypeStruct((n, D_ATTN), BF16),
        scratch_shapes=[pltpu.VMEM((seq // kc, kc, tq), F32),
                        pltpu.VMEM((KV_GROUP, 1, tq), F32),
                        pltpu.VMEM((KV_GROUP, 1, tq), F32),
                        pltpu.VMEM((KV_GROUP, HEAD_DIM, tq), F32)],
        compiler_params=_params("parallel", "arbitrary"),
        name="attn_prompt",
    )(qi_hm, wi_t, ki16, q_hm, k16, vt16)


def _merge_kernel(u_ref, a_ref, wc_ref, wa_ref, gc_ref, ga_ref, o_ref):
    yc = _dot(u_ref[...], wc_ref[...])
    ya = _dot(a_ref[...], wa_ref[...])
    gc = jax.nn.sigmoid(gc_ref[...].astype(F32))
    ga = jax.nn.sigmoid(ga_ref[...].astype(F32))
    o_ref[...] = (gc * yc + ga * ya).astype(o_ref.dtype)


def _merge(u, attn, wc, wa, gates):
    n, dc = u.shape
    da = attn.shape[1]
    d = wc.shape[1]
    tm = _tile(n, PROJ_TM)
    tn = _proj_tn(d)
    ncb = d // tn
    return pl.pallas_call(
        _merge_kernel,
        grid=(n // tm, ncb),
        in_specs=[pl.BlockSpec((tm, dc), lambda i, j: (i, 0)),
                  pl.BlockSpec((tm, da), lambda i, j: (i, 0)),
                  pl.BlockSpec((dc, tn), lambda i, j: (0, j)),
                  pl.BlockSpec((da, tn), lambda i, j: (0, j)),
                  pl.BlockSpec((tm, tn), lambda i, j: (i, j)),
                  pl.BlockSpec((tm, tn), lambda i, j: (i, ncb + j))],
        out_specs=pl.BlockSpec((tm, tn), lambda i, j: (i, j)),
        out_shape=jax.ShapeDtypeStruct((n, d), BF16),
        compiler_params=_params("parallel", "arbitrary"),
        name="merge",
    )(u, attn, wc, wa, gates, gates)


def _out_proj_epilogue(acc, extra, outs):
    x_ref, gt_ref = extra
    outs[0][...] = x_ref[...] + gt_ref[...] * acc


def _out_proj(a, w, x, mods3, rows):
    n = a.shape[0]
    d = w.shape[1]
    tm = _tile(min(n, rows.rows_per_group if not rows.per_token else n), PROJ_TM)
    tn = _proj_tn(d)
    tile = pl.BlockSpec((tm, tn), lambda i, j: (i, j))
    gate = rows.mod_spec(tm, d, 2, d // tn, tiled=True)
    return _proj(a, w, 0, d, tm=tm, tn=tn, epilogue=_out_proj_epilogue, extra=(x, mods3), extra_specs=(tile, gate),
                 out_shape=jax.ShapeDtypeStruct((n, d), F32), out_specs=tile, name="out_proj")


def _ffn_kernel(x_ref, g_ref, sc_ref, sh_ref, gt_ref, gf_ref, wu_ref, wd_ref, o_ref, h_ref):
    f = pl.program_id(1)

    @pl.when(f == 0)
    def _():
        h_ref[...] = _norm_mod(x_ref[...], g_ref[...], sc_ref[...], sh_ref[...]).astype(BF16)
        o_ref[...] = jnp.zeros(o_ref.shape, F32)

    u = jnp.maximum(_dot(h_ref[...], wu_ref[...]), 0.0)
    o_ref[...] += _dot((u * u).astype(BF16), wd_ref[...])

    @pl.when(f == pl.num_programs(1) - 1)
    def _():
        x2 = x_ref[...] + gt_ref[...] * o_ref[...]
        ms = jnp.mean(x2 * x2, axis=-1, keepdims=True)
        o_ref[...] = x2 * lax.rsqrt(ms + NORM_EPS) * gf_ref[...]


def _ffn(x1, g_ffn, g_final, mods3, rows, w_up, w_down):
    n, d = x1.shape
    dff = w_up.shape[1]
    tm = _tile(min(n, rows.rows_per_group if not rows.per_token else n), 512)
    tf = _tile(dff, 512, V7X_LANES)
    return pl.pallas_call(
        _ffn_kernel,
        grid=(n // tm, dff // tf),
        in_specs=[pl.BlockSpec((tm, d), lambda i, f: (i, 0), pipeline_mode=pl.Buffered(1)),
                  pl.BlockSpec((1, d), lambda i, f: (0, 0)),
                  rows.mod_spec(tm, d, 4),
                  rows.mod_spec(tm, d, 3),
                  rows.mod_spec(tm, d, 5),
                  pl.BlockSpec((1, d), lambda i, f: (0, 0)),
                  pl.BlockSpec((d, tf), lambda i, f: (0, f)),
                  pl.BlockSpec((tf, d), lambda i, f: (f, 0))],
        out_specs=pl.BlockSpec((tm, d), lambda i, f: (i, 0)),
        out_shape=jax.ShapeDtypeStruct((n, d), F32),
        scratch_shapes=[pltpu.VMEM((tm, d), BF16)],
        compiler_params=_params("parallel", "arbitrary"),
        name="ffn",
    )(x1, g_ffn.reshape(1, d), mods3, mods3, mods3, g_final.reshape(1, d), w_up, w_down)


QPAD = 8
INDEX_PAGES_PER_DOT = 8


def _sample_index_kernel(pt_ref, qi_ref, wi_ref, kin_ref, expand_ref, cache_ref, bias_ref,
                         kbuf_ref, sem_ref, score_ref, *, n_pages, page, t_len, k_top):
    b = pl.program_id(0)

    slot = b % 2

    def page_copy(bb, p):
        return pltpu.make_async_copy(cache_ref.at[pt_ref[bb, p]], kbuf_ref.at[bb % 2, p], sem_ref.at[bb % 2, p])

    @pl.when(b == 0)
    def _():
        for p in range(n_pages):
            page_copy(0, p).start()

    @pl.when(b + 1 < pl.num_programs(0))
    def _():
        for p in range(n_pages):
            page_copy(b + 1, p).start()

    qi = qi_ref[...]
    wi = wi_ref[...].reshape(N_IDX_HEADS, QPAD, 1)

    def head_sum(s):
        s3 = jnp.maximum(s, 0.0).reshape(N_IDX_HEADS, QPAD, s.shape[1]) * wi
        return jnp.sum(s3, axis=0)

    ppd = INDEX_PAGES_PER_DOT if n_pages % INDEX_PAGES_PER_DOT == 0 else 1
    for p0 in range(0, n_pages, ppd):
        for p in range(p0, p0 + ppd):
            page_copy(b, p).wait()
        kt = jnp.concatenate([kbuf_ref[slot, p] for p in range(p0, p0 + ppd)], axis=1).astype(BF16)
        sc = head_sum(_dot(qi, kt))
        for j in range(ppd):
            score_ref[p0 + j] = sc[:, j * page:(j + 1) * page]

    qrow = lax.broadcasted_iota(jnp.int32, (QPAD, page), 0)
    lane = lax.broadcasted_iota(jnp.int32, (QPAD, page), 1)
    s_new = head_sum(_dot(qi, kin_ref[...]))
    score_ref[n_pages] = jnp.where((lane <= qrow) & (lane < t_len), s_new, NEG_INF)

    n_chunks = n_pages + 1
    lane_f = lane.astype(F32)
    key_index = lambda c: lane_f + float(c * page)
    is_query = lax.broadcasted_iota(jnp.int32, (QPAD, 1), 0) < t_len
    want = jnp.where(is_query, float(k_top), float(n_pages * page + t_len))
    thr, tie_j = _topk_threshold(lambda c: score_ref[c], key_index, n_chunks, n_chunks * page, want,
                                 (QPAD, page), axis=1)

    sel = [jnp.where(_selected(score_ref[c], key_index(c), thr, tie_j), 1.0, 0.0) for c in range(n_chunks)]
    sel += [jnp.zeros((QPAD, page), F32)] * (n_chunks % 2)
    hit = _dot(jnp.concatenate(sel, axis=0).astype(BF16), expand_ref[...])
    bias = jnp.where(hit > 0.5, 0.0, NEG_INF)
    for c in range(n_chunks):
        bias_ref[c] = bias[c * QPAD:(c + 1) * QPAD]


def _sample_index(page_table, qi_s, wi_s, ki_new_t, expand, cache_kidx_t, t_len, k_top):
    bsz, n_pages = page_table.shape
    page = cache_kidx_t.shape[2]
    kernel = functools.partial(_sample_index_kernel, n_pages=n_pages, page=page, t_len=t_len, k_top=k_top)
    grid_spec = pltpu.PrefetchScalarGridSpec(
        num_scalar_prefetch=1,
        grid=(bsz,),
        in_specs=[pl.BlockSpec((None, N_IDX_HEADS * QPAD, IDX_DIM), lambda b, pt: (b, 0, 0)),
                  pl.BlockSpec((None, N_IDX_HEADS * QPAD, 1), lambda b, pt: (b, 0, 0)),
                  pl.BlockSpec((None, IDX_DIM, page), lambda b, pt: (b, 0, 0)),
                  pl.BlockSpec(expand.shape, lambda b, pt: (0, 0)),
                  pl.BlockSpec(memory_space=pl.ANY)],
        out_specs=pl.BlockSpec((None, n_pages + 1, QPAD, N_KV_HEADS * page), lambda b, pt: (b, 0, 0, 0)),
        scratch_shapes=[pltpu.VMEM((2, n_pages, IDX_DIM, page), F32),
                        pltpu.SemaphoreType.DMA((2, n_pages)),
                        pltpu.VMEM((n_pages + 1, QPAD, page), F32)],
    )
    return pl.pallas_call(
        kernel,
        grid_spec=grid_spec,
        out_shape=jax.ShapeDtypeStruct((bsz, n_pages + 1, QPAD, N_KV_HEADS * page), F32),
        compiler_params=_params("arbitrary"),
        name="sample_index",
    )(page_table, qi_s, wi_s, ki_new_t, expand, cache_kidx_t)


PAGES_PER_CHUNK = 8
KV_SLOTS = 4


def _sample_attn_kernel(pt_ref, q_ref, bias_ref, gmask_ref, kn_ref, vn_ref, ck_ref, cv_ref, o_ref,
                        kbuf_ref, vbuf_ref, sem_ref, m_ref, l_ref, acc_ref, *, n_pages, ppc):
    b = pl.program_id(0)
    n_chunks = n_pages // ppc
    rows = N_HEADS * QPAD
    prow = kbuf_ref.shape[2]

    n_total = pl.num_programs(0) * n_chunks

    def copies(g):
        bb, c, slot = g // n_chunks, g % n_chunks, g % KV_SLOTS
        out = []
        for j in range(ppc):
            pid = pt_ref[bb, c * ppc + j]
            out.append(pltpu.make_async_copy(ck_ref.at[pid], kbuf_ref.at[slot, j], sem_ref.at[0, slot, j]))
            out.append(pltpu.make_async_copy(cv_ref.at[pid], vbuf_ref.at[slot, j], sem_ref.at[1, slot, j]))
        return out

    def start(g):
        @pl.when(g < n_total)
        def _():
            for cp in copies(g):
                cp.start()

    @pl.when(b == 0)
    def _():
        for g in range(KV_SLOTS - 1):
            start(g)

    q = q_ref[...]
    gmask = gmask_ref[...]
    m_ref[...] = jnp.full(m_ref.shape, NEG_INF, F32)
    l_ref[...] = jnp.zeros(l_ref.shape, F32)
    acc_ref[...] = jnp.zeros(acc_ref.shape, F32)
    scale = HEAD_DIM ** -0.5 * LOG2_E

    def attend(kch, vch, biases):
        s = _dot_nt(q, kch) * scale
        parts = []
        for j, bias in enumerate(biases):
            sj = s[:, j * prow:(j + 1) * prow].reshape(N_HEADS, QPAD, prow) + bias[None]
            parts.append(sj.reshape(rows, prow) + gmask)
        s = parts[0] if len(parts) == 1 else jnp.concatenate(parts, axis=1)
        m_old = m_ref[...]
        m_new = jnp.maximum(m_old, jnp.max(s, axis=1, keepdims=True))
        m_safe = jnp.where(m_new == NEG_INF, 0.0, m_new)
        p = jnp.exp2(s - m_safe)
        alpha = jnp.exp2(m_old - m_safe)
        l_ref[...] = alpha * l_ref[...] + jnp.sum(p, axis=1, keepdims=True)
        acc_ref[...] = alpha * acc_ref[...] + _dot(p.astype(BF16), vch)
        m_ref[...] = m_new

    def chunk_body(c, _):
        g = b * n_chunks + c
        slot = g % KV_SLOTS
        start(g + KV_SLOTS - 1)
        for cp in copies(g):
            cp.wait()
        kch = kbuf_ref[slot].reshape(ppc * prow, HEAD_DIM).astype(BF16)
        vch = vbuf_ref[slot].reshape(ppc * prow, HEAD_DIM).astype(BF16)
        attend(kch, vch, [bias_ref[c * ppc + j] for j in range(ppc)])
        return 0

    lax.fori_loop(0, n_chunks, chunk_body, 0)
    attend(kn_ref[...], vn_ref[...], [bias_ref[n_pages]])
    o_ref[...] = acc_ref[...] / l_ref[...]


def _sample_attn(page_table, q_s, bias, gmask, k_new, v_new, cache_k, cache_v):
    bsz, n_pages = page_table.shape
    prow = cache_k.shape[1]
    ppc = PAGES_PER_CHUNK if n_pages % PAGES_PER_CHUNK == 0 else 1
    rows = N_HEADS * QPAD
    kernel = functools.partial(_sample_attn_kernel, n_pages=n_pages, ppc=ppc)
    grid_spec = pltpu.PrefetchScalarGridSpec(
        num_scalar_prefetch=1,
        grid=(bsz,),
        in_specs=[pl.BlockSpec((None, rows, HEAD_DIM), lambda b, pt: (b, 0, 0)),
                  pl.BlockSpec((None, n_pages + 1, QPAD, prow), lambda b, pt: (b, 0, 0, 0)),
                  pl.BlockSpec((rows, prow), lambda b, pt: (0, 0)),
                  pl.BlockSpec((None, prow, HEAD_DIM), lambda b, pt: (b, 0, 0)),
                  pl.BlockSpec((None, prow, HEAD_DIM), lambda b, pt: (b, 0, 0)),
                  pl.BlockSpec(memory_space=pl.ANY),
                  pl.BlockSpec(memory_space=pl.ANY)],
        out_specs=pl.BlockSpec((None, rows, HEAD_DIM), lambda b, pt: (b, 0, 0)),
        scratch_shapes=[pltpu.VMEM((KV_SLOTS, ppc, prow, HEAD_DIM), F32),
                        pltpu.VMEM((KV_SLOTS, ppc, prow, HEAD_DIM), F32),
                        pltpu.SemaphoreType.DMA((2, KV_SLOTS, ppc)),
                        pltpu.VMEM((rows, 1), F32),
                        pltpu.VMEM((rows, 1), F32),
                        pltpu.VMEM((rows, HEAD_DIM), F32)],
    )
    return pl.pallas_call(
        kernel,
        grid_spec=grid_spec,
        out_shape=jax.ShapeDtypeStruct((bsz, rows, HEAD_DIM), F32),
        compiler_params=_params("arbitrary"),
        name="sample_attn",
    )(page_table, q_s, bias, gmask, k_new, v_new, cache_k, cache_v)


def _rope_tables(pos, head_w):
    half = head_w // 2
    inv = ROPE_THETA ** (-jnp.arange(half, dtype=F32) / half)
    ang = pos.astype(F32)[:, None] * inv[None, :]
    cos, sin = jnp.cos(ang), jnp.sin(ang)
    reps = V7X_LANES // head_w
    cos_t = jnp.tile(jnp.concatenate([cos, cos], axis=1), (1, reps))
    sin_t = jnp.tile(jnp.concatenate([-sin, sin], axis=1), (1, reps))
    return cos_t, sin_t


class _InWeights:
    def __init__(self, w_in, d):
        dc = d // 2
        self.t = w_in.T.astype(BF16)
        self.conv = 0
        self.q = 3 * dc
        self.kv = self.q + D_ATTN
        self.qi = self.kv + 2 * D_KV
        self.kiwi = self.qi + D_QI
        self.gates = self.kiwi + IDX_DIM + N_IDX_HEADS
        assert all(r % BF16_SUBLANE_TILE == 0 for r in (self.q, self.kv, self.qi, self.kiwi, self.gates))


def _in_proj(h, w, pos, d):
    cos128, sin128 = _rope_tables(pos, HEAD_DIM)
    cos64, sin64 = _rope_tables(pos, IDX_DIM)
    lane = jnp.arange(V7X_LANES)
    cos_kw = jnp.where(lane < IDX_DIM, cos64, 1.0)
    sin_kw = jnp.where(lane < IDX_DIM, sin64, 0.0)
    dc3 = 3 * (d // 2)
    zc = _mm(h, w.t, w.conv, dc3, BF16, "in_conv", w_t=True)
    gates = _mm(h, w.t, w.gates, 2 * d, BF16, "in_gates", w_t=True)
    q_hm = _mm_rope_q(h, w.t, w.q, D_ATTN, cos128, sin128, HEAD_DIM, "in_q", w_t=True)
    qi_hm = _mm_rope_q(h, w.t, w.qi, D_QI, cos64, sin64, IDX_DIM, "in_qi", w_t=True)
    k32, k16, v32, v16, kw32, kw16 = _mm_kvx(h, w.t, w.kv, w.kiwi, cos128, sin128, cos_kw, sin_kw)
    ki32 = kw32[:, :IDX_DIM]
    ki16 = kw16[:, :IDX_DIM]
    wi = kw32[:, IDX_DIM:IDX_DIM + N_IDX_HEADS] * (N_IDX_HEADS ** -0.5) * (IDX_DIM ** -0.5)
    return zc, gates, q_hm, qi_hm, k32, k16, v32, v16, ki32, ki16, wi


def kernel(x_prompt, x_sample, cache_k, cache_v, cache_kidx, state_conv, page_table, c_prompt, c_sample,
           w_ada, b_ada, g_mix, w_in, conv_w, w_conv_out, w_attn_out, w_out, g_ffn, w_up, w_down, g_final):
    depth = w_ada.shape[0]
    assert depth == 1, "single-layer trunk"
    bp, sp, d = x_prompt.shape
    bs, ts, _ = x_sample.shape
    dc = d // 2
    n_pool, page = cache_k.shape[1], cache_k.shape[2]
    n_pages = page_table.shape[1]
    past = n_pages * page
    assert ts <= QPAD and ts <= page

    c_all = jnp.concatenate([c_prompt, c_sample], axis=0)
    n_c = c_all.shape[0]
    c_all = jnp.pad(c_all, ((0, (-n_c) % 8), (0, 0)))
    mods = _ada(c_all, w_ada[0], b_ada[0])
    mods_p = mods[:bp].reshape(bp, 1, N_MOD * d)
    mods_s = jnp.repeat(mods[bp:bp + bs], ts, axis=0).reshape(1, bs * ts, N_MOD * d)
    rows_p = _Rows(bp * sp, sp, per_token=False)
    rows_s = _Rows(bs * ts, bs * ts, per_token=True)

    w = _InWeights(w_in[0], d)
    wc, wa, wo = w_conv_out[0].astype(BF16), w_attn_out[0].astype(BF16), w_out[0].astype(BF16)
    wu16 = w_up[0].astype(BF16)
    wd16 = w_down[0].astype(BF16)

    xp = x_prompt.reshape(bp * sp, d)
    hp = _norm_mod_call(xp, g_mix[0], mods_p, rows_p, 1, 0)
    pos_p = jnp.arange(sp)
    zc, gates, q_hm, qi_hm, k32, k16, v32, v16, ki32, ki16, wi = _in_proj(hp, w, pos_p, d)
    u_p, conv_p = _conv_prompt(zc, conv_w[0], bp, sp)
    k_top_p = min(TOPK_MAX, sp // 4)
    kc_p = _tile(_tile(sp, ATTN_TQ), ATTN_KC)
    vt16 = v16.reshape(bp, sp // kc_p, kc_p, D_KV).transpose(0, 1, 3, 2)
    attn_p = _attn_prompt(qi_hm, wi.T, ki16, q_hm, k16, vt16, bp, sp, k_top_p)
    merged_p = _merge(u_p, attn_p, wc, wa, gates)
    x1_p = _out_proj(merged_p, wo, xp, mods_p, rows_p)
    y_p = _ffn(x1_p, g_ffn[0], g_final, mods_p, rows_p, wu16, wd16)

    ns = bs * ts
    xs = x_sample.reshape(ns, d)
    hs = _norm_mod_call(xs, g_mix[0], mods_s, rows_s, 1, 0)
    pos_s = jnp.tile(past + jnp.arange(ts), bs)
    zc_s, gates_s, q_hm_s, qi_hm_s, ks32, ks16, vs32, vs16, kis32, kis16, wi_s = _in_proj(hs, w, pos_s, d)

    zc_t = zc_s.reshape(bs, ts, 3 * dc).transpose(1, 0, 2)
    st_t = state_conv[0].transpose(1, 0, 2)
    u_t, nst_t = _conv_sample(zc_t, st_t, conv_w[0])
    u_s = u_t.transpose(1, 0, 2).reshape(ns, dc)
    conv_s = nst_t.transpose(1, 0, 2)

    k_top_s = min(TOPK_MAX, (past + ts) // 4)
    pad_q = ((0, 0), (0, 0), (0, QPAD - ts), (0, 0))
    qi_s = jnp.pad(qi_hm_s.reshape(N_IDX_HEADS, bs, ts, IDX_DIM).transpose(1, 0, 2, 3), pad_q)
    qi_s = qi_s.reshape(bs, N_IDX_HEADS * QPAD, IDX_DIM)
    wi_sp = jnp.pad(wi_s.reshape(bs, ts, N_IDX_HEADS).transpose(0, 2, 1), ((0, 0), (0, 0), (0, QPAD - ts)))
    wi_sp = wi_sp.reshape(bs, N_IDX_HEADS * QPAD, 1)
    ki_new_t = jnp.pad(kis16.reshape(bs, ts, IDX_DIM).transpose(0, 2, 1), ((0, 0), (0, 0), (0, page - ts)))
    expand = jnp.repeat(jnp.eye(page, dtype=BF16), N_KV_HEADS, axis=1)
    kidx_t = jnp.swapaxes(cache_kidx.reshape(n_pool, page, IDX_DIM), 1, 2)
    bias_s = _sample_index(page_table, qi_s, wi_sp, ki_new_t, expand, kidx_t, ts, k_top_s)

    q_s = jnp.pad(q_hm_s.reshape(N_HEADS, bs, ts, HEAD_DIM).transpose(1, 0, 2, 3), pad_q)
    q_s = q_s.reshape(bs, N_HEADS * QPAD, HEAD_DIM)
    prow = page * N_KV_HEADS
    col_head = jnp.arange(prow) % N_KV_HEADS
    row_head = jnp.arange(N_HEADS * QPAD) // (KV_GROUP * QPAD)
    gmask = jnp.where(row_head[:, None] == col_head[None, :], 0.0, NEG_INF).astype(F32)
    pad_new = ((0, 0), (0, page - ts), (0, 0), (0, 0))
    k_new = jnp.pad(ks16.reshape(bs, ts, N_KV_HEADS, HEAD_DIM), pad_new).reshape(bs, prow, HEAD_DIM)
    v_new = jnp.pad(vs16.reshape(bs, ts, N_KV_HEADS, HEAD_DIM), pad_new).reshape(bs, prow, HEAD_DIM)
    ck = cache_k.reshape(n_pool, prow, HEAD_DIM)
    cv = cache_v.reshape(n_pool, prow, HEAD_DIM)
    attn_hq = _sample_attn(page_table, q_s, bias_s, gmask, k_new, v_new, ck, cv)
    attn_s = attn_hq.reshape(bs, N_HEADS, QPAD, HEAD_DIM)[:, :, :ts].transpose(0, 2, 1, 3)
    attn_s = attn_s.reshape(ns, D_ATTN).astype(BF16)

    merged_s = _merge(u_s, attn_s, wc, wa, gates_s)
    x1_s = _out_proj(merged_s, wo, xs, mods_s, rows_s)
    y_s = _ffn(x1_s, g_ffn[0], g_final, mods_s, rows_s, wu16, wd16)

    return (y_p.reshape(bp, sp, d),
            y_s.reshape(bs, ts, d),
            k32.reshape(1, bp, sp, N_KV_HEADS, HEAD_DIM),
            v32.reshape(1, bp, sp, N_KV_HEADS, HEAD_DIM),
            ki32.reshape(1, bp, sp, IDX_DIM),
            conv_p.reshape(1, bp, CONV_W - 1, dc),
            ks32.reshape(1, bs, ts, N_KV_HEADS, HEAD_DIM),
            vs32.reshape(1, bs, ts, N_KV_HEADS, HEAD_DIM),
            kis32.reshape(1, bs, ts, IDX_DIM),
            conv_s.reshape(1, bs, CONV_W - 1, dc))
```

```python
import functools

import jax
import jax.numpy as jnp
from jax import lax
from jax.experimental import pallas as pl
from jax.experimental.pallas import tpu as pltpu

N_HEADS = 16
HEAD_DIM = 128
N_KV_HEADS = 4
KV_GROUP = N_HEADS // N_KV_HEADS
N_IDX_HEADS = 16
IDX_DIM = 64
TOPK_MAX = 256
CONV_W = 3
ROPE_THETA = 10000.0
NORM_EPS = 1e-6
N_MOD = 6
D_ATTN = N_HEADS * HEAD_DIM
D_KV = N_KV_HEADS * HEAD_DIM
D_QI = N_IDX_HEADS * IDX_DIM

V7X_LANES = 128
V7X_VMEM_LIMIT_BYTES = 56 * 1024 * 1024

BF16 = jnp.bfloat16
F32 = jnp.float32
NEG_INF = float("-inf")
LOG2_E = 1.4426950408889634

BISECT_STEPS = 17


def _tile(n, target, mult=8):
    if n <= target:
        return n
    t = (target // mult) * mult
    while t >= mult:
        if n % t == 0:
            return t
        t -= mult
    return n


def _params(*sem):
    return pltpu.CompilerParams(dimension_semantics=sem, vmem_limit_bytes=V7X_VMEM_LIMIT_BYTES)


def _dot(a, b):
    return jnp.dot(a, b, preferred_element_type=F32)


def _dot_nt(a, b):
    return lax.dot_general(a, b, (((1,), (1,)), ((), ())), preferred_element_type=F32)


def _ada_kernel(c_ref, w_ref, b_ref, o_ref):
    c = c_ref[...]
    a = (c * jax.nn.sigmoid(c)).astype(BF16)
    o_ref[...] = _dot(a, w_ref[...].astype(BF16)) + b_ref[...]


def _ada(c, w, b):
    m, d = c.shape
    n = w.shape[1]
    tn = _tile(n, 512, V7X_LANES)
    return pl.pallas_call(
        _ada_kernel,
        grid=(n // tn,),
        in_specs=[pl.BlockSpec((m, d), lambda j: (0, 0)),
                  pl.BlockSpec((d, tn), lambda j: (0, j)),
                  pl.BlockSpec((1, tn), lambda j: (0, j))],
        out_specs=pl.BlockSpec((m, tn), lambda j: (0, j)),
        out_shape=jax.ShapeDtypeStruct((m, n), F32),
        compiler_params=_params("arbitrary"),
        name="ada",
    )(c, w, b.reshape(1, n))


class _Rows:
    def __init__(self, n_rows, rows_per_group, per_token):
        self.n = n_rows
        self.rows_per_group = rows_per_group
        self.per_token = per_token

    def mod_spec(self, tm, d, which, ncol_blocks_per_mod=1, tiled=False):
        tn = d // ncol_blocks_per_mod
        rpg = self.rows_per_group
        if self.per_token:
            shape = (None, tm, tn)
            where = lambda i: (0, i)
        else:
            shape = (None, 1, tn)
            where = lambda i: ((i * tm) // rpg, 0)
        if not tiled:
            return pl.BlockSpec(shape, lambda i, *_: (*where(i), which * ncol_blocks_per_mod))
        return pl.BlockSpec(shape, lambda i, j: (*where(i), which * ncol_blocks_per_mod + j))


def _norm_mod(x, g, sc, sh):
    ms = jnp.mean(x * x, axis=-1, keepdims=True)
    y = x * lax.rsqrt(ms + NORM_EPS) * g
    return y * (1.0 + sc) + sh


def _norm_mod_kernel(x_ref, g_ref, sc_ref, sh_ref, o_ref):
    o_ref[...] = _norm_mod(x_ref[...], g_ref[...], sc_ref[...], sh_ref[...]).astype(o_ref.dtype)


def _norm_mod_call(x, g, mods3, rows, which_sc, which_sh):
    n, d = x.shape
    tm = _tile(min(n, rows.rows_per_group if not rows.per_token else n), 256)
    return pl.pallas_call(
        _norm_mod_kernel,
        grid=(n // tm,),
        in_specs=[pl.BlockSpec((tm, d), lambda i: (i, 0)),
                  pl.BlockSpec((1, d), lambda i: (0, 0)),
                  rows.mod_spec(tm, d, which_sc),
                  rows.mod_spec(tm, d, which_sh)],
        out_specs=pl.BlockSpec((tm, d), lambda i: (i, 0)),
        out_shape=jax.ShapeDtypeStruct((n, d), BF16),
        compiler_params=_params("parallel"),
        name="norm_mod",
    )(x, g.reshape(1, d), mods3, mods3)


BF16_SUBLANE_TILE = 16


def _rows_spec(rows, k, row0_of):
    def index(i, j):
        row0 = row0_of(j)
        return (row0 if isinstance(row0, int) else pl.multiple_of(row0, BF16_SUBLANE_TILE), 0)

    return pl.BlockSpec((pl.Element(rows), pl.Element(k)), index)


def _proj_kernel(x_ref, w_ref, *rest, epilogue, n_extra, w_t):
    dot = _dot_nt if w_t else _dot
    epilogue(dot(x_ref[...], w_ref[...]), rest[:n_extra], rest[n_extra:])


def _proj(x, w, col0, m, *, tm, tn, epilogue, extra, extra_specs, out_shape, out_specs, name, w_t=False):
    n, k = x.shape
    assert n % tm == 0 and m % tn == 0 and w.dtype == BF16
    if w_t:
        assert col0 % BF16_SUBLANE_TILE == 0 and w.shape[1] == k
        w_spec = _rows_spec(tn, k, lambda j: col0 + j * tn)
    else:
        assert col0 % tn == 0 and w.shape[0] == k
        w_spec = pl.BlockSpec((k, tn), lambda i, j: (0, col0 // tn + j))
    return pl.pallas_call(
        functools.partial(_proj_kernel, epilogue=epilogue, n_extra=len(extra), w_t=w_t),
        grid=(n // tm, m // tn),
        in_specs=[pl.BlockSpec((tm, k), lambda i, j: (i, 0)), w_spec] + list(extra_specs),
        out_specs=out_specs,
        out_shape=out_shape,
        compiler_params=_params("parallel", "arbitrary"),
        name=name,
    )(x, w, *extra)


PROJ_TM = 1024
PROJ_TN = 1024


def _proj_tn(m):
    return _tile(m, PROJ_TN, V7X_LANES)


def _cast_epilogue(acc, extra, outs):
    outs[0][...] = acc.astype(outs[0].dtype)


def _mm(x, w, col0, m, out_dtype, name, w_t=False):
    n = x.shape[0]
    tm, tn = _tile(n, PROJ_TM), _proj_tn(m)
    return _proj(x, w, col0, m, tm=tm, tn=tn, epilogue=_cast_epilogue, extra=(), extra_specs=(),
                 out_shape=jax.ShapeDtypeStruct((n, m), out_dtype),
                 out_specs=pl.BlockSpec((tm, tn), lambda i, j: (i, j)), name=name, w_t=w_t)


def _swap_halves(x, head_w):
    if head_w == V7X_LANES:
        return pltpu.roll(x, V7X_LANES // 2, axis=1)
    half = head_w // 2
    lane = lax.broadcasted_iota(jnp.int32, x.shape, 1)
    take_upper = (lane & (head_w - 1)) < half
    return jnp.where(take_upper, pltpu.roll(x, V7X_LANES - half, axis=1), pltpu.roll(x, half, axis=1))


def _rope_cols(acc, cos, sin, head_w):
    out = []
    for c in range(acc.shape[1] // V7X_LANES):
        x = acc[:, c * V7X_LANES:(c + 1) * V7X_LANES]
        out.append(x * cos + _swap_halves(x, head_w) * sin)
    return out


def _rope_q_epilogue(acc, extra, outs, *, head_w, out_scale):
    cos_ref, sin_ref = extra
    o_ref = outs[0]
    cols = _rope_cols(acc, cos_ref[...], sin_ref[...], head_w)
    per = V7X_LANES // head_w
    for c, x in enumerate(cols):
        if out_scale != 1.0:
            x = x * out_scale
        for p in range(per):
            o_ref[c * per + p] = x[:, p * head_w:(p + 1) * head_w].astype(o_ref.dtype)


Q_SCALE = HEAD_DIM ** -0.5 * LOG2_E


def _mm_rope_q(x, w, col0, m, cos, sin, head_w, name, w_t=False, out_scale=1.0):
    n = x.shape[0]
    tm, tn = _tile(min(n, cos.shape[0]), PROJ_TM), _proj_tn(m)
    hpt = tn // head_w
    table = _table_spec(cos, tm)
    return _proj(x, w, col0, m, tm=tm, tn=tn, epilogue=functools.partial(_rope_q_epilogue, head_w=head_w, out_scale=out_scale),
                 extra=(cos, sin), extra_specs=(table, table),
                 out_shape=jax.ShapeDtypeStruct((m // head_w, n, head_w), BF16),
                 out_specs=pl.BlockSpec((hpt, tm, head_w), lambda i, j: (j, i, 0)), name=name, w_t=w_t)


def _table_spec(table, tm):
    period = table.shape[0] // tm
    assert period * tm == table.shape[0]
    return pl.BlockSpec((tm, V7X_LANES), lambda i, j: (i % period, 0))


KVX_TM = 512


def _kvx_kernel(h_ref, wkv_ref, wx_ref, cos_ref, sin_ref, cos_x_ref, sin_x_ref,
                k32_ref, k16_ref, v32_ref, v16_ref, x32_ref, x16_ref):
    h = h_ref[...]
    kv = _dot_nt(h, wkv_ref[...])
    tm = kv.shape[0]
    for c, k in enumerate(_rope_cols(kv[:, :D_KV], cos_ref[...], sin_ref[...], HEAD_DIM)):
        k32_ref[pl.ds(c, tm, stride=N_KV_HEADS), :] = k
        k16_ref[:, c * V7X_LANES:(c + 1) * V7X_LANES] = k.astype(BF16)
    v = kv[:, D_KV:]
    for c in range(N_KV_HEADS):
        v32_ref[pl.ds(c, tm, stride=N_KV_HEADS), :] = v[:, c * HEAD_DIM:(c + 1) * HEAD_DIM]
    v16_ref[...] = v.astype(BF16)
    (x,) = _rope_cols(_dot_nt(h, wx_ref[...]), cos_x_ref[...], sin_x_ref[...], IDX_DIM)
    x32_ref[...] = x
    x16_ref[...] = x.astype(BF16)


def _mm_kvx(x, w_t, row_kv, row_x, cos, sin, cos_x, sin_x):
    n, kdim = x.shape
    assert row_x + V7X_LANES <= w_t.shape[0]
    tm = _tile(min(n, cos.shape[0]), KVX_TM)
    table = _table_spec(cos, tm)
    kv32 = (N_KV_HEADS, HEAD_DIM)
    layouts = [kv32, (1, D_KV), kv32, (1, D_KV), (1, V7X_LANES), (1, V7X_LANES)]
    dtypes = [F32, BF16] * 3
    shapes = [jax.ShapeDtypeStruct((n * r, wd), dt) for (r, wd), dt in zip(layouts, dtypes)]
    specs = [pl.BlockSpec((tm * r, wd), lambda i, j: (i, 0)) for r, wd in layouts]
    return pl.pallas_call(
        _kvx_kernel,
        grid=(n // tm, 1),
        in_specs=[pl.BlockSpec((tm, kdim), lambda i, j: (i, 0)),
                  _rows_spec(2 * D_KV, kdim, lambda j: row_kv),
                  _rows_spec(V7X_LANES, kdim, lambda j: row_x)] + [table] * 4,
        out_specs=specs,
        out_shape=shapes,
        compiler_params=_params("parallel", "arbitrary"),
        name="in_kvx",
    )(x, w_t, w_t, cos, sin, cos_x, sin_x)


PREV_ROWS = 16


def _conv_prompt_kernel(xin_ref, bg_ref, cg_ref, xin_prev_ref, cg_prev_ref, w_ref, u_ref, st_ref,
                        *, blocks_per_seq):
    i = pl.program_id(1)
    p = cg_ref[...].astype(F32) * xin_ref[...].astype(F32)
    prev = cg_prev_ref[...].astype(F32) * xin_prev_ref[...].astype(F32)
    prev = jnp.where(i % blocks_per_seq == 0, 0.0, prev)
    tm = p.shape[0]
    last = PREV_ROWS - 1
    row = lax.broadcasted_iota(jnp.int32, p.shape, 0)
    p1 = pltpu.roll(p, 1, axis=0)
    p1 = jnp.where(row == 0, prev[last:last + 1], p1)
    p2 = pltpu.roll(p, 2, axis=0)
    p2 = jnp.where(row == 0, prev[last - 1:last], jnp.where(row == 1, prev[last:last + 1], p2))
    w = w_ref[...]
    y = p2 * w[0:1] + p1 * w[1:2] + p * w[2:3]
    u_ref[...] = (bg_ref[...].astype(F32) * y).astype(u_ref.dtype)
    st_ref[...] = p[tm - (CONV_W - 1):, :]


def _conv_prompt(zc, conv_w, n_seq, seq):
    n = zc.shape[0]
    dc = zc.shape[1] // 3
    tm = _tile(seq, 512)
    tc = _tile(dc, 512, V7X_LANES)
    ncb = dc // tc
    bps = seq // tm
    prev_row_blk = lambda i: jnp.maximum(i * (tm // PREV_ROWS) - 1, 0)
    u, st = pl.pallas_call(
        functools.partial(_conv_prompt_kernel, blocks_per_seq=bps),
        grid=(ncb, n // tm),
        in_specs=[pl.BlockSpec((tm, tc), lambda j, i: (i, j)),
                  pl.BlockSpec((tm, tc), lambda j, i: (i, ncb + j)),
                  pl.BlockSpec((tm, tc), lambda j, i: (i, 2 * ncb + j)),
                  pl.BlockSpec((PREV_ROWS, tc), lambda j, i: (prev_row_blk(i), j)),
                  pl.BlockSpec((PREV_ROWS, tc), lambda j, i: (prev_row_blk(i), 2 * ncb + j)),
                  pl.BlockSpec((CONV_W, tc), lambda j, i: (0, j))],
        out_specs=[pl.BlockSpec((tm, tc), lambda j, i: (i, j)),
                   pl.BlockSpec((None, CONV_W - 1, tc), lambda j, i: (i // bps, 0, j))],
        out_shape=[jax.ShapeDtypeStruct((n, dc), BF16),
                   jax.ShapeDtypeStruct((n_seq, CONV_W - 1, dc), F32)],
        compiler_params=_params("parallel", "arbitrary"),
        name="conv_prompt",
    )(zc, zc, zc, zc, zc, conv_w)
    return u, st


def _conv_sample_kernel(xin_ref, bg_ref, cg_ref, st_ref, w_ref, u_ref, nst_ref):
    t_len = xin_ref.shape[0]
    w = w_ref[...]
    ext = [st_ref[j] for j in range(CONV_W - 1)]
    ext += [cg_ref[t].astype(F32) * xin_ref[t].astype(F32) for t in range(t_len)]
    for t in range(t_len):
        y = ext[t] * w[0:1]
        for j in range(1, CONV_W):
            y = y + ext[t + j] * w[j:j + 1]
        u_ref[t] = (bg_ref[t].astype(F32) * y).astype(u_ref.dtype)
    for j in range(CONV_W - 1):
        nst_ref[j] = ext[t_len + j]


def _conv_sample(zc_t, st_t, conv_w):
    t_len, b, dc3 = zc_t.shape
    dc = dc3 // 3
    tc = _tile(dc, 512, V7X_LANES)
    ncb = dc // tc
    return pl.pallas_call(
        _conv_sample_kernel,
        grid=(ncb,),
        in_specs=[pl.BlockSpec((t_len, b, tc), lambda j: (0, 0, j)),
                  pl.BlockSpec((t_len, b, tc), lambda j: (0, 0, ncb + j)),
                  pl.BlockSpec((t_len, b, tc), lambda j: (0, 0, 2 * ncb + j)),
                  pl.BlockSpec((CONV_W - 1, b, tc), lambda j: (0, 0, j)),
                  pl.BlockSpec((CONV_W, tc), lambda j: (0, j))],
        out_specs=[pl.BlockSpec((t_len, b, tc), lambda j: (0, 0, j)),
                   pl.BlockSpec((CONV_W - 1, b, tc), lambda j: (0, 0, j))],
        out_shape=[jax.ShapeDtypeStruct((t_len, b, dc), BF16),
                   jax.ShapeDtypeStruct((CONV_W - 1, b, dc), F32)],
        compiler_params=_params("parallel"),
        name="conv_sample",
    )(zc_t, zc_t, zc_t, st_t, conv_w)


_SUBLANES = 8
_REDUCERS = {jnp.add: jnp.sum, jnp.maximum: jnp.max, jnp.minimum: jnp.min}


def _fold_chunks(n_chunks, term, combine, init, shape, axis):
    if axis == 0 and shape[0] > _SUBLANES:
        full_term = term
        shape = (_SUBLANES, shape[1])
        term = lambda c: _REDUCERS[combine](full_term(c).reshape(-1, *shape), axis=0)
    if isinstance(n_chunks, int):
        vals = [term(c) for c in range(n_chunks)]
        while len(vals) > 1:
            vals = [combine(vals[i], vals[i + 1]) if i + 1 < len(vals) else vals[i]
                    for i in range(0, len(vals), 2)]
        return vals[0]
    return lax.fori_loop(0, n_chunks, lambda c, acc: combine(acc, term(c)), jnp.full(shape, init, F32))


def _topk_threshold(score_at, index_at, n_chunks, n_keys_max, want, shape, axis):
    fold = functools.partial(_fold_chunks, n_chunks, shape=shape, axis=axis)

    def count_ge(thr):
        part = fold(lambda c: jnp.where(score_at(c) >= thr, 1.0, 0.0), jnp.add, 0.0)
        return jnp.sum(part, axis=axis, keepdims=True)

    hi = jnp.max(fold(score_at, jnp.maximum, NEG_INF), axis=axis, keepdims=True)
    lo = jnp.min(fold(lambda c: jnp.where(score_at(c) == NEG_INF, jnp.inf, score_at(c)), jnp.minimum, jnp.inf),
                 axis=axis, keepdims=True)
    cnt_lo = count_ge(lo)
    cnt_hi = count_ge(hi)
    top_ok = cnt_hi >= want
    lo = jnp.where(top_ok, hi, lo)
    cnt_lo = jnp.where(top_ok, cnt_hi, cnt_lo)

    def bisect(_, carry):
        lo, hi, cnt_lo = carry
        mid = lo + (hi - lo) * 0.5
        cnt = count_ge(mid)
        ok = cnt >= want
        return (jnp.where(ok, mid, lo), jnp.where(ok, hi, mid), jnp.where(ok, cnt, cnt_lo))

    lo, hi, cnt_lo = lax.fori_loop(0, BISECT_STEPS, bisect, (lo, hi, cnt_lo))

    def pending_of(cnt_lo, done):
        return jnp.max(jnp.where((cnt_lo > want) & (done == 0.0), 1.0, 0.0))

    def finish(carry):
        lo, hi, cnt_lo, done, _ = carry
        cand = jnp.max(fold(lambda c: jnp.where(score_at(c) < hi, score_at(c), NEG_INF), jnp.maximum, NEG_INF),
                       axis=axis, keepdims=True)
        cnt = count_ge(cand)
        active = (cnt_lo > want) & (done == 0.0)
        ok = active & (cnt >= want)
        lo = jnp.where(ok, cand, lo)
        cnt_lo = jnp.where(ok, cnt, cnt_lo)
        done = jnp.where(ok, 1.0, done)
        hi = jnp.where(active & (cnt < want), cand, hi)
        return lo, hi, cnt_lo, done, pending_of(cnt_lo, done)

    done0 = jnp.where(lo == hi, 1.0, 0.0)
    lo, _, cnt_lo, *_ = lax.while_loop(lambda carry: carry[4] > 0.0, finish,
                                       (lo, hi, cnt_lo, done0, pending_of(cnt_lo, done0)))

    excess = cnt_lo - want
    tied = jnp.max(jnp.where(excess > 0.0, 1.0, 0.0))

    def count_gt(carry):
        part = fold(lambda c: jnp.where(score_at(c) > lo, 1.0, 0.0), jnp.add, 0.0)
        return jnp.sum(part, axis=axis, keepdims=True), 0.0

    cnt_gt, _ = lax.while_loop(lambda carry: carry[1] > 0.0, count_gt, (jnp.zeros_like(lo), tied))
    need = want - cnt_gt

    def tie_step(carry):
        lo_j, hi_j, left = carry
        mid = jnp.floor((lo_j + hi_j) * 0.5)
        part = fold(lambda c: jnp.where((score_at(c) == lo) & (index_at(c) <= mid), 1.0, 0.0), jnp.add, 0.0)
        ok = jnp.sum(part, axis=axis, keepdims=True) >= need
        return jnp.where(ok, lo_j, mid), jnp.where(ok, mid, hi_j), left - 1.0

    steps = float(max(1, (n_keys_max - 1).bit_length() + 1))
    _, hi_j, _ = lax.while_loop(lambda carry: carry[2] > 0.0, tie_step,
                                (jnp.full_like(lo, -1.0), jnp.full_like(lo, n_keys_max - 1.0), tied * steps))
    tie_j = jnp.where(excess > 0.0, hi_j, jnp.inf)
    return lo, tie_j


def _selected(score, index, lo, tie_j):
    return (score > lo) | ((score == lo) & (index <= tie_j))


ATTN_TQ = 256
ATTN_KC = 256


def _attn_prompt_kernel(qi_ref, wi_ref, ki_ref, q_ref, k_ref, vt_ref, o_ref,
                        score_ref, m_ref, l_ref, acc_ref, *, tq, kc, k_top):
    qb = pl.program_id(1)
    n_chunks = (qb * tq) // kc + tq // kc
    qpos = qb * tq + lax.broadcasted_iota(jnp.int32, (1, tq), 1)
    wi = wi_ref[...]

    def fill(c, _):
        kic = ki_ref[pl.ds(pl.multiple_of(c * kc, kc), kc), :]
        acc = jnp.zeros((kc, tq), F32)
        for h in range(N_IDX_HEADS):
            acc = acc + jnp.maximum(_dot_nt(kic, qi_ref[h]), 0.0) * wi[h:h + 1]
        kpos = c * kc + lax.broadcasted_iota(jnp.int32, (kc, tq), 0)
        score_ref[c] = jnp.where(kpos <= qpos, acc, NEG_INF)
        return 0

    lax.fori_loop(0, n_chunks, fill, 0)

    want = jnp.minimum(qpos + 1, k_top).astype(F32)
    def key_index(c):
        return (c * kc + lax.broadcasted_iota(jnp.int32, (kc, tq), 0)).astype(F32)

    n_keys_max = score_ref.shape[0] * kc
    thr, tie_j = _topk_threshold(lambda c: score_ref[c], key_index, n_chunks, n_keys_max, want, (kc, tq), axis=0)

    def to_bias(c, _):
        score_ref[c] = jnp.where(_selected(score_ref[c], key_index(c), thr, tie_j), 0.0, NEG_INF)
        return 0

    lax.fori_loop(0, n_chunks, to_bias, 0)

    for g in range(N_KV_HEADS):
        m_ref[...] = jnp.full(m_ref.shape, NEG_INF, F32)
        l_ref[...] = jnp.zeros(l_ref.shape, F32)
        acc_ref[...] = jnp.zeros(acc_ref.shape, F32)

        heads = range(KV_GROUP)

        def scores(c):
            kch = k_ref[pl.ds(pl.multiple_of(c * kc, kc), kc), g * HEAD_DIM:(g + 1) * HEAD_DIM]
            bias = score_ref[c]
            return [_dot_nt(kch, q_ref[g * KV_GROUP + r]) + bias for r in heads]

        def update(c, s):
            vt = vt_ref[c, g * HEAD_DIM:(g + 1) * HEAD_DIM, :]
            m_old = [m_ref[r] for r in heads]
            l_old = [l_ref[r] for r in heads]
            m_new = [jnp.maximum(m_old[r], jnp.max(s[r], axis=0, keepdims=True)) for r in heads]
            m_safe = [jnp.where(m_new[r] == NEG_INF, 0.0, m_new[r]) for r in heads]
            p = [jnp.exp2(s[r] - m_safe[r]) for r in heads]
            alpha = [jnp.exp2(m_old[r] - m_safe[r]) for r in heads]
            pv = [_dot(vt, p[r].astype(BF16)) for r in heads]
            for r in heads:
                l_ref[r] = alpha[r] * l_old[r] + jnp.sum(p[r], axis=0, keepdims=True)
                acc_ref[r] = alpha[r] * acc_ref[r] + pv[r]
                m_ref[r] = m_new[r]

        def attend(first, count):
            s = [scores(first + u) for u in range(count)]
            for u in range(count):
                update(first + u, s[u])

        def attend_four(i, _):
            attend(4 * i, 4)
            return 0

        lax.fori_loop(0, n_chunks // 4, attend_four, 0)
        rest = n_chunks % 4
        done = n_chunks - rest

        @pl.when(rest >= 2)
        def _():
            attend(done, 2)

        @pl.when(rest % 2 == 1)
        def _():
            attend(n_chunks - 1, 1)
        for r in range(KV_GROUP):
            h = g * KV_GROUP + r
            out = acc_ref[r] / l_ref[r]
            o_ref[:, h * HEAD_DIM:(h + 1) * HEAD_DIM] = out.T.astype(o_ref.dtype)


def _attn_prompt(qi_hm, wi_t, ki16, q_hm, k16, vt16, n_seq, seq, k_top):
    n = n_seq * seq
    tq = _tile(seq, ATTN_TQ)
    kc = _tile(tq, ATTN_KC)
    nqb = seq // tq
    assert vt16.shape == (n_seq, seq // kc, D_KV, kc)
    kernel = functools.partial(_attn_prompt_kernel, tq=tq, kc=kc, k_top=k_top)
    return pl.pallas_call(
        kernel,
        grid=(n_seq, nqb),
        in_specs=[pl.BlockSpec((N_IDX_HEADS, tq, IDX_DIM), lambda b, q: (0, b * nqb + q, 0)),
                  pl.BlockSpec((N_IDX_HEADS, tq), lambda b, q: (0, b * nqb + q)),
                  pl.BlockSpec((seq, IDX_DIM), lambda b, q: (b, 0)),
                  pl.BlockSpec((N_HEADS, tq, HEAD_DIM), lambda b, q: (0, b * nqb + q, 0)),
                  pl.BlockSpec((seq, D_KV), lambda b, q: (b, 0)),
                  pl.BlockSpec((None, seq // kc, D_KV, kc), lambda b, q: (b, 0, 0, 0))],
        out_specs=pl.BlockSpec((tq, D_ATTN), lambda b, q: (b * nqb + q, 0)),
        out_shape=jax.ShapeDtypeStruct((n, D_ATTN), BF16),
        scratch_shapes=[pltpu.VMEM((seq // kc, kc, tq), F32),
                        pltpu.VMEM((KV_GROUP, 1, tq), F32),
                        pltpu.VMEM((KV_GROUP, 1, tq), F32),
                        pltpu.VMEM((KV_GROUP, HEAD_DIM, tq), F32)],
        compiler_params=_params("parallel", "arbitrary"),
        name="attn_prompt",
    )(qi_hm, wi_t, ki16, q_hm, k16, vt16)


def _merge_kernel(u_ref, a_ref, wc_ref, wa_ref, gc_ref, ga_ref, o_ref):
    yc = _dot(u_ref[...], wc_ref[...])
    ya = _dot(a_ref[...], wa_ref[...])
    gc = jax.nn.sigmoid(gc_ref[...].astype(F32))
    ga = jax.nn.sigmoid(ga_ref[...].astype(F32))
    o_ref[...] = (gc * yc + ga * ya).astype(o_ref.dtype)


def _merge(u, attn, wc, wa, gates):
    n, dc = u.shape
    da = attn.shape[1]
    d = wc.shape[1]
    tm = _tile(n, PROJ_TM)
    tn = _proj_tn(d)
    ncb = d // tn
    return pl.pallas_call(
        _merge_kernel,
        grid=(n // tm, ncb),
        in_specs=[pl.BlockSpec((tm, dc), lambda i, j: (i, 0)),
                  pl.BlockSpec((tm, da), lambda i, j: (i, 0)),
                  pl.BlockSpec((dc, tn), lambda i, j: (0, j)),
                  pl.BlockSpec((da, tn), lambda i, j: (0, j)),
                  pl.BlockSpec((tm, tn), lambda i, j: (i, j)),
                  pl.BlockSpec((tm, tn), lambda i, j: (i, ncb + j))],
        out_specs=pl.BlockSpec((tm, tn), lambda i, j: (i, j)),
        out_shape=jax.ShapeDtypeStruct((n, d), BF16),
        compiler_params=_params("parallel", "arbitrary"),
        name="merge",
    )(u, attn, wc, wa, gates, gates)


def _out_proj_epilogue(acc, extra, outs):
    x_ref, gt_ref = extra
    outs[0][...] = x_ref[...] + gt_ref[...] * acc


def _out_proj(a, w, x, mods3, rows):
    n = a.shape[0]
    d = w.shape[1]
    tm = _tile(min(n, rows.rows_per_group if not rows.per_token else n), PROJ_TM)
    tn = _proj_tn(d)
    tile = pl.BlockSpec((tm, tn), lambda i, j: (i, j))
    gate = rows.mod_spec(tm, d, 2, d // tn, tiled=True)
    return _proj(a, w, 0, d, tm=tm, tn=tn, epilogue=_out_proj_epilogue, extra=(x, mods3), extra_specs=(tile, gate),
                 out_shape=jax.ShapeDtypeStruct((n, d), F32), out_specs=tile, name="out_proj")


def _ffn_kernel(x_ref, g_ref, sc_ref, sh_ref, gt_ref, gf_ref, wu_ref, wd_ref, o_ref, h_ref):
    f = pl.program_id(1)

    @pl.when(f == 0)
    def _():
        h_ref[...] = _norm_mod(x_ref[...], g_ref[...], sc_ref[...], sh_ref[...]).astype(BF16)
        o_ref[...] = jnp.zeros(o_ref.shape, F32)

    u = jnp.maximum(_dot(h_ref[...], wu_ref[...]), 0.0)
    o_ref[...] += _dot((u * u).astype(BF16), wd_ref[...])

    @pl.when(f == pl.num_programs(1) - 1)
    def _():
        x2 = x_ref[...] + gt_ref[...] * o_ref[...]
        ms = jnp.mean(x2 * x2, axis=-1, keepdims=True)
        o_ref[...] = x2 * lax.rsqrt(ms + NORM_EPS) * gf_ref[...]


def _ffn(x1, g_ffn, g_final, mods3, rows, w_up, w_down):
    n, d = x1.shape
    dff = w_up.shape[1]
    tm = _tile(min(n, rows.rows_per_group if not rows.per_token else n), 512)
    tf = _tile(dff, 512, V7X_LANES)
    return pl.pallas_call(
        _ffn_kernel,
        grid=(n // tm, dff // tf),
        in_specs=[pl.BlockSpec((tm, d), lambda i, f: (i, 0), pipeline_mode=pl.Buffered(1)),
                  pl.BlockSpec((1, d), lambda i, f: (0, 0)),
                  rows.mod_spec(tm, d, 4),
                  rows.mod_spec(tm, d, 3),
                  rows.mod_spec(tm, d, 5),
                  pl.BlockSpec((1, d), lambda i, f: (0, 0)),
                  pl.BlockSpec((d, tf), lambda i, f: (0, f)),
                  pl.BlockSpec((tf, d), lambda i, f: (f, 0))],
        out_specs=pl.BlockSpec((tm, d), lambda i, f: (i, 0)),
        out_shape=jax.ShapeDtypeStruct((n, d), F32),
        scratch_shapes=[pltpu.VMEM((tm, d), BF16)],
        compiler_params=_params("parallel", "arbitrary"),
        name="ffn",
    )(x1, g_ffn.reshape(1, d), mods3, mods3, mods3, g_final.reshape(1, d), w_up, w_down)


QPAD = 8
INDEX_PAGES_PER_DOT = 8


def _sample_index_kernel(pt_ref, qi_ref, wi_ref, kin_ref, expand_ref, cache_ref, bias_ref,
                         kbuf_ref, sem_ref, score_ref, *, n_pages, page, t_len, k_top):
    b = pl.program_id(0)

    slot = b % 2

    def page_copy(bb, p):
        return pltpu.make_async_copy(cache_ref.at[pt_ref[bb, p]], kbuf_ref.at[bb % 2, p], sem_ref.at[bb % 2, p])

    @pl.when(b == 0)
    def _():
        for p in range(n_pages):
            page_copy(0, p).start()

    @pl.when(b + 1 < pl.num_programs(0))
    def _():
        for p in range(n_pages):
            page_copy(b + 1, p).start()

    qi = qi_ref[...]
    wi = wi_ref[...].reshape(N_IDX_HEADS, QPAD, 1)

    def head_sum(s):
        s3 = jnp.maximum(s, 0.0).reshape(N_IDX_HEADS, QPAD, s.shape[1]) * wi
        return jnp.sum(s3, axis=0)

    ppd = INDEX_PAGES_PER_DOT if n_pages % INDEX_PAGES_PER_DOT == 0 else 1
    for p0 in range(0, n_pages, ppd):
        for p in range(p0, p0 + ppd):
            page_copy(b, p).wait()
        kt = jnp.concatenate([kbuf_ref[slot, p] for p in range(p0, p0 + ppd)], axis=1).astype(BF16)
        sc = head_sum(_dot(qi, kt))
        for j in range(ppd):
            score_ref[p0 + j] = sc[:, j * page:(j + 1) * page]

    qrow = lax.broadcasted_iota(jnp.int32, (QPAD, page), 0)
    lane = lax.broadcasted_iota(jnp.int32, (QPAD, page), 1)
    s_new = head_sum(_dot(qi, kin_ref[...]))
    score_ref[n_pages] = jnp.where((lane <= qrow) & (lane < t_len), s_new, NEG_INF)

    n_chunks = n_pages + 1
    lane_f = lane.astype(F32)
    key_index = lambda c: lane_f + float(c * page)
    is_query = lax.broadcasted_iota(jnp.int32, (QPAD, 1), 0) < t_len
    want = jnp.where(is_query, float(k_top), float(n_pages * page + t_len))
    thr, tie_j = _topk_threshold(lambda c: score_ref[c], key_index, n_chunks, n_chunks * page, want,
                                 (QPAD, page), axis=1)

    sel = [jnp.where(_selected(score_ref[c], key_index(c), thr, tie_j), 1.0, 0.0) for c in range(n_chunks)]
    sel += [jnp.zeros((QPAD, page), F32)] * (n_chunks % 2)
    hit = _dot(jnp.concatenate(sel, axis=0).astype(BF16), expand_ref[...])
    bias = jnp.where(hit > 0.5, 0.0, NEG_INF)
    for c in range(n_chunks):
        bias_ref[c] = bias[c * QPAD:(c + 1) * QPAD]


def _sample_index(page_table, qi_s, wi_s, ki_new_t, expand, cache_kidx_t, t_len, k_top):
    bsz, n_pages = page_table.shape
    page = cache_kidx_t.shape[2]
    kernel = functools.partial(_sample_index_kernel, n_pages=n_pages, page=page, t_len=t_len, k_top=k_top)
    grid_spec = pltpu.PrefetchScalarGridSpec(
        num_scalar_prefetch=1,
        grid=(bsz,),
        in_specs=[pl.BlockSpec((None, N_IDX_HEADS * QPAD, IDX_DIM), lambda b, pt: (b, 0, 0)),
                  pl.BlockSpec((None, N_IDX_HEADS * QPAD, 1), lambda b, pt: (b, 0, 0)),
                  pl.BlockSpec((None, IDX_DIM, page), lambda b, pt: (b, 0, 0)),
                  pl.BlockSpec(expand.shape, lambda b, pt: (0, 0)),
                  pl.BlockSpec(memory_space=pl.ANY)],
        out_specs=pl.BlockSpec((None, n_pages + 1, QPAD, N_KV_HEADS * page), lambda b, pt: (b, 0, 0, 0)),
        scratch_shapes=[pltpu.VMEM((2, n_pages, IDX_DIM, page), F32),
                        pltpu.SemaphoreType.DMA((2, n_pages)),
                        pltpu.VMEM((n_pages + 1, QPAD, page), F32)],
    )
    return pl.pallas_call(
        kernel,
        grid_spec=grid_spec,
        out_shape=jax.ShapeDtypeStruct((bsz, n_pages + 1, QPAD, N_KV_HEADS * page), F32),
        compiler_params=_params("arbitrary"),
        name="sample_index",
    )(page_table, qi_s, wi_s, ki_new_t, expand, cache_kidx_t)


PAGES_PER_CHUNK = 8
KV_SLOTS = 4


def _sample_attn_kernel(pt_ref, q_ref, bias_ref, gmask_ref, kn_ref, vn_ref, ck_ref, cv_ref, o_ref,
                        kbuf_ref, vbuf_ref, sem_ref, m_ref, l_ref, acc_ref, *, n_pages, ppc):
    b = pl.program_id(0)
    n_chunks = n_pages // ppc
    rows = N_HEADS * QPAD
    prow = kbuf_ref.shape[2]

    n_total = pl.num_programs(0) * n_chunks

    def copies(g):
        bb, c, slot = g // n_chunks, g % n_chunks, g % KV_SLOTS
        out = []
        for j in range(ppc):
            pid = pt_ref[bb, c * ppc + j]
            out.append(pltpu.make_async_copy(ck_ref.at[pid], kbuf_ref.at[slot, j], sem_ref.at[0, slot, j]))
            out.append(pltpu.make_async_copy(cv_ref.at[pid], vbuf_ref.at[slot, j], sem_ref.at[1, slot, j]))
        return out

    def start(g):
        @pl.when(g < n_total)
        def _():
            for cp in copies(g):
                cp.start()

    @pl.when(b == 0)
    def _():
        for g in range(KV_SLOTS - 1):
            start(g)

    q = q_ref[...]
    gmask = gmask_ref[...]
    m_ref[...] = jnp.full(m_ref.shape, NEG_INF, F32)
    l_ref[...] = jnp.zeros(l_ref.shape, F32)
    acc_ref[...] = jnp.zeros(acc_ref.shape, F32)

    def attend(kch, vch, biases):
        s = _dot_nt(q, kch)
        parts = []
        for j, bias in enumerate(biases):
            sj = s[:, j * prow:(j + 1) * prow].reshape(N_HEADS, QPAD, prow) + bias[None]
            parts.append(sj.reshape(rows, prow) + gmask)
        s = parts[0] if len(parts) == 1 else jnp.concatenate(parts, axis=1)
        m_old = m_ref[...]
        m_new = jnp.maximum(m_old, jnp.max(s, axis=1, keepdims=True))
        m_safe = jnp.where(m_new == NEG_INF, 0.0, m_new)
        p = jnp.exp2(s - m_safe)
        alpha = jnp.exp2(m_old - m_safe)
        l_ref[...] = alpha * l_ref[...] + jnp.sum(p, axis=1, keepdims=True)
        acc_ref[...] = alpha * acc_ref[...] + _dot(p.astype(BF16), vch)
        m_ref[...] = m_new

    def chunk_body(c, _):
        g = b * n_chunks + c
        slot = g % KV_SLOTS
        start(g + KV_SLOTS - 1)
        for cp in copies(g):
            cp.wait()
        kch = kbuf_ref[slot].reshape(ppc * prow, HEAD_DIM).astype(BF16)
        vch = vbuf_ref[slot].reshape(ppc * prow, HEAD_DIM).astype(BF16)
        attend(kch, vch, [bias_ref[c * ppc + j] for j in range(ppc)])
        return 0

    lax.fori_loop(0, n_chunks, chunk_body, 0)
    attend(kn_ref[...], vn_ref[...], [bias_ref[n_pages]])
    o_ref[...] = acc_ref[...] / l_ref[...]


def _sample_attn(page_table, q_s, bias, gmask, k_new, v_new, cache_k, cache_v):
    bsz, n_pages = page_table.shape
    prow = cache_k.shape[1]
    ppc = PAGES_PER_CHUNK if n_pages % PAGES_PER_CHUNK == 0 else 1
    rows = N_HEADS * QPAD
    kernel = functools.partial(_sample_attn_kernel, n_pages=n_pages, ppc=ppc)
    grid_spec = pltpu.PrefetchScalarGridSpec(
        num_scalar_prefetch=1,
        grid=(bsz,),
        in_specs=[pl.BlockSpec((None, rows, HEAD_DIM), lambda b, pt: (b, 0, 0)),
                  pl.BlockSpec((None, n_pages + 1, QPAD, prow), lambda b, pt: (b, 0, 0, 0)),
                  pl.BlockSpec((rows, prow), lambda b, pt: (0, 0)),
                  pl.BlockSpec((None, prow, HEAD_DIM), lambda b, pt: (b, 0, 0)),
                  pl.BlockSpec((None, prow, HEAD_DIM), lambda b, pt: (b, 0, 0)),
                  pl.BlockSpec(memory_space=pl.ANY),
                  pl.BlockSpec(memory_space=pl.ANY)],
        out_specs=pl.BlockSpec((None, rows, HEAD_DIM), lambda b, pt: (b, 0, 0)),
        scratch_shapes=[pltpu.VMEM((KV_SLOTS, ppc, prow, HEAD_DIM), F32),
                        pltpu.VMEM((KV_SLOTS, ppc, prow, HEAD_DIM), F32),
                        pltpu.SemaphoreType.DMA((2, KV_SLOTS, ppc)),
                        pltpu.VMEM((rows, 1), F32),
                        pltpu.VMEM((rows, 1), F32),
                        pltpu.VMEM((rows, HEAD_DIM), F32)],
    )
    return pl.pallas_call(
        kernel,
        grid_spec=grid_spec,
        out_shape=jax.ShapeDtypeStruct((bsz, rows, HEAD_DIM), F32),
        compiler_params=_params("arbitrary"),
        name="sample_attn",
    )(page_table, q_s, bias, gmask, k_new, v_new, cache_k, cache_v)


def _rope_tables(pos, head_w):
    half = head_w // 2
    inv = ROPE_THETA ** (-jnp.arange(half, dtype=F32) / half)
    ang = pos.astype(F32)[:, None] * inv[None, :]
    cos, sin = jnp.cos(ang), jnp.sin(ang)
    reps = V7X_LANES // head_w
    cos_t = jnp.tile(jnp.concatenate([cos, cos], axis=1), (1, reps))
    sin_t = jnp.tile(jnp.concatenate([-sin, sin], axis=1), (1, reps))
    return cos_t, sin_t


class _InWeights:
    def __init__(self, w_in, d):
        dc = d // 2
        self.t = w_in.T.astype(BF16)
        self.conv = 0
        self.q = 3 * dc
        self.kv = self.q + D_ATTN
        self.qi = self.kv + 2 * D_KV
        self.kiwi = self.qi + D_QI
        self.gates = self.kiwi + IDX_DIM + N_IDX_HEADS
        assert all(r % BF16_SUBLANE_TILE == 0 for r in (self.q, self.kv, self.qi, self.kiwi, self.gates))


def _in_proj(h, w, pos, d):
    cos128, sin128 = _rope_tables(pos, HEAD_DIM)
    cos64, sin64 = _rope_tables(pos, IDX_DIM)
    lane = jnp.arange(V7X_LANES)
    cos_kw = jnp.where(lane < IDX_DIM, cos64, 1.0)
    sin_kw = jnp.where(lane < IDX_DIM, sin64, 0.0)
    dc3 = 3 * (d // 2)
    zc = _mm(h, w.t, w.conv, dc3, BF16, "in_conv", w_t=True)
    gates = _mm(h, w.t, w.gates, 2 * d, BF16, "in_gates", w_t=True)
    q_hm = _mm_rope_q(h, w.t, w.q, D_ATTN, cos128, sin128, HEAD_DIM, "in_q", w_t=True, out_scale=Q_SCALE)
    qi_hm = _mm_rope_q(h, w.t, w.qi, D_QI, cos64, sin64, IDX_DIM, "in_qi", w_t=True)
    k32, k16, v32, v16, kw32, kw16 = _mm_kvx(h, w.t, w.kv, w.kiwi, cos128, sin128, cos_kw, sin_kw)
    ki32 = kw32[:, :IDX_DIM]
    ki16 = kw16[:, :IDX_DIM]
    wi = kw32[:, IDX_DIM:IDX_DIM + N_IDX_HEADS] * (N_IDX_HEADS ** -0.5) * (IDX_DIM ** -0.5)
    return zc, gates, q_hm, qi_hm, k32, k16, v32, v16, ki32, ki16, wi


def kernel(x_prompt, x_sample, cache_k, cache_v, cache_kidx, state_conv, page_table, c_prompt, c_sample,
           w_ada, b_ada, g_mix, w_in, conv_w, w_conv_out, w_attn_out, w_out, g_ffn, w_up, w_down, g_final):
    depth = w_ada.shape[0]
    assert depth == 1, "single-layer trunk"
    bp, sp, d = x_prompt.shape
    bs, ts, _ = x_sample.shape
    dc = d // 2
    n_pool, page = cache_k.shape[1], cache_k.shape[2]
    n_pages = page_table.shape[1]
    past = n_pages * page
    assert ts <= QPAD and ts <= page

    c_all = jnp.concatenate([c_prompt, c_sample], axis=0)
    n_c = c_all.shape[0]
    c_all = jnp.pad(c_all, ((0, (-n_c) % 8), (0, 0)))
    mods = _ada(c_all, w_ada[0], b_ada[0])
    mods_p = mods[:bp].reshape(bp, 1, N_MOD * d)
    mods_s = jnp.repeat(mods[bp:bp + bs], ts, axis=0).reshape(1, bs * ts, N_MOD * d)
    rows_p = _Rows(bp * sp, sp, per_token=False)
    rows_s = _Rows(bs * ts, bs * ts, per_token=True)

    w = _InWeights(w_in[0], d)
    wc, wa, wo = w_conv_out[0].astype(BF16), w_attn_out[0].astype(BF16), w_out[0].astype(BF16)
    wu16 = w_up[0].astype(BF16)
    wd16 = w_down[0].astype(BF16)

    xp = x_prompt.reshape(bp * sp, d)
    hp = _norm_mod_call(xp, g_mix[0], mods_p, rows_p, 1, 0)
    pos_p = jnp.arange(sp)
    zc, gates, q_hm, qi_hm, k32, k16, v32, v16, ki32, ki16, wi = _in_proj(hp, w, pos_p, d)
    u_p, conv_p = _conv_prompt(zc, conv_w[0], bp, sp)
    k_top_p = min(TOPK_MAX, sp // 4)
    kc_p = _tile(_tile(sp, ATTN_TQ), ATTN_KC)
    vt16 = v16.reshape(bp, sp // kc_p, kc_p, D_KV).transpose(0, 1, 3, 2)
    attn_p = _attn_prompt(qi_hm, wi.T, ki16, q_hm, k16, vt16, bp, sp, k_top_p)
    merged_p = _merge(u_p, attn_p, wc, wa, gates)
    x1_p = _out_proj(merged_p, wo, xp, mods_p, rows_p)
    y_p = _ffn(x1_p, g_ffn[0], g_final, mods_p, rows_p, wu16, wd16)

    ns = bs * ts
    xs = x_sample.reshape(ns, d)
    hs = _norm_mod_call(xs, g_mix[0], mods_s, rows_s, 1, 0)
    pos_s = jnp.tile(past + jnp.arange(ts), bs)
    zc_s, gates_s, q_hm_s, qi_hm_s, ks32, ks16, vs32, vs16, kis32, kis16, wi_s = _in_proj(hs, w, pos_s, d)

    zc_t = zc_s.reshape(bs, ts, 3 * dc).transpose(1, 0, 2)
    st_t = state_conv[0].transpose(1, 0, 2)
    u_t, nst_t = _conv_sample(zc_t, st_t, conv_w[0])
    u_s = u_t.transpose(1, 0, 2).reshape(ns, dc)
    conv_s = nst_t.transpose(1, 0, 2)

    k_top_s = min(TOPK_MAX, (past + ts) // 4)
    pad_q = ((0, 0), (0, 0), (0, QPAD - ts), (0, 0))
    qi_s = jnp.pad(qi_hm_s.reshape(N_IDX_HEADS, bs, ts, IDX_DIM).transpose(1, 0, 2, 3), pad_q)
    qi_s = qi_s.reshape(bs, N_IDX_HEADS * QPAD, IDX_DIM)
    wi_sp = jnp.pad(wi_s.reshape(bs, ts, N_IDX_HEADS).transpose(0, 2, 1), ((0, 0), (0, 0), (0, QPAD - ts)))
    wi_sp = wi_sp.reshape(bs, N_IDX_HEADS * QPAD, 1)
    ki_new_t = jnp.pad(kis16.reshape(bs, ts, IDX_DIM).transpose(0, 2, 1), ((0, 0), (0, 0), (0, page - ts)))
    expand = jnp.repeat(jnp.eye(page, dtype=BF16), N_KV_HEADS, axis=1)
    kidx_t = jnp.swapaxes(cache_kidx.reshape(n_pool, page, IDX_DIM), 1, 2)
    bias_s = _sample_index(page_table, qi_s, wi_sp, ki_new_t, expand, kidx_t, ts, k_top_s)

    q_s = jnp.pad(q_hm_s.reshape(N_HEADS, bs, ts, HEAD_DIM).transpose(1, 0, 2, 3), pad_q)
    q_s = q_s.reshape(bs, N_HEADS * QPAD, HEAD_DIM)
    prow = page * N_KV_HEADS
    col_head = jnp.arange(prow) % N_KV_HEADS
    row_head = jnp.arange(N_HEADS * QPAD) // (KV_GROUP * QPAD)
    gmask = jnp.where(row_head[:, None] == col_head[None, :], 0.0, NEG_INF).astype(F32)
    pad_new = ((0, 0), (0, page - ts), (0, 0), (0, 0))
    k_new = jnp.pad(ks16.reshape(bs, ts, N_KV_HEADS, HEAD_DIM), pad_new).reshape(bs, prow, HEAD_DIM)
    v_new = jnp.pad(vs16.reshape(bs, ts, N_KV_HEADS, HEAD_DIM), pad_new).reshape(bs, prow, HEAD_DIM)
    ck = cache_k.reshape(n_pool, prow, HEAD_DIM)
    cv = cache_v.reshape(n_pool, prow, HEAD_DIM)
    attn_hq = _sample_attn(page_table, q_s, bias_s, gmask, k_new, v_new, ck, cv)
    attn_s = attn_hq.reshape(bs, N_HEADS, QPAD, HEAD_DIM)[:, :, :ts].transpose(0, 2, 1, 3)
    attn_s = attn_s.reshape(ns, D_ATTN).astype(BF16)

    merged_s = _merge(u_s, attn_s, wc, wa, gates_s)
    x1_s = _out_proj(merged_s, wo, xs, mods_s, rows_s)
    y_s = _ffn(x1_s, g_ffn[0], g_final, mods_s, rows_s, wu16, wd16)

    return (y_p.reshape(bp, sp, d),
            y_s.reshape(bs, ts, d),
            k32.reshape(1, bp, sp, N_KV_HEADS, HEAD_DIM),
            v32.reshape(1, bp, sp, N_KV_HEADS, HEAD_DIM),
            ki32.reshape(1, bp, sp, IDX_DIM),
            conv_p.reshape(1, bp, CONV_W - 1, dc),
            ks32.reshape(1, bs, ts, N_KV_HEADS, HEAD_DIM),
            vs32.reshape(1, bs, ts, N_KV_HEADS, HEAD_DIM),
            kis32.reshape(1, bs, ts, IDX_DIM),
            conv_s.reshape(1, bs, CONV_W - 1, dc))
```
